```python
import math
import jax, jax.numpy as jnp
from jax import lax
import numpy as np

D_MODEL = 1024
BATCH = 16
SEQ = 2048
DEPTH = 2
DEC_BATCH = 128
DEC_SEQ = 1
PAST_LEN = 16384
PAGE_SIZE = 128

N_MIXERS = 2
N_MLSTM_LAYERS = (DEPTH + 1) // 2
N_MLA_LAYERS = DEPTH // 2

ML_HEADS = 8
ML_DQK = D_MODEL // 2 // ML_HEADS
ML_DV = D_MODEL // ML_HEADS
ML_CHUNK = 64
ML_IN = 2 * ML_HEADS * ML_DQK + 2 * ML_HEADS * ML_DV + 2 * ML_HEADS

MLA_HEADS = 8
Q_LORA = 384
KV_LORA = 256
QK_NOPE = 128
QK_ROPE = 64
V_DIM = 128
MLA_IN = Q_LORA + KV_LORA + QK_ROPE
MLA_SCALE = (QK_NOPE + QK_ROPE) ** -0.5
ROPE_THETA = 10000.0
Q_BLOCK = 128

D_FF = -(-8 * D_MODEL // (3 * 256)) * 256
EPS = 1e-6

kernel_name = 'hybrid_mlstm_mla_decoder_step'


def rmsnorm(x, g):
    x32 = x.astype(jnp.float32)
    y = x32 * lax.rsqrt(jnp.mean(x32 * x32, axis=-1, keepdims=True) + EPS)
    return (y * g.astype(jnp.float32)).astype(x.dtype)


def swiglu(x, w_gate_up, w_down):
    g, u = jnp.split(x @ w_gate_up, 2, axis=-1)
    return (jax.nn.silu(g) * u) @ w_down


def rope(x, pos):
    d = x.shape[-1]
    inv = ROPE_THETA ** (-jnp.arange(0, d, 2, dtype=jnp.float32) / d)
    ang = pos[:, None] * inv[None, :]
    cos = jnp.cos(ang)[None, :, None, :]
    sin = jnp.sin(ang)[None, :, None, :]
    x32 = x.astype(jnp.float32)
    x1, x2 = x32[..., : d // 2], x32[..., d // 2:]
    return jnp.concatenate([x1 * cos - x2 * sin, x1 * sin + x2 * cos], axis=-1).astype(x.dtype)


def mlstm_chunkwise(q, k, v, i_pre, log_f, C0, n0, m0):
    B, S, H, DK = q.shape
    DV = v.shape[-1]
    L = math.gcd(S, ML_CHUNK)
    NC = S // L

    def to_chunks(a):
        a = a.reshape((B, NC, L) + a.shape[2:])
        return jnp.moveaxis(a, [1, 3], [0, 2])

    qc, kc, vc = (to_chunks(a.astype(jnp.float32)) for a in (q, k, v))
    ic, fc = to_chunks(i_pre), to_chunks(log_f)
    causal = jnp.tril(jnp.ones((L, L), dtype=bool))

    def step(carry, xs):
        C, n, m = carry
        qb, kb, vb, ib, fb = xs
        b = jnp.cumsum(fb, axis=-1)
        log_inter = b + m[..., None]
        log_intra = b[..., :, None] - b[..., None, :] + ib[..., None, :]
        log_intra = jnp.where(causal, log_intra, -jnp.inf)
        m_t = jnp.maximum(log_inter, jnp.max(log_intra, axis=-1))
        w_inter = jnp.exp(log_inter - m_t)
        w_intra = jnp.exp(log_intra - m_t[..., None])
        s = jnp.einsum('bhtd,bhsd->bhts', qb, kb) * w_intra
        num = w_inter[..., None] * jnp.einsum('bhtd,bhde->bhte', qb, C) + jnp.einsum('bhts,bhse->bhte', s, vb)
        den = w_inter * jnp.einsum('bhtd,bhd->bht', qb, n) + jnp.sum(s, axis=-1)
        h = num / jnp.maximum(jnp.abs(den), jnp.exp(-m_t))[..., None]
        m_new = m_t[..., -1]
        bL = b[..., -1]
        w_C = jnp.exp(bL + m - m_new)
        w_s = jnp.exp(bL[..., None] - b + ib - m_new[..., None])
        C_new = w_C[..., None, None] * C + jnp.einsum('bhs,bhsd,bhse->bhde', w_s, kb, vb)
        n_new = w_C[..., None] * n + jnp.einsum('bhs,bhsd->bhd', w_s, kb)
        return (C_new, n_new, m_new), h

    carry0 = (C0.astype(jnp.float32), n0.astype(jnp.float32), m0.astype(jnp.float32))
    (C, n, m), hc = lax.scan(step, carry0, (qc, kc, vc, ic, fc))
    h = jnp.moveaxis(hc, [0, 2], [1, 3]).reshape(B, S, H, DV)
    return h, C, n, m


def mlstm_mixer(x, w_in, b_gates, g_head, w_out, C0, n0, m0):
    B, S, _ = x.shape
    dqk = ML_HEADS * ML_DQK
    dv = ML_HEADS * ML_DV
    q, k, v, o, gates = jnp.split(x @ w_in, [dqk, 2 * dqk, 2 * dqk + dv, 2 * dqk + 2 * dv], axis=-1)
    q = q.reshape(B, S, ML_HEADS, ML_DQK)
    k = k.reshape(B, S, ML_HEADS, ML_DQK) * (ML_DQK ** -0.5)
    v = v.reshape(B, S, ML_HEADS, ML_DV)
    gates = gates.astype(jnp.float32) + b_gates.astype(jnp.float32)
    i_pre = gates[..., :ML_HEADS]
    log_f = jax.nn.log_sigmoid(gates[..., ML_HEADS:])
    h, C, n, m = mlstm_chunkwise(q, k, v, i_pre, log_f, C0, n0, m0)
    h = h * lax.rsqrt(jnp.mean(h * h, axis=-1, keepdims=True) + EPS)
    h = h.reshape(B, S, dv) * g_head.astype(jnp.float32)
    y = (jax.nn.sigmoid(o.astype(jnp.float32)) * h).astype(x.dtype) @ w_out
    return y, C, n, m


def mla_project(x, w_in, g_q, g_kv, w_uq, w_uk, pos):
    B, S, _ = x.shape
    c_q, c_kv, k_r = jnp.split(x @ w_in, [Q_LORA, Q_LORA + KV_LORA], axis=-1)
    q = (rmsnorm(c_q, g_q) @ w_uq).reshape(B, S, MLA_HEADS, QK_NOPE + QK_ROPE)
    q_nope = q[..., :QK_NOPE]
    q_rope = rope(q[..., QK_NOPE:], pos)
    c_kv = rmsnorm(c_kv, g_kv)
    k_r = rope(k_r[:, :, None, :], pos)[:, :, 0, :]
    q_lat = jnp.einsum('bshn,chn->bshc', q_nope, w_uk.reshape(KV_LORA, MLA_HEADS, QK_NOPE))
    return q_lat, q_rope, c_kv, k_r


def mla_output(o_lat, w_uv, w_o):
    B, S = o_lat.shape[:2]
    v = jnp.einsum('bshc,chv->bshv', o_lat, w_uv.reshape(KV_LORA, MLA_HEADS, V_DIM))
    return v.reshape(B, S, MLA_HEADS * V_DIM) @ w_o


def mla_attend_prompt(q_lat, q_rope, c_kv, k_r):
    B, S, H, _ = q_lat.shape
    NB = S // Q_BLOCK
    qlb = jnp.moveaxis(q_lat.reshape(B, NB, Q_BLOCK, H, KV_LORA), 1, 0)
    qrb = jnp.moveaxis(q_rope.reshape(B, NB, Q_BLOCK, H, QK_ROPE), 1, 0)
    kpos = jnp.arange(S)

    def block(args):
        ql, qr, start = args
        s = jnp.einsum('bqhc,bkc->bhqk', ql, c_kv) + jnp.einsum('bqhr,bkr->bhqk', qr, k_r)
        s = s.astype(jnp.float32) * MLA_SCALE
        qpos = start + jnp.arange(Q_BLOCK)
        s = jnp.where(kpos[None, :] <= qpos[:, None], s, -jnp.inf)
        p = jax.nn.softmax(s, axis=-1)
        return jnp.einsum('bhqk,bkc->bqhc', p.astype(c_kv.dtype), c_kv)

    out = lax.map(block, (qlb, qrb, jnp.arange(NB) * Q_BLOCK))
    return jnp.moveaxis(out, 0, 1).reshape(B, S, H, KV_LORA)


def mla_attend_sample(q_lat, q_rope, c_new, kr_new, cache_latent, cache_k_rope, page_table, layer_idx):
    T = q_lat.shape[1]
    past = page_table.shape[1] * cache_latent.shape[2]
    kpos = jnp.arange(past + T)
    qpos = past + jnp.arange(T)
    mask = kpos[None, :] <= qpos[:, None]

    def one_seq(args):
        ql, qr, cn, krn, pages = args
        c_all = jnp.concatenate([cache_latent[layer_idx, pages].reshape(past, KV_LORA), cn], axis=0)
        kr_all = jnp.concatenate([cache_k_rope[layer_idx, pages].reshape(past, QK_ROPE), krn], axis=0)
        s = jnp.einsum('qhc,kc->hqk', ql, c_all) + jnp.einsum('qhr,kr->hqk', qr, kr_all)
        s = jnp.where(mask, s.astype(jnp.float32) * MLA_SCALE, -jnp.inf)
        p = jax.nn.softmax(s, axis=-1)
        return jnp.einsum('hqk,kc->qhc', p.astype(c_all.dtype), c_all)

    return lax.map(one_seq, (q_lat, q_rope, c_new, kr_new, page_table))


def setup_inputs(seed: int = 0) -> dict:
    key = jax.random.key(seed)
    ks = iter(jax.random.split(key, 32))
    nrm = lambda shape, scale: jax.random.normal(next(ks), shape, jnp.float32) * scale
    n_pages = PAST_LEN // PAGE_SIZE
    n_used = DEC_BATCH * n_pages
    n_phys = n_used + n_used // 4
    page_table = jax.random.permutation(next(ks), n_phys)[:n_used].reshape(DEC_BATCH, n_pages).astype(jnp.int32)
    NA, NB = N_MLSTM_LAYERS, N_MLA_LAYERS
    b_i = nrm((NA, ML_HEADS), 0.1)
    b_f = 3.0 + 3.0 * jax.random.uniform(next(ks), (NA, ML_HEADS), jnp.float32)
    return {
        'x_prompt': nrm((BATCH, SEQ, D_MODEL), 1.0),
        'x_sample': nrm((DEC_BATCH, DEC_SEQ, D_MODEL), 1.0),
        'state_mlstm_C': nrm((NA, DEC_BATCH, ML_HEADS, ML_DQK, ML_DV), 0.5),
        'state_mlstm_n': nrm((NA, DEC_BATCH, ML_HEADS, ML_DQK), 0.5),
        'state_mlstm_m': nrm((NA, DEC_BATCH, ML_HEADS), 1.0),
        'cache_latent': nrm((NB, n_phys, PAGE_SIZE, KV_LORA), 1.0),
        'cache_k_rope': nrm((NB, n_phys, PAGE_SIZE, QK_ROPE), 1.0),
        'page_table': page_table,
        'norm_mix': 1.0 + nrm((DEPTH, D_MODEL), 0.01),
        'norm_ffn': 1.0 + nrm((DEPTH, D_MODEL), 0.01),
        'norm_final': 1.0 + nrm((D_MODEL,), 0.01),
        'mlstm_w_in': nrm((NA, D_MODEL, ML_IN), D_MODEL ** -0.5),
        'mlstm_b_gates': jnp.concatenate([b_i, b_f], axis=-1),
        'mlstm_g_head': 1.0 + nrm((NA, ML_HEADS * ML_DV), 0.01),
        'mlstm_w_out': nrm((NA, ML_HEADS * ML_DV, D_MODEL), (ML_HEADS * ML_DV) ** -0.5),
        'mla_w_in': nrm((NB, D_MODEL, MLA_IN), D_MODEL ** -0.5),
        'mla_g_q': 1.0 + nrm((NB, Q_LORA), 0.01),
        'mla_g_kv': 1.0 + nrm((NB, KV_LORA), 0.01),
        'mla_w_uq': nrm((NB, Q_LORA, MLA_HEADS * (QK_NOPE + QK_ROPE)), Q_LORA ** -0.5),
        'mla_w_uk': nrm((NB, KV_LORA, MLA_HEADS * QK_NOPE), KV_LORA ** -0.5),
        'mla_w_uv': nrm((NB, KV_LORA, MLA_HEADS * V_DIM), KV_LORA ** -0.5),
        'mla_w_o': nrm((NB, MLA_HEADS * V_DIM, D_MODEL), (MLA_HEADS * V_DIM) ** -0.5),
        'ffn_w_gate_up': nrm((DEPTH, D_MODEL, 2 * D_FF), D_MODEL ** -0.5),
        'ffn_w_down': nrm((DEPTH, D_FF, D_MODEL), D_FF ** -0.5),
    }


def reference(x_prompt, x_sample, state_mlstm_C, state_mlstm_n, state_mlstm_m, cache_latent, cache_k_rope,
              page_table, norm_mix, norm_ffn, norm_final, mlstm_w_in, mlstm_b_gates, mlstm_g_head, mlstm_w_out,
              mla_w_in, mla_g_q, mla_g_kv, mla_w_uq, mla_w_uk, mla_w_uv, mla_w_o, ffn_w_gate_up, ffn_w_down):
    B, S, _ = x_prompt.shape
    T = x_sample.shape[1]
    past_len = page_table.shape[1] * cache_latent.shape[2]
    pos_p = jnp.arange(S, dtype=jnp.float32)
    pos_s = jnp.arange(T, dtype=jnp.float32) + past_len
    hp, hs = x_prompt, x_sample
    C_p, n_p, m_p, C_s, n_s, m_s = [], [], [], [], [], []
    lat_p, kr_p, lat_s, kr_s = [], [], [], []
    for layer in range(DEPTH):
        j = layer // N_MIXERS
        up = rmsnorm(hp, norm_mix[layer])
        us = rmsnorm(hs, norm_mix[layer])
        if layer % N_MIXERS == 0:
            zC = jnp.zeros((B, ML_HEADS, ML_DQK, ML_DV), jnp.float32)
            zn = jnp.zeros((B, ML_HEADS, ML_DQK), jnp.float32)
            zm = jnp.zeros((B, ML_HEADS), jnp.float32)
            yp, Cp, np_, mp = mlstm_mixer(up, mlstm_w_in[j], mlstm_b_gates[j], mlstm_g_head[j], mlstm_w_out[j], zC, zn, zm)
            ys, Cs, ns_, ms = mlstm_mixer(us, mlstm_w_in[j], mlstm_b_gates[j], mlstm_g_head[j], mlstm_w_out[j],
                                          state_mlstm_C[j], state_mlstm_n[j], state_mlstm_m[j])
            C_p.append(Cp); n_p.append(np_); m_p.append(mp)
            C_s.append(Cs); n_s.append(ns_); m_s.append(ms)
        else:
            qlp, qrp, cp, krp = mla_project(up, mla_w_in[j], mla_g_q[j], mla_g_kv[j], mla_w_uq[j], mla_w_uk[j], pos_p)
            yp = mla_output(mla_attend_prompt(qlp, qrp, cp, krp), mla_w_uv[j], mla_w_o[j])
            qls, qrs, cs, krs = mla_project(us, mla_w_in[j], mla_g_q[j], mla_g_kv[j], mla_w_uq[j], mla_w_uk[j], pos_s)
            ys = mla_output(mla_attend_sample(qls, qrs, cs, krs, cache_latent, cache_k_rope, page_table, j),
                            mla_w_uv[j], mla_w_o[j])
            lat_p.append(cp); kr_p.append(krp); lat_s.append(cs); kr_s.append(krs)
        hp = hp + yp
        hs = hs + ys
        hp = hp + swiglu(rmsnorm(hp, norm_ffn[layer]), ffn_w_gate_up[layer], ffn_w_down[layer])
        hs = hs + swiglu(rmsnorm(hs, norm_ffn[layer]), ffn_w_gate_up[layer], ffn_w_down[layer])
    y_prompt = rmsnorm(hp, norm_final)
    y_sample = rmsnorm(hs, norm_final)
    return (y_prompt, y_sample,
            jnp.stack(C_p), jnp.stack(n_p), jnp.stack(m_p),
            jnp.stack(C_s), jnp.stack(n_s), jnp.stack(m_s),
            jnp.stack(lat_p), jnp.stack(kr_p), jnp.stack(lat_s), jnp.stack(kr_s))
```

```python
import functools

import jax
import jax.numpy as jnp
from jax import lax
from jax.experimental import pallas as pl
from jax.experimental.pallas import tpu as pltpu

F32 = jnp.float32
BF16 = jnp.bfloat16
EPS = 1e-6
ROPE_THETA = 10000.0

V7X_VMEM_BYTES = 64 * 1024 * 1024
LANES = 128
VMEM_LIMIT_BYTES = V7X_VMEM_BYTES - 8 * 1024 * 1024

ML_HEADS = 8
MLA_HEADS = 8
Q_LORA = 384
KV_LORA = 256
QK_NOPE = 128
QK_ROPE = 64
V_DIM = 128
Q_SLOT = KV_LORA + LANES
MLA_SCALE = (QK_NOPE + QK_ROPE) ** -0.5

NT_DIMS = (((1,), (1,)), ((), ()))


def _params(n_grid_axes):
    return pltpu.CompilerParams(
        dimension_semantics=("arbitrary",) * n_grid_axes,
        vmem_limit_bytes=VMEM_LIMIT_BYTES,
    )


def _rms(x, g):
    return x * lax.rsqrt(jnp.mean(x * x, axis=-1, keepdims=True) + EPS) * g


def _log_sigmoid(x):
    return jnp.minimum(x, 0.0) - jnp.log1p(jnp.exp(-jnp.abs(x)))


def _resident(shape):
    nd = len(shape)
    return pl.BlockSpec(shape, lambda *_: (0,) * nd)


def _norm_matmul_kernel(x_ref, g_ref, w_ref, o_ref):
    xn = _rms(x_ref[...], g_ref[...]).astype(BF16)
    o_ref[...] = jnp.dot(xn, w_ref[...], preferred_element_type=F32)


def _norm_matmul(x, g, w, tm):
    M, D = x.shape
    N = w.shape[1]
    return pl.pallas_call(
        _norm_matmul_kernel,
        out_shape=jax.ShapeDtypeStruct((M, N), F32),
        grid=(M // tm,),
        in_specs=[pl.BlockSpec((tm, D), lambda i: (i, 0)), _resident((1, D)), _resident((D, N))],
        out_specs=pl.BlockSpec((tm, N), lambda i: (i, 0)),
        compiler_params=_params(1),
        name="norm_matmul",
    )(x, g, w)


def _mlstm_kernel(q_ref, k_ref, v_ref, o_ref, gt_ref, bias_ref, gh_ref,
                  hg_ref, c_out_ref, n_out_ref, m_out_ref,
                  caug_ref, mst_ref, *, L, H, DK, DV):
    c = pl.program_id(1)
    n_chunks = pl.num_programs(1)

    @pl.when(c == 0)
    def _():
        caug_ref[...] = jnp.zeros_like(caug_ref)
        mst_ref[...] = jnp.zeros_like(mst_ref)

    gates = gt_ref[...] + bias_ref[...]
    lane = lax.broadcasted_iota(jnp.int32, gates.shape, 1)
    G = jnp.where(lane < H, gates, _log_sigmoid(gates))
    row = lax.broadcasted_iota(jnp.int32, (L, L), 0)
    col = lax.broadcasted_iota(jnp.int32, (L, L), 1)
    causal = col <= row
    cs = jnp.dot(causal.astype(F32), G, precision=lax.Precision.HIGHEST,
                 preferred_element_type=F32)
    GT = G.T
    csT = cs.T

    kT = (k_ref[...] * (DK ** -0.5)).T
    lane_pair = lax.broadcasted_iota(jnp.int32, (L, 2 * DK), 1)
    ones = jnp.ones((L, DV), F32)

    for h in range(H):
        p, half = h // 2, h % 2
        r0 = half * DK
        b_col1 = cs[:, H + h:H + h + 1]
        b_row = csT[H + h:H + h + 1, :]
        i_row = GT[h:h + 1, :]
        D = jnp.where(causal, b_col1 - b_row + i_row, -jnp.inf)
        m_prev = mst_ref[h:h + 1, 0:1]
        log_inter = b_col1 + m_prev
        m_t = jnp.maximum(log_inter, jnp.max(D, axis=1, keepdims=True))
        w_inter = jnp.exp(log_inter - m_t)
        W = jnp.exp(D - m_t)

        q_pair = q_ref[:, p * 2 * DK:(p + 1) * 2 * DK]
        in_head = (lane_pair >= r0) & (lane_pair < r0 + DK)
        qm = jnp.where(in_head, q_pair, 0.0).astype(BF16)
        k_pair = (k_ref[:, p * 2 * DK:(p + 1) * 2 * DK] * (DK ** -0.5)).astype(BF16)
        S = lax.dot_general(qm, k_pair, NT_DIMS, preferred_element_type=F32) * W

        m_new = m_t[L - 1:L, :]
        b_last = cs[L - 1:L, H + h:H + h + 1]
        w_s_row = jnp.exp(b_last - b_row + i_row - m_new)
        kwT = (kT[h * DK:(h + 1) * DK, :] * w_s_row).astype(BF16)

        vaug = jnp.concatenate([v_ref[:, h * DV:(h + 1) * DV], ones], axis=1).astype(BF16)
        lhs = jnp.concatenate([S.astype(BF16), kwT], axis=0)
        R = jnp.dot(lhs, vaug, preferred_element_type=F32)

        caug_pair = caug_ref[p]
        inter = jnp.dot(qm, caug_pair.astype(BF16), preferred_element_type=F32)
        numden = w_inter * inter + R[:L]
        num, den = numden[:, :DV], numden[:, DV:]
        hh = num / jnp.maximum(jnp.abs(den), jnp.exp(-m_t))
        hn = hh * lax.rsqrt(jnp.mean(hh * hh, axis=-1, keepdims=True) + EPS)
        og = jax.nn.sigmoid(o_ref[:, h * DV:(h + 1) * DV])
        hg_ref[:, h * DV:(h + 1) * DV] = (og * (hn * gh_ref[:, h * DV:(h + 1) * DV])).astype(BF16)

        w_c = jnp.exp(b_last + m_prev - m_new)
        caug_ref[p, r0:r0 + DK, :] = w_c * caug_pair[r0:r0 + DK, :] + R[L:]
        mst_ref[h:h + 1, :] = jnp.broadcast_to(m_new, (1, LANES))

    @pl.when(c == n_chunks - 1)
    def _():
        pick0 = (lax.broadcasted_iota(jnp.int32, (8, DV), 1) == 0).astype(F32)
        for h in range(H):
            p, r0 = h // 2, (h % 2) * DK
            ca = caug_ref[p, r0:r0 + DK, :]
            c_out_ref[0, h] = ca[:, :DV]
            n_rows = lax.dot_general(pick0, ca[:, DV:], NT_DIMS, precision=lax.Precision.HIGHEST,
                                     preferred_element_type=F32)
            n_out_ref[0, h:h + 1, :] = n_rows[0:1, :]
            m_out_ref[0, :, h:h + 1] = mst_ref[h:h + 1, 0:1]


def _mlstm_prompt(xw, bias, g_head, B, S, L):
    H, DK, DV = ML_HEADS, 64, 128
    NC = S // L
    kern = functools.partial(_mlstm_kernel, L=L, H=H, DK=DK, DV=DV)
    qk_w, v_w = H * DK, H * DV
    rows = lambda b, c: b * NC + c
    gate_blk = (2 * qk_w + 2 * v_w) // LANES
    return pl.pallas_call(
        kern,
        out_shape=(
            jax.ShapeDtypeStruct((B * S, v_w), BF16),
            jax.ShapeDtypeStruct((B, H, DK, DV), F32),
            jax.ShapeDtypeStruct((B, H, DK), F32),
            jax.ShapeDtypeStruct((B, 1, H), F32),
        ),
        grid=(B, NC),
        in_specs=[
            pl.BlockSpec((L, qk_w), lambda b, c: (rows(b, c), 0)),
            pl.BlockSpec((L, qk_w), lambda b, c: (rows(b, c), 1)),
            pl.BlockSpec((L, v_w), lambda b, c: (rows(b, c), 1)),
            pl.BlockSpec((L, v_w), lambda b, c: (rows(b, c), 2)),
            pl.BlockSpec((L, LANES), lambda b, c: (rows(b, c), gate_blk)),
            _resident((1, LANES)),
            _resident((1, v_w)),
        ],
        out_specs=(
            pl.BlockSpec((L, v_w), lambda b, c: (rows(b, c), 0)),
            pl.BlockSpec((1, H, DK, DV), lambda b, c: (b, 0, 0, 0)),
            pl.BlockSpec((1, H, DK), lambda b, c: (b, 0, 0)),
            pl.BlockSpec((1, 1, H), lambda b, c: (b, 0, 0)),
        ),
        scratch_shapes=[
            pltpu.VMEM((H // 2, 2 * DK, 2 * DV), F32),
            pltpu.VMEM((H, LANES), F32),
        ],
        compiler_params=_params(2),
        name="mlstm_prompt",
    )(xw, xw, xw, xw, xw, bias, g_head)


def _mlstm_step_kernel(q_ref, k_ref, v_ref, o_ref, gt_ref, bias_ref, gh_ref, c0_ref, n0_ref, m0_ref,
                       hg_ref, c_out_ref, n_out_ref, m_out_ref, *, H, DK, DV):
    gates = gt_ref[0] + bias_ref[...]
    q_row = q_ref[0]
    k_row = k_ref[0] * (DK ** -0.5)

    def column_broadcast(row_pair):
        return jnp.broadcast_to(row_pair, (LANES, LANES)).T

    for h in range(H):
        p, r0 = h // 2, (h % 2) * DK
        if h % 2 == 0:
            q_cols = column_broadcast(q_row[:, p * LANES:(p + 1) * LANES])
            k_cols = column_broadcast(k_row[:, p * LANES:(p + 1) * LANES])
        q_bc = q_cols[r0:r0 + DK, :]
        k_bc = k_cols[r0:r0 + DK, :]
        q_h = q_row[:, h * DK:(h + 1) * DK]
        k_h = k_row[:, h * DK:(h + 1) * DK]
        v_h = v_ref[0][:, h * DV:(h + 1) * DV]
        i_pre = gates[:, h:h + 1]
        log_f = _log_sigmoid(gates[:, H + h:H + h + 1])
        m_prev = m0_ref[0][:, h:h + 1]
        n_prev = n0_ref[0, h:h + 1, :]
        c_prev = c0_ref[0, h]

        log_inter = log_f + m_prev
        m_t = jnp.maximum(log_inter, i_pre)
        w_inter = jnp.exp(log_inter - m_t)
        w_intra = jnp.exp(i_pre - m_t)
        s = jnp.sum(q_h * k_h, axis=1, keepdims=True) * w_intra
        qc = jnp.sum(q_bc * c_prev, axis=0, keepdims=True)
        qn = jnp.sum(q_h * n_prev, axis=1, keepdims=True)
        num = w_inter * qc + s * v_h
        den = w_inter * qn + s
        hh = num / jnp.maximum(jnp.abs(den), jnp.exp(-m_t))
        hn = hh * lax.rsqrt(jnp.mean(hh * hh, axis=-1, keepdims=True) + EPS)
        og = jax.nn.sigmoid(o_ref[0][:, h * DV:(h + 1) * DV])
        hg_ref[0, :, h * DV:(h + 1) * DV] = (og * (hn * gh_ref[:, h * DV:(h + 1) * DV])).astype(BF16)

        c_out_ref[0, h] = w_inter * c_prev + (w_intra * k_bc) * v_h
        n_out_ref[0, h:h + 1, :] = w_inter * n_prev + w_intra * k_h
        m_out_ref[0, :, h:h + 1] = m_t


def _mlstm_sample(xw, bias, g_head, c0, n0, m0):
    H, DK, DV = ML_HEADS, 64, 128
    B = xw.shape[0]
    qk_w, v_w = H * DK, H * DV
    gate_blk = (2 * qk_w + 2 * v_w) // LANES
    xw3 = xw.reshape(B, 1, xw.shape[1])
    kern = functools.partial(_mlstm_step_kernel, H=H, DK=DK, DV=DV)
    return pl.pallas_call(
        kern,
        out_shape=(
            jax.ShapeDtypeStruct((B, 1, v_w), BF16),
            jax.ShapeDtypeStruct((B, H, DK, DV), F32),
            jax.ShapeDtypeStruct((B, H, DK), F32),
            jax.ShapeDtypeStruct((B, 1, H), F32),
        ),
        grid=(B,),
        in_specs=[
            pl.BlockSpec((1, 1, qk_w), lambda b: (b, 0, 0)),
            pl.BlockSpec((1, 1, qk_w), lambda b: (b, 0, 1)),
            pl.BlockSpec((1, 1, v_w), lambda b: (b, 0, 1)),
            pl.BlockSpec((1, 1, v_w), lambda b: (b, 0, 2)),
            pl.BlockSpec((1, 1, LANES), lambda b: (b, 0, gate_blk)),
            _resident((1, LANES)),
            _resident((1, v_w)),
            pl.BlockSpec((1, H, DK, DV), lambda b: (b, 0, 0, 0)),
            pl.BlockSpec((1, H, DK), lambda b: (b, 0, 0)),
            pl.BlockSpec((1, 1, H), lambda b: (b, 0, 0)),
        ],
        out_specs=(
            pl.BlockSpec((1, 1, v_w), lambda b: (b, 0, 0)),
            pl.BlockSpec((1, H, DK, DV), lambda b: (b, 0, 0, 0)),
            pl.BlockSpec((1, H, DK), lambda b: (b, 0, 0)),
            pl.BlockSpec((1, 1, H), lambda b: (b, 0, 0)),
        ),
        compiler_params=_params(1),
        name="mlstm_sample",
    )(xw3, xw3, xw3, xw3, xw3, bias, g_head, c0, n0, m0.reshape(B, 1, H))


def _ffn_kernel(h_ref, a_ref, wa_ref, gn_ref, wgu_ref, wd_ref, gf_ref, o_ref, *, d_ff, tf, final_norm):
    h1 = h_ref[...] + jnp.dot(a_ref[...], wa_ref[...], preferred_element_type=F32)
    xn = _rms(h1, gn_ref[...]).astype(BF16)
    acc = h1
    for c in range(d_ff // tf):
        g = jnp.dot(xn, wgu_ref[:, c * tf:(c + 1) * tf], preferred_element_type=F32)
        u = jnp.dot(xn, wgu_ref[:, d_ff + c * tf:d_ff + (c + 1) * tf], preferred_element_type=F32)
        act = (g * jax.nn.sigmoid(g) * u).astype(BF16)
        acc = acc + jnp.dot(act, wd_ref[c * tf:(c + 1) * tf, :], preferred_element_type=F32)
    if final_norm:
        acc = _rms(acc, gf_ref[...])
    o_ref[...] = acc


def _mixer_out_ffn(h, a, w_a, g_ffn, w_gu, w_d, g_final, tm, final_norm):
    M, D = h.shape
    KA = a.shape[1]
    d_ff = w_d.shape[0]
    kern = functools.partial(_ffn_kernel, d_ff=d_ff, tf=256, final_norm=final_norm)
    single = pl.Buffered(1)
    return pl.pallas_call(
        kern,
        out_shape=jax.ShapeDtypeStruct((M, D), F32),
        grid=(M // tm,),
        in_specs=[
            pl.BlockSpec((tm, D), lambda i: (i, 0)),
            pl.BlockSpec((tm, KA), lambda i: (i, 0)),
            pl.BlockSpec((KA, D), lambda i: (0, 0), pipeline_mode=single),
            _resident((1, D)),
            pl.BlockSpec((D, 2 * d_ff), lambda i: (0, 0), pipeline_mode=single),
            pl.BlockSpec((d_ff, D), lambda i: (0, 0), pipeline_mode=single),
            _resident((1, D)),
        ],
        out_specs=pl.BlockSpec((tm, D), lambda i: (i, 0)),
        compiler_params=_params(1),
        name="mixer_out_ffn",
    )(h, a, w_a, g_ffn, w_gu, w_d, g_final)


def _rope_slot(x, cos, sin_lo, sin_hi):
    return x * cos + pltpu.roll(x, 96, 1) * sin_lo + pltpu.roll(x, 32, 1) * sin_hi


def _mla_proj_kernel(h_ref, gn_ref, win_ref, gq_ref, gkv_ref, wuq_ref, wukt_ref,
                     cos_ref, sinlo_ref, sinhi_ref,
                     q_ref, kcat_ref, lat_ref, kr_ref, *, H):
    xn = _rms(h_ref[...], gn_ref[...]).astype(BF16)
    t = jnp.dot(xn, win_ref[...], preferred_element_type=F32)
    c_q = t[:, :Q_LORA]
    c_kv = t[:, Q_LORA:Q_LORA + KV_LORA]
    k_slot = t[:, Q_LORA + KV_LORA:]
    cos, sin_lo, sin_hi = cos_ref[...], sinlo_ref[...], sinhi_ref[...]

    lat = _rms(c_kv, gkv_ref[...])
    k_rot = _rope_slot(k_slot, cos, sin_lo, sin_hi)
    lat_ref[...] = lat
    kr_ref[...] = k_rot[:, :QK_ROPE]
    kcat_ref[:, :KV_LORA] = lat.astype(BF16)
    kcat_ref[:, KV_LORA:] = k_rot.astype(BF16)

    cqn = _rms(c_q, gq_ref[...]).astype(BF16)
    q = jnp.dot(cqn, wuq_ref[...], preferred_element_type=F32)
    for h in range(H):
        q_nope = q[:, h * QK_NOPE:(h + 1) * QK_NOPE].astype(BF16)
        q_lat = jnp.dot(q_nope, wukt_ref[h], preferred_element_type=F32)
        q_rot = _rope_slot(q[:, (H + h) * LANES:(H + h + 1) * LANES], cos, sin_lo, sin_hi)
        q_ref[:, h * Q_SLOT:h * Q_SLOT + KV_LORA] = q_lat.astype(BF16)
        q_ref[:, h * Q_SLOT + KV_LORA:(h + 1) * Q_SLOT] = q_rot.astype(BF16)


def _mla_project(h, g_norm, w_in, g_q, g_kv, w_uq, w_ukt, cos, sin_lo, sin_hi, tm, table_blocks):
    M, D = h.shape
    H = MLA_HEADS
    kern = functools.partial(_mla_proj_kernel, H=H)
    table = pl.BlockSpec((tm, LANES), lambda i: (i % table_blocks, 0))
    return pl.pallas_call(
        kern,
        out_shape=(
            jax.ShapeDtypeStruct((M, H * Q_SLOT), BF16),
            jax.ShapeDtypeStruct((M, Q_SLOT), BF16),
            jax.ShapeDtypeStruct((M, KV_LORA), F32),
            jax.ShapeDtypeStruct((M, QK_ROPE), F32),
        ),
        grid=(M // tm,),
        in_specs=[
            pl.BlockSpec((tm, D), lambda i: (i, 0)),
            _resident((1, D)),
            _resident(w_in.shape),
            _resident((1, Q_LORA)),
            _resident((1, KV_LORA)),
            _resident(w_uq.shape),
            _resident(w_ukt.shape),
            table, table, table,
        ],
        out_specs=(
            pl.BlockSpec((tm, H * Q_SLOT), lambda i: (i, 0)),
            pl.BlockSpec((tm, Q_SLOT), lambda i: (i, 0)),
            pl.BlockSpec((tm, KV_LORA), lambda i: (i, 0)),
            pl.BlockSpec((tm, QK_ROPE), lambda i: (i, 0)),
        ),
        compiler_params=_params(1),
        name="mla_project",
    )(h, g_norm, w_in, g_q, g_kv, w_uq, w_ukt, cos, sin_lo, sin_hi)


def _attn_kernel(q_ref, k_ref, wuv_ref, o_ref, qs_ref, m_ref, l_ref, acc_ref, *, H, T):
    i = pl.program_id(1)
    for h in range(H):
        qs_ref[h * T:(h + 1) * T, :] = q_ref[:, h * Q_SLOT:(h + 1) * Q_SLOT]
    m_ref[...] = jnp.full_like(m_ref, -jnp.inf)
    l_ref[...] = jnp.zeros_like(l_ref)
    acc_ref[...] = jnp.zeros_like(acc_ref)

    def step(j, masked):
        kj = k_ref[0, pl.ds(pl.multiple_of(j * T, T), T), :]
        s = lax.dot_general(qs_ref[...], kj, NT_DIMS, preferred_element_type=F32) * MLA_SCALE
        if masked:
            t_idx = lax.broadcasted_iota(jnp.int32, s.shape, 0) & (T - 1)
            s_idx = lax.broadcasted_iota(jnp.int32, s.shape, 1)
            s = jnp.where(s_idx <= t_idx, s, -jnp.inf)
        m_prev = m_ref[...]
        m_new = jnp.maximum(m_prev, jnp.max(s, axis=1, keepdims=True))
        alpha = jnp.exp(m_prev - m_new)
        p = jnp.exp(s - m_new)
        l_ref[...] = alpha * l_ref[...] + jnp.sum(p, axis=1, keepdims=True)
        acc_ref[...] = alpha * acc_ref[...] + jnp.dot(p.astype(BF16), kj[:, :KV_LORA],
                                                      preferred_element_type=F32)
        m_ref[...] = m_new

    def body(j, carry):
        step(j, masked=False)
        return carry

    lax.fori_loop(0, i, body, 0)
    step(i, masked=True)

    o_lat = (acc_ref[...] / l_ref[...]).astype(BF16)
    for h in range(H):
        v_h = jnp.dot(o_lat[h * T:(h + 1) * T, :], wuv_ref[h], preferred_element_type=F32)
        o_ref[:, h * V_DIM:(h + 1) * V_DIM] = v_h.astype(BF16)


def _attention_prompt(q, kcat, w_uv, B, S, T):
    H = MLA_HEADS
    NQ = S // T
    kern = functools.partial(_attn_kernel, H=H, T=T)
    return pl.pallas_call(
        kern,
        out_shape=jax.ShapeDtypeStruct((B * S, H * V_DIM), BF16),
        grid=(B, NQ),
        in_specs=[
            pl.BlockSpec((T, H * Q_SLOT), lambda b, i: (b * NQ + i, 0)),
            pl.BlockSpec((1, S, Q_SLOT), lambda b, i: (b, 0, 0)),
            _resident(w_uv.shape),
        ],
        out_specs=pl.BlockSpec((T, H * V_DIM), lambda b, i: (b * NQ + i, 0)),
        scratch_shapes=[
            pltpu.VMEM((H * T, Q_SLOT), BF16),
            pltpu.VMEM((H * T, 1), F32),
            pltpu.VMEM((H * T, 1), F32),
            pltpu.VMEM((H * T, KV_LORA), F32),
        ],
        compiler_params=_params(2),
        name="attention_prompt",
    )(q, kcat.reshape(B, S, Q_SLOT), w_uv)


def _decode_kernel(pt_ref, q_ref, cn_ref, krn_ref, lat_hbm, kr_hbm, o_ref,
                   lat_buf, kr_buf, sem, *, layer, G, P, NCH):
    b = pl.program_id(0)
    nb = pl.num_programs(0)

    def page_copies(bb, jj, slot):
        copies = []
        for g in range(G):
            page = pt_ref[bb, jj * G + g]
            copies.append(pltpu.make_async_copy(
                lat_hbm.at[layer, page], lat_buf.at[slot, pl.ds(g * P, P), :], sem.at[0, slot]))
            copies.append(pltpu.make_async_copy(
                kr_hbm.at[layer, page], kr_buf.at[slot, pl.ds(g * P, P), :], sem.at[1, slot]))
        return copies

    def start(bb, jj, slot):
        for cp in page_copies(bb, jj, slot):
            cp.start()

    def wait(bb, jj, slot):
        for cp in page_copies(bb, jj, slot):
            cp.wait()

    @pl.when(b == 0)
    def _():
        start(0, 0, 0)

    q = q_ref[0]
    q_lat = q[:, :KV_LORA]
    q_rope = q[:, KV_LORA:KV_LORA + QK_ROPE]
    c_new = cn_ref[0]
    kr_new = krn_ref[0]

    s_new = (jnp.sum(q_lat.astype(F32) * c_new, axis=1, keepdims=True)
             + jnp.sum(q_rope.astype(F32) * kr_new, axis=1, keepdims=True)) * MLA_SCALE
    m0 = s_new
    l0 = jnp.ones_like(s_new)
    acc0 = jnp.broadcast_to(c_new, (q.shape[0], KV_LORA)).astype(F32)

    def chunk(j, slot, carry):
        m_prev, l_prev, acc = carry
        nxt = j + 1

        @pl.when(nxt < NCH)
        def _():
            start(b, nxt, 1 - slot)

        @pl.when((nxt == NCH) & (b + 1 < nb))
        def _():
            start(b + 1, 0, 1 - slot)

        wait(b, j, slot)
        kl = lat_buf[slot].astype(BF16)
        kr = kr_buf[slot].astype(BF16)
        s = (lax.dot_general(q_lat, kl, NT_DIMS, preferred_element_type=F32)
             + lax.dot_general(q_rope, kr, NT_DIMS, preferred_element_type=F32)) * MLA_SCALE
        m_new = jnp.maximum(m_prev, jnp.max(s, axis=1, keepdims=True))
        alpha = jnp.exp(m_prev - m_new)
        p = jnp.exp(s - m_new)
        l_new = alpha * l_prev + jnp.sum(p, axis=1, keepdims=True)
        acc_new = alpha * acc + jnp.dot(p.astype(BF16), kl, preferred_element_type=F32)
        return m_new, l_new, acc_new

    def body(j2, carry):
        carry = chunk(2 * j2, 0, carry)
        return chunk(2 * j2 + 1, 1, carry)

    m_fin, l_fin, acc_fin = lax.fori_loop(0, NCH // 2, body, (m0, l0, acc0))
    o_ref[0] = acc_fin / l_fin


def _attention_sample(q3, c_new, kr_new, cache_latent, cache_k_rope, page_table, layer, G):
    B, H, _ = q3.shape
    n_pages = page_table.shape[1]
    P = cache_latent.shape[2]
    NCH = n_pages // G
    assert n_pages % G == 0 and NCH % 2 == 0
    kern = functools.partial(_decode_kernel, layer=layer, G=G, P=P, NCH=NCH)
    grid_spec = pltpu.PrefetchScalarGridSpec(
        num_scalar_prefetch=1,
        grid=(B,),
        in_specs=[
            pl.BlockSpec((1, H, Q_SLOT), lambda b, pt: (b, 0, 0)),
            pl.BlockSpec((1, 1, KV_LORA), lambda b, pt: (b, 0, 0)),
            pl.BlockSpec((1, 1, QK_ROPE), lambda b, pt: (b, 0, 0)),
            pl.BlockSpec(memory_space=pl.ANY),
            pl.BlockSpec(memory_space=pl.ANY),
        ],
        out_specs=pl.BlockSpec((1, H, KV_LORA), lambda b, pt: (b, 0, 0)),
        scratch_shapes=[
            pltpu.VMEM((2, G * P, KV_LORA), F32),
            pltpu.VMEM((2, G * P, QK_ROPE), F32),
            pltpu.SemaphoreType.DMA((2, 2)),
        ],
    )
    return pl.pallas_call(
        kern,
        out_shape=jax.ShapeDtypeStruct((B, H, KV_LORA), F32),
        grid_spec=grid_spec,
        compiler_params=_params(1),
        name="attention_sample",
    )(page_table, q3, c_new.reshape(B, 1, KV_LORA), kr_new.reshape(B, 1, QK_ROPE),
      cache_latent, cache_k_rope)


def _value_up_kernel(o_ref, wuv_ref, v_ref, *, H):
    for h in range(H):
        o_h = o_ref[:, h * KV_LORA:(h + 1) * KV_LORA].astype(BF16)
        v_ref[:, h * V_DIM:(h + 1) * V_DIM] = jnp.dot(
            o_h, wuv_ref[h], preferred_element_type=F32).astype(BF16)


def _value_up(o_lat, w_uv):
    M = o_lat.shape[0]
    H = MLA_HEADS
    return pl.pallas_call(
        functools.partial(_value_up_kernel, H=H),
        out_shape=jax.ShapeDtypeStruct((M, H * V_DIM), BF16),
        grid=(1,),
        in_specs=[_resident(o_lat.shape), _resident(w_uv.shape)],
        out_specs=_resident((M, H * V_DIM)),
        compiler_params=_params(1),
        name="value_up",
    )(o_lat, w_uv)


def _rope_tables(pos):
    half = QK_ROPE // 2
    inv = ROPE_THETA ** (-jnp.arange(0, QK_ROPE, 2, dtype=F32) / QK_ROPE)
    ang = pos[:, None] * inv[None, :]
    cos, sin = jnp.cos(ang), jnp.sin(ang)
    z = jnp.zeros_like(cos)
    cos_t = jnp.concatenate([cos, cos, z, z], axis=1)
    sin_lo = jnp.concatenate([-sin, z, z, z], axis=1)
    sin_hi = jnp.concatenate([z, sin, z, z], axis=1)
    return cos_t, sin_lo, sin_hi


def _pad_cols(w, n):
    return jnp.pad(w, ((0, 0), (0, n - w.shape[1])))


def kernel(x_prompt, x_sample, state_mlstm_C, state_mlstm_n, state_mlstm_m, cache_latent, cache_k_rope,
           page_table, norm_mix, norm_ffn, norm_final, mlstm_w_in, mlstm_b_gates, mlstm_g_head, mlstm_w_out,
           mla_w_in, mla_g_q, mla_g_kv, mla_w_uq, mla_w_uk, mla_w_uv, mla_w_o, ffn_w_gate_up, ffn_w_down):
    B, S, D = x_prompt.shape
    BS, T, _ = x_sample.shape
    assert T == 1, "sample group is one new token per sequence"
    depth = norm_mix.shape[0]
    H = MLA_HEADS
    past_len = page_table.shape[1] * cache_latent.shape[2]

    hp = x_prompt.reshape(B * S, D)
    hs = x_sample.reshape(BS, D)
    TM = 512
    row = lambda v: v.reshape(1, -1).astype(F32)

    pos_p = jnp.arange(S, dtype=F32)
    pos_s = jnp.broadcast_to(jnp.arange(T, dtype=F32) + past_len, (BS,))
    rope_p = _rope_tables(pos_p)
    rope_s = _rope_tables(pos_s)

    outs = {k: [] for k in ("C_p", "n_p", "m_p", "C_s", "n_s", "m_s", "lat_p", "kr_p", "lat_s", "kr_s")}
    for layer in range(depth):
        j = layer // 2
        last = layer == depth - 1
        g_mix = row(norm_mix[layer])
        if layer % 2 == 0:
            n_gates = 2 * ML_HEADS
            w_in = _pad_cols(mlstm_w_in[j], mlstm_w_in.shape[2] - n_gates + LANES).astype(BF16)
            bias = _pad_cols(mlstm_b_gates[j].reshape(1, -1), LANES).astype(F32)
            g_head = row(mlstm_g_head[j])
            w_a = mlstm_w_out[j].astype(BF16)

            xw_p = _norm_matmul(hp, g_mix, w_in, TM)
            a_p, C_p, n_p, m_p = _mlstm_prompt(xw_p, bias, g_head, B, S, L=128)
            xw_s = _norm_matmul(hs, g_mix, w_in, BS)
            a_s, C_s, n_s, m_s = _mlstm_sample(xw_s, bias, g_head, state_mlstm_C[j], state_mlstm_n[j],
                                               state_mlstm_m[j])
            a_s = a_s.reshape(BS, -1)
            outs["C_p"].append(C_p); outs["n_p"].append(n_p); outs["m_p"].append(m_p.reshape(B, -1))
            outs["C_s"].append(C_s); outs["n_s"].append(n_s); outs["m_s"].append(m_s.reshape(BS, -1))
        else:
            w_in = _pad_cols(mla_w_in[j], Q_LORA + KV_LORA + LANES).astype(BF16)
            wq = mla_w_uq[j].reshape(Q_LORA, H, QK_NOPE + QK_ROPE)
            wq_nope = wq[:, :, :QK_NOPE].reshape(Q_LORA, H * QK_NOPE)
            wq_rope = jnp.pad(wq[:, :, QK_NOPE:], ((0, 0), (0, 0), (0, LANES - QK_ROPE))).reshape(Q_LORA, H * LANES)
            w_uq = jnp.concatenate([wq_nope, wq_rope], axis=1).astype(BF16)
            w_ukt = jnp.transpose(mla_w_uk[j].reshape(KV_LORA, H, QK_NOPE), (1, 2, 0)).astype(BF16)
            w_uv = jnp.transpose(mla_w_uv[j].reshape(KV_LORA, H, V_DIM), (1, 0, 2)).astype(BF16)
            w_a = mla_w_o[j].astype(BF16)
            g_q, g_kv = row(mla_g_q[j]), row(mla_g_kv[j])

            q_p, kcat_p, lat_p, kr_p = _mla_project(hp, g_mix, w_in, g_q, g_kv, w_uq, w_ukt, *rope_p,
                                                    tm=TM, table_blocks=S // TM)
            a_p = _attention_prompt(q_p, kcat_p, w_uv, B, S, T=256)
            q_s, _, lat_s, kr_s = _mla_project(hs, g_mix, w_in, g_q, g_kv, w_uq, w_ukt, *rope_s,
                                               tm=BS, table_blocks=1)
            o_s = _attention_sample(q_s.reshape(BS, H, Q_SLOT), lat_s, kr_s, cache_latent, cache_k_rope,
                                    page_table, layer=j, G=8)
            a_s = _value_up(o_s.reshape(BS, H * KV_LORA), w_uv)
            outs["lat_p"].append(lat_p.reshape(B, S, KV_LORA)); outs["kr_p"].append(kr_p.reshape(B, S, QK_ROPE))
            outs["lat_s"].append(lat_s.reshape(BS, T, KV_LORA)); outs["kr_s"].append(kr_s.reshape(BS, T, QK_ROPE))

        g_ffn = row(norm_ffn[layer])
        g_fin = row(norm_final)
        w_gu = ffn_w_gate_up[layer].astype(BF16)
        w_d = ffn_w_down[layer].astype(BF16)
        hp = _mixer_out_ffn(hp, a_p, w_a, g_ffn, w_gu, w_d, g_fin, TM, final_norm=last)
        hs = _mixer_out_ffn(hs, a_s, w_a, g_ffn, w_gu, w_d, g_fin, BS, final_norm=last)

    st = jnp.stack
    return (hp.reshape(B, S, D), hs.reshape(BS, T, D),
            st(outs["C_p"]), st(outs["n_p"]), st(outs["m_p"]),
            st(outs["C_s"]), st(outs["n_s"]), st(outs["m_s"]),
            st(outs["lat_p"]), st(outs["kr_p"]), st(outs["lat_s"]), st(outs["kr_s"]))
```

```python
import functools

import jax
import jax.numpy as jnp
from jax import lax
from jax.experimental import pallas as pl
from jax.experimental.pallas import tpu as pltpu

F32 = jnp.float32
BF16 = jnp.bfloat16
EPS = 1e-6
ROPE_THETA = 10000.0

V7X_VMEM_BYTES = 64 * 1024 * 1024
LANES = 128
VMEM_LIMIT_BYTES = V7X_VMEM_BYTES - 8 * 1024 * 1024

ML_HEADS = 8
MLA_HEADS = 8
Q_LORA = 384
KV_LORA = 256
QK_NOPE = 128
QK_ROPE = 64
V_DIM = 128
Q_SLOT = KV_LORA + LANES
MLA_SCALE = (QK_NOPE + QK_ROPE) ** -0.5

NT_DIMS = (((1,), (1,)), ((), ()))


def _params(n_grid_axes):
    return pltpu.CompilerParams(
        dimension_semantics=("arbitrary",) * n_grid_axes,
        vmem_limit_bytes=VMEM_LIMIT_BYTES,
    )


def _rms(x, g):
    return x * lax.rsqrt(jnp.mean(x * x, axis=-1, keepdims=True) + EPS) * g


def _log_sigmoid(x):
    return jnp.minimum(x, 0.0) - jnp.log1p(jnp.exp(-jnp.abs(x)))


def _resident(shape):
    nd = len(shape)
    return pl.BlockSpec(shape, lambda *_: (0,) * nd)


def _norm_matmul_kernel(x_ref, g_ref, w_ref, o_ref):
    xn = _rms(x_ref[...], g_ref[...]).astype(BF16)
    o_ref[...] = jnp.dot(xn, w_ref[...], preferred_element_type=F32)


def _norm_matmul(x, g, w, tm):
    M, D = x.shape
    N = w.shape[1]
    return pl.pallas_call(
        _norm_matmul_kernel,
        out_shape=jax.ShapeDtypeStruct((M, N), F32),
        grid=(M // tm,),
        in_specs=[pl.BlockSpec((tm, D), lambda i: (i, 0)), _resident((1, D)), _resident((D, N))],
        out_specs=pl.BlockSpec((tm, N), lambda i: (i, 0)),
        compiler_params=_params(1),
        name="norm_matmul",
    )(x, g, w)


def _mlstm_kernel(q_ref, k_ref, v_ref, o_ref, gt_ref, bias_ref, gh_ref,
                  hg_ref, c_out_ref, n_out_ref, m_out_ref,
                  caug_ref, mst_ref, *, L, H, DK, DV):
    c = pl.program_id(1)
    n_chunks = pl.num_programs(1)

    @pl.when(c == 0)
    def _():
        caug_ref[...] = jnp.zeros_like(caug_ref)
        mst_ref[...] = jnp.zeros_like(mst_ref)

    gates = gt_ref[...] + bias_ref[...]
    lane = lax.broadcasted_iota(jnp.int32, gates.shape, 1)
    G = jnp.where(lane < H, gates, _log_sigmoid(gates))
    row = lax.broadcasted_iota(jnp.int32, (L, L), 0)
    col = lax.broadcasted_iota(jnp.int32, (L, L), 1)
    causal = col <= row
    cs = jnp.dot(causal.astype(F32), G, precision=lax.Precision.HIGHEST,
                 preferred_element_type=F32)
    GT = G.T
    csT = cs.T

    kT = (k_ref[...] * (DK ** -0.5)).T
    lane_pair = lax.broadcasted_iota(jnp.int32, (L, 2 * DK), 1)
    ones = jnp.ones((L, DV), F32)

    for h in range(H):
        p, half = h // 2, h % 2
        r0 = half * DK
        b_col1 = cs[:, H + h:H + h + 1]
        b_row = csT[H + h:H + h + 1, :]
        i_row = GT[h:h + 1, :]
        D = jnp.where(causal, b_col1 - b_row + i_row, -jnp.inf)
        m_prev = mst_ref[h:h + 1, 0:1]
        log_inter = b_col1 + m_prev
        m_t = jnp.maximum(log_inter, jnp.max(D, axis=1, keepdims=True))
        w_inter = jnp.exp(log_inter - m_t)
        W = jnp.exp(D - m_t)

        q_pair = q_ref[:, p * 2 * DK:(p + 1) * 2 * DK]
        in_head = (lane_pair >= r0) & (lane_pair < r0 + DK)
        qm = jnp.where(in_head, q_pair, 0.0).astype(BF16)
        k_pair = (k_ref[:, p * 2 * DK:(p + 1) * 2 * DK] * (DK ** -0.5)).astype(BF16)
        S = lax.dot_general(qm, k_pair, NT_DIMS, preferred_element_type=F32) * W

        m_new = m_t[L - 1:L, :]
        b_last = cs[L - 1:L, H + h:H + h + 1]
        w_s_row = jnp.exp(b_last - b_row + i_row - m_new)
        kwT = (kT[h * DK:(h + 1) * DK, :] * w_s_row).astype(BF16)

        vaug = jnp.concatenate([v_ref[:, h * DV:(h + 1) * DV], ones], axis=1).astype(BF16)
        lhs = jnp.concatenate([S.astype(BF16), kwT], axis=0)
        R = jnp.dot(lhs, vaug, preferred_element_type=F32)

        caug_pair = caug_ref[p]
        inter = jnp.dot(qm, caug_pair.astype(BF16), preferred_element_type=F32)
        numden = w_inter * inter + R[:L]
        num, den = numden[:, :DV], numden[:, DV:]
        hh = num / jnp.maximum(jnp.abs(den), jnp.exp(-m_t))
        hn = hh * lax.rsqrt(jnp.mean(hh * hh, axis=-1, keepdims=True) + EPS)
        og = jax.nn.sigmoid(o_ref[:, h * DV:(h + 1) * DV])
        hg_ref[:, h * DV:(h + 1) * DV] = (og * (hn * gh_ref[:, h * DV:(h + 1) * DV])).astype(BF16)

        w_c = jnp.exp(b_last + m_prev - m_new)
        caug_ref[p, r0:r0 + DK, :] = w_c * caug_pair[r0:r0 + DK, :] + R[L:]
        mst_ref[h:h + 1, :] = jnp.broadcast_to(m_new, (1, LANES))

    @pl.when(c == n_chunks - 1)
    def _():
        pick0 = (lax.broadcasted_iota(jnp.int32, (8, DV), 1) == 0).astype(F32)
        for h in range(H):
            p, r0 = h // 2, (h % 2) * DK
            ca = caug_ref[p, r0:r0 + DK, :]
            c_out_ref[0, h] = ca[:, :DV]
            n_rows = lax.dot_general(pick0, ca[:, DV:], NT_DIMS, precision=lax.Precision.HIGHEST,
                                     preferred_element_type=F32)
            n_out_ref[0, h:h + 1, :] = n_rows[0:1, :]
            m_out_ref[0, :, h:h + 1] = mst_ref[h:h + 1, 0:1]


def _mlstm_prompt(xw, bias, g_head, B, S, L):
    H, DK, DV = ML_HEADS, 64, 128
    NC = S // L
    kern = functools.partial(_mlstm_kernel, L=L, H=H, DK=DK, DV=DV)
    qk_w, v_w = H * DK, H * DV
    rows = lambda b, c: b * NC + c
    gate_blk = (2 * qk_w + 2 * v_w) // LANES
    return pl.pallas_call(
        kern,
        out_shape=(
            jax.ShapeDtypeStruct((B * S, v_w), BF16),
            jax.ShapeDtypeStruct((B, H, DK, DV), F32),
            jax.ShapeDtypeStruct((B, H, DK), F32),
            jax.ShapeDtypeStruct((B, 1, H), F32),
        ),
        grid=(B, NC),
        in_specs=[
            pl.BlockSpec((L, qk_w), lambda b, c: (rows(b, c), 0)),
            pl.BlockSpec((L, qk_w), lambda b, c: (rows(b, c), 1)),
            pl.BlockSpec((L, v_w), lambda b, c: (rows(b, c), 1)),
            pl.BlockSpec((L, v_w), lambda b, c: (rows(b, c), 2)),
            pl.BlockSpec((L, LANES), lambda b, c: (rows(b, c), gate_blk)),
            _resident((1, LANES)),
            _resident((1, v_w)),
        ],
        out_specs=(
            pl.BlockSpec((L, v_w), lambda b, c: (rows(b, c), 0)),
            pl.BlockSpec((1, H, DK, DV), lambda b, c: (b, 0, 0, 0)),
            pl.BlockSpec((1, H, DK), lambda b, c: (b, 0, 0)),
            pl.BlockSpec((1, 1, H), lambda b, c: (b, 0, 0)),
        ),
        scratch_shapes=[
            pltpu.VMEM((H // 2, 2 * DK, 2 * DV), F32),
            pltpu.VMEM((H, LANES), F32),
        ],
        compiler_params=_params(2),
        name="mlstm_prompt",
    )(xw, xw, xw, xw, xw, bias, g_head)


def _mlstm_step_kernel(q_ref, k_ref, v_ref, o_ref, gt_ref, bias_ref, gh_ref, c0_ref, n0_ref, m0_ref,
                       hg_ref, c_out_ref, n_out_ref, m_out_ref, *, H, DK, DV, TB):
    for t in range(TB):
        _mlstm_step_one(t, q_ref, k_ref, v_ref, o_ref, gt_ref, bias_ref, gh_ref, c0_ref, n0_ref, m0_ref,
                        hg_ref, c_out_ref, n_out_ref, m_out_ref, H=H, DK=DK, DV=DV)


def _mlstm_step_one(t, q_ref, k_ref, v_ref, o_ref, gt_ref, bias_ref, gh_ref, c0_ref, n0_ref, m0_ref,
                    hg_ref, c_out_ref, n_out_ref, m_out_ref, *, H, DK, DV):
    gates = gt_ref[t] + bias_ref[...]
    q_row = q_ref[t]
    k_row = k_ref[t] * (DK ** -0.5)

    def column_broadcast(row_pair):
        return jnp.broadcast_to(row_pair, (LANES, LANES)).T

    for h in range(H):
        p, r0 = h // 2, (h % 2) * DK
        if h % 2 == 0:
            q_cols = column_broadcast(q_row[:, p * LANES:(p + 1) * LANES])
            k_cols = column_broadcast(k_row[:, p * LANES:(p + 1) * LANES])
        q_bc = q_cols[r0:r0 + DK, :]
        k_bc = k_cols[r0:r0 + DK, :]
        q_h = q_row[:, h * DK:(h + 1) * DK]
        k_h = k_row[:, h * DK:(h + 1) * DK]
        v_h = v_ref[t][:, h * DV:(h + 1) * DV]
        i_pre = gates[:, h:h + 1]
        log_f = _log_sigmoid(gates[:, H + h:H + h + 1])
        m_prev = m0_ref[t][:, h:h + 1]
        n_prev = n0_ref[t, h:h + 1, :]
        c_prev = c0_ref[t, h]

        log_inter = log_f + m_prev
        m_t = jnp.maximum(log_inter, i_pre)
        w_inter = jnp.exp(log_inter - m_t)
        w_intra = jnp.exp(i_pre - m_t)
        s = jnp.sum(q_h * k_h, axis=1, keepdims=True) * w_intra
        qc = jnp.sum(q_bc * c_prev, axis=0, keepdims=True)
        qn = jnp.sum(q_h * n_prev, axis=1, keepdims=True)
        num = w_inter * qc + s * v_h
        den = w_inter * qn + s
        hh = num / jnp.maximum(jnp.abs(den), jnp.exp(-m_t))
        hn = hh * lax.rsqrt(jnp.mean(hh * hh, axis=-1, keepdims=True) + EPS)
        og = jax.nn.sigmoid(o_ref[t][:, h * DV:(h + 1) * DV])
        hg_ref[t, :, h * DV:(h + 1) * DV] = (og * (hn * gh_ref[:, h * DV:(h + 1) * DV])).astype(BF16)

        c_out_ref[t, h] = w_inter * c_prev + (w_intra * k_bc) * v_h
        n_out_ref[t, h:h + 1, :] = w_inter * n_prev + w_intra * k_h
        m_out_ref[t, :, h:h + 1] = m_t


def _mlstm_sample(xw, bias, g_head, c0, n0, m0, TB):
    H, DK, DV = ML_HEADS, 64, 128
    B = xw.shape[0]
    qk_w, v_w = H * DK, H * DV
    gate_blk = (2 * qk_w + 2 * v_w) // LANES
    xw3 = xw.reshape(B, 1, xw.shape[1])
    kern = functools.partial(_mlstm_step_kernel, H=H, DK=DK, DV=DV, TB=TB)
    return pl.pallas_call(
        kern,
        out_shape=(
            jax.ShapeDtypeStruct((B, 1, v_w), BF16),
            jax.ShapeDtypeStruct((B, H, DK, DV), F32),
            jax.ShapeDtypeStruct((B, H, DK), F32),
            jax.ShapeDtypeStruct((B, 1, H), F32),
        ),
        grid=(B // TB,),
        in_specs=[
            pl.BlockSpec((TB, 1, qk_w), lambda b: (b, 0, 0)),
            pl.BlockSpec((TB, 1, qk_w), lambda b: (b, 0, 1)),
            pl.BlockSpec((TB, 1, v_w), lambda b: (b, 0, 1)),
            pl.BlockSpec((TB, 1, v_w), lambda b: (b, 0, 2)),
            pl.BlockSpec((TB, 1, LANES), lambda b: (b, 0, gate_blk)),
            _resident((1, LANES)),
            _resident((1, v_w)),
            pl.BlockSpec((TB, H, DK, DV), lambda b: (b, 0, 0, 0)),
            pl.BlockSpec((TB, H, DK), lambda b: (b, 0, 0)),
            pl.BlockSpec((TB, 1, H), lambda b: (b, 0, 0)),
        ],
        out_specs=(
            pl.BlockSpec((TB, 1, v_w), lambda b: (b, 0, 0)),
            pl.BlockSpec((TB, H, DK, DV), lambda b: (b, 0, 0, 0)),
            pl.BlockSpec((TB, H, DK), lambda b: (b, 0, 0)),
            pl.BlockSpec((TB, 1, H), lambda b: (b, 0, 0)),
        ),
        compiler_params=_params(1),
        name="mlstm_sample",
    )(xw3, xw3, xw3, xw3, xw3, bias, g_head, c0, n0, m0.reshape(B, 1, H))


def _ffn_kernel(h_ref, a_ref, wa_ref, gn_ref, wgu_ref, wd_ref, gf_ref, o_ref, *, d_ff, tf, final_norm):
    h1 = h_ref[...] + jnp.dot(a_ref[...], wa_ref[...], preferred_element_type=F32)
    xn = _rms(h1, gn_ref[...]).astype(BF16)
    acc = h1
    for c in range(d_ff // tf):
        g = jnp.dot(xn, wgu_ref[:, c * tf:(c + 1) * tf], preferred_element_type=F32)
        u = jnp.dot(xn, wgu_ref[:, d_ff + c * tf:d_ff + (c + 1) * tf], preferred_element_type=F32)
        act = (g * jax.nn.sigmoid(g) * u).astype(BF16)
        acc = acc + jnp.dot(act, wd_ref[c * tf:(c + 1) * tf, :], preferred_element_type=F32)
    if final_norm:
        acc = _rms(acc, gf_ref[...])
    o_ref[...] = acc


def _mixer_out_ffn(h, a, w_a, g_ffn, w_gu, w_d, g_final, tm, final_norm):
    M, D = h.shape
    KA = a.shape[1]
    d_ff = w_d.shape[0]
    kern = functools.partial(_ffn_kernel, d_ff=d_ff, tf=256, final_norm=final_norm)
    single = pl.Buffered(1)
    return pl.pallas_call(
        kern,
        out_shape=jax.ShapeDtypeStruct((M, D), F32),
        grid=(M // tm,),
        in_specs=[
            pl.BlockSpec((tm, D), lambda i: (i, 0)),
            pl.BlockSpec((tm, KA), lambda i: (i, 0)),
            pl.BlockSpec((KA, D), lambda i: (0, 0), pipeline_mode=single),
            _resident((1, D)),
            pl.BlockSpec((D, 2 * d_ff), lambda i: (0, 0), pipeline_mode=single),
            pl.BlockSpec((d_ff, D), lambda i: (0, 0), pipeline_mode=single),
            _resident((1, D)),
        ],
        out_specs=pl.BlockSpec((tm, D), lambda i: (i, 0)),
        compiler_params=_params(1),
        name="mixer_out_ffn",
    )(h, a, w_a, g_ffn, w_gu, w_d, g_final)


def _rope_slot(x, cos, sin_lo, sin_hi):
    return x * cos + pltpu.roll(x, 96, 1) * sin_lo + pltpu.roll(x, 32, 1) * sin_hi


def _mla_proj_kernel(h_ref, gn_ref, win_ref, gq_ref, gkv_ref, wuq_ref, wukt_ref,
                     cos_ref, sinlo_ref, sinhi_ref,
                     q_ref, kcat_ref, lat_ref, kr_ref, *maybe_vt_ref, H, vt_tile):
    xn = _rms(h_ref[...], gn_ref[...]).astype(BF16)
    t = jnp.dot(xn, win_ref[...], preferred_element_type=F32)
    c_q = t[:, :Q_LORA]
    c_kv = t[:, Q_LORA:Q_LORA + KV_LORA]
    k_slot = t[:, Q_LORA + KV_LORA:]
    cos, sin_lo, sin_hi = cos_ref[...], sinlo_ref[...], sinhi_ref[...]

    lat = _rms(c_kv, gkv_ref[...])
    k_rot = _rope_slot(k_slot, cos, sin_lo, sin_hi)
    lat_ref[...] = lat
    kr_ref[...] = k_rot[:, :QK_ROPE]
    kcat_ref[:, :KV_LORA] = lat.astype(BF16)
    kcat_ref[:, KV_LORA:] = k_rot.astype(BF16)
    if vt_tile:
        (vt_ref,) = maybe_vt_ref
        for t in range(lat.shape[0] // vt_tile):
            vt_ref[t] = lat[t * vt_tile:(t + 1) * vt_tile, :].T.astype(BF16)

    cqn = _rms(c_q, gq_ref[...]).astype(BF16)
    q = jnp.dot(cqn, wuq_ref[...], preferred_element_type=F32)
    for h in range(H):
        q_nope = q[:, h * QK_NOPE:(h + 1) * QK_NOPE].astype(BF16)
        q_lat = jnp.dot(q_nope, wukt_ref[h], preferred_element_type=F32)
        q_rot = _rope_slot(q[:, (H + h) * LANES:(H + h + 1) * LANES], cos, sin_lo, sin_hi)
        q_ref[:, h * Q_SLOT:h * Q_SLOT + KV_LORA] = q_lat.astype(BF16)
        q_ref[:, h * Q_SLOT + KV_LORA:(h + 1) * Q_SLOT] = q_rot.astype(BF16)


def _mla_project(h, g_norm, w_in, g_q, g_kv, w_uq, w_ukt, cos, sin_lo, sin_hi, tm, table_blocks, vt_tile):
    M, D = h.shape
    H = MLA_HEADS
    kern = functools.partial(_mla_proj_kernel, H=H, vt_tile=vt_tile)
    table = pl.BlockSpec((tm, LANES), lambda i: (i % table_blocks, 0))
    out_shape = [
        jax.ShapeDtypeStruct((M, H * Q_SLOT), BF16),
        jax.ShapeDtypeStruct((M, Q_SLOT), BF16),
        jax.ShapeDtypeStruct((M, KV_LORA), F32),
        jax.ShapeDtypeStruct((M, QK_ROPE), F32),
    ]
    out_specs = [
        pl.BlockSpec((tm, H * Q_SLOT), lambda i: (i, 0)),
        pl.BlockSpec((tm, Q_SLOT), lambda i: (i, 0)),
        pl.BlockSpec((tm, KV_LORA), lambda i: (i, 0)),
        pl.BlockSpec((tm, QK_ROPE), lambda i: (i, 0)),
    ]
    if vt_tile:
        out_shape.append(jax.ShapeDtypeStruct((M // vt_tile, KV_LORA, vt_tile), BF16))
        out_specs.append(pl.BlockSpec((tm // vt_tile, KV_LORA, vt_tile), lambda i: (i, 0, 0)))
    return pl.pallas_call(
        kern,
        out_shape=tuple(out_shape),
        grid=(M // tm,),
        in_specs=[
            pl.BlockSpec((tm, D), lambda i: (i, 0)),
            _resident((1, D)),
            _resident(w_in.shape),
            _resident((1, Q_LORA)),
            _resident((1, KV_LORA)),
            _resident(w_uq.shape),
            _resident(w_ukt.shape),
            table, table, table,
        ],
        out_specs=tuple(out_specs),
        compiler_params=_params(1),
        name="mla_project",
    )(h, g_norm, w_in, g_q, g_kv, w_uq, w_ukt, cos, sin_lo, sin_hi)


def _attn_kernel(q_ref, k_ref, vt_ref, wuv_ref, o_ref, qs_ref, m_ref, l_ref, acc_ref, *, H, T):
    i = pl.program_id(1)
    for h in range(H):
        qs_ref[h * T:(h + 1) * T, :] = q_ref[:, h * Q_SLOT:(h + 1) * Q_SLOT]
    m_ref[...] = jnp.full_like(m_ref, -jnp.inf)
    l_ref[...] = jnp.zeros_like(l_ref)
    acc_ref[...] = jnp.zeros_like(acc_ref)

    def step(j, masked):
        kj = k_ref[0, pl.ds(pl.multiple_of(j * T, T), T), :]
        vtj = vt_ref[j]
        st = lax.dot_general(kj, qs_ref[...], NT_DIMS, preferred_element_type=F32) * MLA_SCALE
        if masked:
            key = lax.broadcasted_iota(jnp.int32, st.shape, 0)
            qry = lax.broadcasted_iota(jnp.int32, st.shape, 1) & (T - 1)
            st = jnp.where(key <= qry, st, -jnp.inf)
        m_prev = m_ref[...]
        m_new = jnp.maximum(m_prev, jnp.max(st, axis=0, keepdims=True))
        alpha = jnp.exp(m_prev - m_new)
        p = jnp.exp(st - m_new)
        l_ref[...] = alpha * l_ref[...] + jnp.sum(p, axis=0, keepdims=True)
        acc_ref[...] = alpha * acc_ref[...] + jnp.dot(vtj, p.astype(BF16), preferred_element_type=F32)
        m_ref[...] = m_new

    def body(j, carry):
        step(j, masked=False)
        return carry

    lax.fori_loop(0, i, body, 0)
    step(i, masked=True)

    o_t = acc_ref[...] / l_ref[...]
    for h in range(H):
        o_lat = o_t[:, h * T:(h + 1) * T].T.astype(BF16)
        v_h = jnp.dot(o_lat, wuv_ref[h], preferred_element_type=F32)
        o_ref[:, h * V_DIM:(h + 1) * V_DIM] = v_h.astype(BF16)


def _attention_prompt(q, kcat, vt, w_uv, B, S, T):
    H = MLA_HEADS
    NQ = S // T
    kern = functools.partial(_attn_kernel, H=H, T=T)
    return pl.pallas_call(
        kern,
        out_shape=jax.ShapeDtypeStruct((B * S, H * V_DIM), BF16),
        grid=(B, NQ),
        in_specs=[
            pl.BlockSpec((T, H * Q_SLOT), lambda b, i: (b * NQ + i, 0)),
            pl.BlockSpec((1, S, Q_SLOT), lambda b, i: (b, 0, 0)),
            pl.BlockSpec((NQ, KV_LORA, T), lambda b, i: (b, 0, 0)),
            _resident(w_uv.shape),
        ],
        out_specs=pl.BlockSpec((T, H * V_DIM), lambda b, i: (b * NQ + i, 0)),
        scratch_shapes=[
            pltpu.VMEM((H * T, Q_SLOT), BF16),
            pltpu.VMEM((1, H * T), F32),
            pltpu.VMEM((1, H * T), F32),
            pltpu.VMEM((KV_LORA, H * T), F32),
        ],
        compiler_params=_params(2),
        name="attention_prompt",
    )(q, kcat.reshape(B, S, Q_SLOT), vt, w_uv)


def _decode_kernel(pt_ref, q_ref, cn_ref, krn_ref, lat_hbm, krt_hbm, o_ref,
                   lat_buf, kr_buf, sem, *, layer, G, P, NCH, RING):
    b = pl.program_id(0)
    nb = pl.num_programs(0)
    total = nb * NCH

    def page_copies(c):
        slot = lax.rem(c, RING)
        cw = jnp.where(c >= total, c - total, c)
        bb, jj = lax.div(cw, NCH), lax.rem(cw, NCH)
        copies = []
        for g in range(G):
            page = pt_ref[bb, jj * G + g]
            copies.append(pltpu.make_async_copy(
                lat_hbm.at[layer, page], lat_buf.at[slot, pl.ds(g * P, P), :], sem.at[0, slot]))
            copies.append(pltpu.make_async_copy(
                krt_hbm.at[layer, page], kr_buf.at[slot, :, pl.ds(g * P, P)], sem.at[1, slot]))
        return copies

    def start(c):
        for cp in page_copies(c):
            cp.start()

    def wait(c):
        for cp in page_copies(c):
            cp.wait()

    @pl.when(b == 0)
    def _():
        for c in range(RING - 1):
            start(jnp.int32(c))

    q = q_ref[0].astype(F32)
    q_lat = q[:, :KV_LORA]
    q_rope = q[:, KV_LORA:KV_LORA + QK_ROPE]
    c_new = cn_ref[0]
    kr_new = krn_ref[0]

    s_new = (jnp.sum(q_lat * c_new, axis=1, keepdims=True)
             + jnp.sum(q_rope * kr_new, axis=1, keepdims=True)) * MLA_SCALE
    m0 = s_new
    l0 = jnp.ones_like(s_new)
    acc0 = jnp.broadcast_to(c_new, (q.shape[0], KV_LORA)).astype(F32)

    def chunk(j, carry):
        m_prev, l_prev, acc = carry
        c = b * NCH + j
        slot = lax.rem(c, RING)
        wait(c)
        kl = lat_buf[slot]
        krt = kr_buf[slot]
        s = (lax.dot_general(q_lat, kl, NT_DIMS, preferred_element_type=F32)
             + jnp.dot(q_rope, krt, preferred_element_type=F32)) * MLA_SCALE
        m_new = jnp.maximum(m_prev, jnp.max(s, axis=1, keepdims=True))
        alpha = jnp.exp(m_prev - m_new)
        p = jnp.exp(s - m_new)
        l_new = alpha * l_prev + jnp.sum(p, axis=1, keepdims=True)
        acc_new = alpha * acc + jnp.dot(p, kl, preferred_element_type=F32)
        start(c + (RING - 1))
        return m_new, l_new, acc_new

    m_fin, l_fin, acc_fin = lax.fori_loop(0, NCH, chunk, (m0, l0, acc0))
    o_ref[0] = acc_fin / l_fin

    @pl.when(b == nb - 1)
    def _():
        for c in range(RING - 1):
            wait(total + c)


def _attention_sample(q3, c_new, kr_new, cache_latent, cache_k_rope_t, page_table, layer, G):
    B, H, _ = q3.shape
    n_pages = page_table.shape[1]
    P = cache_latent.shape[2]
    NCH = n_pages // G
    assert n_pages % G == 0
    RING = 3
    kern = functools.partial(_decode_kernel, layer=layer, G=G, P=P, NCH=NCH, RING=RING)
    grid_spec = pltpu.PrefetchScalarGridSpec(
        num_scalar_prefetch=1,
        grid=(B,),
        in_specs=[
            pl.BlockSpec((1, H, Q_SLOT), lambda b, pt: (b, 0, 0)),
            pl.BlockSpec((1, 1, KV_LORA), lambda b, pt: (b, 0, 0)),
            pl.BlockSpec((1, 1, QK_ROPE), lambda b, pt: (b, 0, 0)),
            pl.BlockSpec(memory_space=pl.ANY),
            pl.BlockSpec(memory_space=pl.ANY),
        ],
        out_specs=pl.BlockSpec((1, H, KV_LORA), lambda b, pt: (b, 0, 0)),
        scratch_shapes=[
            pltpu.VMEM((RING, G * P, KV_LORA), F32),
            pltpu.VMEM((RING, QK_ROPE, G * P), F32),
            pltpu.SemaphoreType.DMA((2, RING)),
        ],
    )
    return pl.pallas_call(
        kern,
        out_shape=jax.ShapeDtypeStruct((B, H, KV_LORA), F32),
        grid_spec=grid_spec,
        compiler_params=_params(1),
        name="attention_sample",
    )(page_table, q3, c_new.reshape(B, 1, KV_LORA), kr_new.reshape(B, 1, QK_ROPE),
      cache_latent, cache_k_rope_t)


def _value_up_kernel(o_ref, wuv_ref, v_ref, *, H):
    for h in range(H):
        o_h = o_ref[:, h * KV_LORA:(h + 1) * KV_LORA].astype(BF16)
        v_ref[:, h * V_DIM:(h + 1) * V_DIM] = jnp.dot(
            o_h, wuv_ref[h], preferred_element_type=F32).astype(BF16)


def _value_up(o_lat, w_uv):
    M = o_lat.shape[0]
    H = MLA_HEADS
    return pl.pallas_call(
        functools.partial(_value_up_kernel, H=H),
        out_shape=jax.ShapeDtypeStruct((M, H * V_DIM), BF16),
        grid=(1,),
        in_specs=[_resident(o_lat.shape), _resident(w_uv.shape)],
        out_specs=_resident((M, H * V_DIM)),
        compiler_params=_params(1),
        name="value_up",
    )(o_lat, w_uv)


def _rope_tables(pos):
    half = QK_ROPE // 2
    inv = ROPE_THETA ** (-jnp.arange(0, QK_ROPE, 2, dtype=F32) / QK_ROPE)
    ang = pos[:, None] * inv[None, :]
    cos, sin = jnp.cos(ang), jnp.sin(ang)
    z = jnp.zeros_like(cos)
    cos_t = jnp.concatenate([cos, cos, z, z], axis=1)
    sin_lo = jnp.concatenate([-sin, z, z, z], axis=1)
    sin_hi = jnp.concatenate([z, sin, z, z], axis=1)
    return cos_t, sin_lo, sin_hi


def _pad_cols(w, n):
    return jnp.pad(w, ((0, 0), (0, n - w.shape[1])))


def kernel(x_prompt, x_sample, state_mlstm_C, state_mlstm_n, state_mlstm_m, cache_latent, cache_k_rope,
           page_table, norm_mix, norm_ffn, norm_final, mlstm_w_in, mlstm_b_gates, mlstm_g_head, mlstm_w_out,
           mla_w_in, mla_g_q, mla_g_kv, mla_w_uq, mla_w_uk, mla_w_uv, mla_w_o, ffn_w_gate_up, ffn_w_down):
    B, S, D = x_prompt.shape
    BS, T, _ = x_sample.shape
    assert T == 1, "sample group is one new token per sequence"
    depth = norm_mix.shape[0]
    H = MLA_HEADS
    past_len = page_table.shape[1] * cache_latent.shape[2]

    hp = x_prompt.reshape(B * S, D)
    hs = x_sample.reshape(BS, D)
    TM = 512
    row = lambda v: v.reshape(1, -1).astype(F32)

    pos_p = jnp.arange(S, dtype=F32)
    pos_s = jnp.broadcast_to(jnp.arange(T, dtype=F32) + past_len, (BS,))
    rope_p = _rope_tables(pos_p)
    rope_s = _rope_tables(pos_s)

    outs = {k: [] for k in ("C_p", "n_p", "m_p", "C_s", "n_s", "m_s", "lat_p", "kr_p", "lat_s", "kr_s")}
    for layer in range(depth):
        j = layer // 2
        last = layer == depth - 1
        g_mix = row(norm_mix[layer])
        if layer % 2 == 0:
            n_gates = 2 * ML_HEADS
            w_in = _pad_cols(mlstm_w_in[j], mlstm_w_in.shape[2] - n_gates + LANES).astype(BF16)
            bias = _pad_cols(mlstm_b_gates[j].reshape(1, -1), LANES).astype(F32)
            g_head = row(mlstm_g_head[j])
            w_a = mlstm_w_out[j].astype(BF16)

            xw_p = _norm_matmul(hp, g_mix, w_in, TM)
            a_p, C_p, n_p, m_p = _mlstm_prompt(xw_p, bias, g_head, B, S, L=128)
            xw_s = _norm_matmul(hs, g_mix, w_in, BS)
            a_s, C_s, n_s, m_s = _mlstm_sample(xw_s, bias, g_head, state_mlstm_C[j], state_mlstm_n[j],
                                               state_mlstm_m[j], TB=1)
            a_s = a_s.reshape(BS, -1)
            outs["C_p"].append(C_p); outs["n_p"].append(n_p); outs["m_p"].append(m_p.reshape(B, -1))
            outs["C_s"].append(C_s); outs["n_s"].append(n_s); outs["m_s"].append(m_s.reshape(BS, -1))
        else:
            w_in = _pad_cols(mla_w_in[j], Q_LORA + KV_LORA + LANES).astype(BF16)
            wq = mla_w_uq[j].reshape(Q_LORA, H, QK_NOPE + QK_ROPE)
            wq_nope = wq[:, :, :QK_NOPE].reshape(Q_LORA, H * QK_NOPE)
            wq_rope = jnp.pad(wq[:, :, QK_NOPE:], ((0, 0), (0, 0), (0, LANES - QK_ROPE))).reshape(Q_LORA, H * LANES)
            w_uq = jnp.concatenate([wq_nope, wq_rope], axis=1).astype(BF16)
            w_ukt = jnp.transpose(mla_w_uk[j].reshape(KV_LORA, H, QK_NOPE), (1, 2, 0)).astype(BF16)
            w_uv = jnp.transpose(mla_w_uv[j].reshape(KV_LORA, H, V_DIM), (1, 0, 2)).astype(BF16)
            w_a = mla_w_o[j].astype(BF16)
            g_q, g_kv = row(mla_g_q[j]), row(mla_g_kv[j])

            T_ATT = 256
            q_p, kcat_p, lat_p, kr_p, vt_p = _mla_project(hp, g_mix, w_in, g_q, g_kv, w_uq, w_ukt, *rope_p,
                                                          tm=TM, table_blocks=S // TM, vt_tile=T_ATT)
            a_p = _attention_prompt(q_p, kcat_p, vt_p, w_uv, B, S, T=T_ATT)
            q_s, _, lat_s, kr_s = _mla_project(hs, g_mix, w_in, g_q, g_kv, w_uq, w_ukt, *rope_s,
                                               tm=BS, table_blocks=1, vt_tile=0)
            cache_k_rope_t = jnp.swapaxes(cache_k_rope, 2, 3)
            o_s = _attention_sample(q_s.reshape(BS, H, Q_SLOT), lat_s, kr_s, cache_latent, cache_k_rope_t,
                                    page_table, layer=j, G=32)
            a_s = _value_up(o_s.reshape(BS, H * KV_LORA), w_uv)
            outs["lat_p"].append(lat_p.reshape(B, S, KV_LORA)); outs["kr_p"].append(kr_p.reshape(B, S, QK_ROPE))
            outs["lat_s"].append(lat_s.reshape(BS, T, KV_LORA)); outs["kr_s"].append(kr_s.reshape(BS, T, QK_ROPE))

        g_ffn = row(norm_ffn[layer])
        g_fin = row(norm_final)
        w_gu = ffn_w_gate_up[layer].astype(BF16)
        w_d = ffn_w_down[layer].astype(BF16)
        hp = _mixer_out_ffn(hp, a_p, w_a, g_ffn, w_gu, w_d, g_fin, TM, final_norm=last)
        hs = _mixer_out_ffn(hs, a_s, w_a, g_ffn, w_gu, w_d, g_fin, BS, final_norm=last)

    st = jnp.stack
    return (hp.reshape(B, S, D), hs.reshape(BS, T, D),
            st(outs["C_p"]), st(outs["n_p"]), st(outs["m_p"]),
            st(outs["C_s"]), st(outs["n_s"]), st(outs["m_s"]),
            st(outs["lat_p"]), st(outs["kr_p"]), st(outs["lat_s"]), st(outs["kr_s"]))
```

```python
import functools

import jax
import jax.numpy as jnp
from jax import lax
from jax.experimental import pallas as pl
from jax.experimental.pallas import tpu as pltpu

F32 = jnp.float32
BF16 = jnp.bfloat16
EPS = 1e-6
ROPE_THETA = 10000.0

V7X_VMEM_BYTES = 64 * 1024 * 1024
LANES = 128
VMEM_LIMIT_BYTES = V7X_VMEM_BYTES - 8 * 1024 * 1024

ML_HEADS = 8
MLA_HEADS = 8
Q_LORA = 384
KV_LORA = 256
QK_NOPE = 128
QK_ROPE = 64
V_DIM = 128
Q_SLOT = KV_LORA + LANES
MLA_SCALE = (QK_NOPE + QK_ROPE) ** -0.5

NT_DIMS = (((1,), (1,)), ((), ()))


def _params(n_grid_axes):
    return pltpu.CompilerParams(
        dimension_semantics=("arbitrary",) * n_grid_axes,
        vmem_limit_bytes=VMEM_LIMIT_BYTES,
    )


def _rms(x, g):
    return x * lax.rsqrt(jnp.mean(x * x, axis=-1, keepdims=True) + EPS) * g


def _log_sigmoid(x):
    return jnp.minimum(x, 0.0) - jnp.log1p(jnp.exp(-jnp.abs(x)))


def _resident(shape):
    nd = len(shape)
    return pl.BlockSpec(shape, lambda *_: (0,) * nd)


def _norm_matmul_kernel(x_ref, g_ref, w_ref, o_ref):
    xn = _rms(x_ref[...], g_ref[...]).astype(BF16)
    o_ref[...] = jnp.dot(xn, w_ref[...], preferred_element_type=F32)


def _norm_matmul(x, g, w, tm):
    M, D = x.shape
    N = w.shape[1]
    return pl.pallas_call(
        _norm_matmul_kernel,
        out_shape=jax.ShapeDtypeStruct((M, N), F32),
        grid=(M // tm,),
        in_specs=[pl.BlockSpec((tm, D), lambda i: (i, 0)), _resident((1, D)), _resident((D, N))],
        out_specs=pl.BlockSpec((tm, N), lambda i: (i, 0)),
        compiler_params=_params(1),
        name="norm_matmul",
    )(x, g, w)


def _mlstm_kernel(q_ref, k_ref, v_ref, o_ref, gt_ref, bias_ref, gh_ref,
                  hg_ref, c_out_ref, n_out_ref, m_out_ref,
                  caug_ref, mst_ref, *, L, H, DK, DV):
    c = pl.program_id(1)
    n_chunks = pl.num_programs(1)

    @pl.when(c == 0)
    def _():
        caug_ref[...] = jnp.zeros_like(caug_ref)
        mst_ref[...] = jnp.zeros_like(mst_ref)

    gates = gt_ref[...] + bias_ref[...]
    lane = lax.broadcasted_iota(jnp.int32, gates.shape, 1)
    G = jnp.where(lane < H, gates, _log_sigmoid(gates))
    row = lax.broadcasted_iota(jnp.int32, (L, L), 0)
    col = lax.broadcasted_iota(jnp.int32, (L, L), 1)
    causal = col <= row
    cs = jnp.dot(causal.astype(F32), G, precision=lax.Precision.HIGHEST,
                 preferred_element_type=F32)
    GT = G.T
    csT = cs.T

    kT = (k_ref[...] * (DK ** -0.5)).T
    ones = jnp.ones((L, DV), F32)
    heads = range(H)


    b_col = [cs[:, H + h:H + h + 1] for h in heads]
    b_row = [csT[H + h:H + h + 1, :] for h in heads]
    i_row = [GT[h:h + 1, :] for h in heads]
    r_mat = [jnp.where(causal, i_row[h] - b_row[h], -jnp.inf) for h in heads]
    r_max = [jnp.max(r_mat[h], axis=1, keepdims=True) for h in heads]

    m_prev = [mst_ref[h:h + 1, 0:1] for h in heads]
    log_inter = [b_col[h] + m_prev[h] for h in heads]
    m_t = [jnp.maximum(log_inter[h], b_col[h] + r_max[h]) for h in heads]
    w_inter = [jnp.exp(log_inter[h] - m_t[h]) for h in heads]
    W = [jnp.exp(r_mat[h] + (b_col[h] - m_t[h])) for h in heads]

    lane_pair = lax.broadcasted_iota(jnp.int32, (L, 2 * DK), 1)
    S, inter, caug_prev = [None] * H, [None] * H, [None] * H
    for p in range(H // 2):
        q_pair = q_ref[:, p * 2 * DK:(p + 1) * 2 * DK]
        q_lo = jnp.where(lane_pair < DK, q_pair, 0.0)
        q_hi = jnp.where(lane_pair < DK, 0.0, q_pair)
        qm2 = jnp.concatenate([q_lo, q_hi], axis=0).astype(BF16)
        k_pair = (k_ref[:, p * 2 * DK:(p + 1) * 2 * DK] * (DK ** -0.5)).astype(BF16)
        s2 = lax.dot_general(qm2, k_pair, NT_DIMS, preferred_element_type=F32)
        caug_pair = caug_ref[p]
        i2 = jnp.dot(qm2, caug_pair.astype(BF16), preferred_element_type=F32)
        for half in range(2):
            h = 2 * p + half
            S[h] = s2[half * L:(half + 1) * L] * W[h]
            inter[h] = i2[half * L:(half + 1) * L]
            caug_prev[h] = caug_pair[half * DK:(half + 1) * DK, :]

    m_new = [m_t[h][L - 1:L, :] for h in heads]
    b_last = [cs[L - 1:L, H + h:H + h + 1] for h in heads]
    R = []
    for h in heads:
        w_s_row = jnp.exp(b_last[h] - b_row[h] + i_row[h] - m_new[h])
        kwT = (kT[h * DK:(h + 1) * DK, :] * w_s_row).astype(BF16)
        vaug = jnp.concatenate([v_ref[:, h * DV:(h + 1) * DV], ones], axis=1).astype(BF16)
        lhs = jnp.concatenate([S[h].astype(BF16), kwT], axis=0)
        R.append(jnp.dot(lhs, vaug, preferred_element_type=F32))

    hh = []
    for h in heads:
        numden = w_inter[h] * inter[h] + R[h][:L]
        num, den = numden[:, :DV], numden[:, DV:]
        hh.append(num / jnp.maximum(jnp.abs(den), jnp.exp(-m_t[h])))
    ms = [jnp.mean(hh[h] * hh[h], axis=-1, keepdims=True) for h in heads]
    for h in heads:
        hn = hh[h] * lax.rsqrt(ms[h] + EPS)
        og = jax.nn.sigmoid(o_ref[:, h * DV:(h + 1) * DV])
        hg_ref[:, h * DV:(h + 1) * DV] = (og * (hn * gh_ref[:, h * DV:(h + 1) * DV])).astype(BF16)

    for h in heads:
        p, r0 = h // 2, (h % 2) * DK
        w_c = jnp.exp(b_last[h] + m_prev[h] - m_new[h])
        caug_ref[p, r0:r0 + DK, :] = w_c * caug_prev[h] + R[h][L:]
        mst_ref[h:h + 1, :] = jnp.broadcast_to(m_new[h], (1, LANES))

    @pl.when(c == n_chunks - 1)
    def _():
        pick0 = (lax.broadcasted_iota(jnp.int32, (8, DV), 1) == 0).astype(F32)
        for h in range(H):
            p, r0 = h // 2, (h % 2) * DK
            ca = caug_ref[p, r0:r0 + DK, :]
            c_out_ref[0, h] = ca[:, :DV]
            n_rows = lax.dot_general(pick0, ca[:, DV:], NT_DIMS, precision=lax.Precision.HIGHEST,
                                     preferred_element_type=F32)
            n_out_ref[0, h:h + 1, :] = n_rows[0:1, :]
            m_out_ref[0, :, h:h + 1] = mst_ref[h:h + 1, 0:1]


def _mlstm_prompt(xw, bias, g_head, B, S, L):
    H, DK, DV = ML_HEADS, 64, 128
    NC = S // L
    kern = functools.partial(_mlstm_kernel, L=L, H=H, DK=DK, DV=DV)
    qk_w, v_w = H * DK, H * DV
    rows = lambda b, c: b * NC + c
    gate_blk = (2 * qk_w + 2 * v_w) // LANES
    return pl.pallas_call(
        kern,
        out_shape=(
            jax.ShapeDtypeStruct((B * S, v_w), BF16),
            jax.ShapeDtypeStruct((B, H, DK, DV), F32),
            jax.ShapeDtypeStruct((B, H, DK), F32),
            jax.ShapeDtypeStruct((B, 1, H), F32),
        ),
        grid=(B, NC),
        in_specs=[
            pl.BlockSpec((L, qk_w), lambda b, c: (rows(b, c), 0)),
            pl.BlockSpec((L, qk_w), lambda b, c: (rows(b, c), 1)),
            pl.BlockSpec((L, v_w), lambda b, c: (rows(b, c), 1)),
            pl.BlockSpec((L, v_w), lambda b, c: (rows(b, c), 2)),
            pl.BlockSpec((L, LANES), lambda b, c: (rows(b, c), gate_blk)),
            _resident((1, LANES)),
            _resident((1, v_w)),
        ],
        out_specs=(
            pl.BlockSpec((L, v_w), lambda b, c: (rows(b, c), 0)),
            pl.BlockSpec((1, H, DK, DV), lambda b, c: (b, 0, 0, 0)),
            pl.BlockSpec((1, H, DK), lambda b, c: (b, 0, 0)),
            pl.BlockSpec((1, 1, H), lambda b, c: (b, 0, 0)),
        ),
        scratch_shapes=[
            pltpu.VMEM((H // 2, 2 * DK, 2 * DV), F32),
            pltpu.VMEM((H, LANES), F32),
        ],
        compiler_params=_params(2),
        name="mlstm_prompt",
    )(xw, xw, xw, xw, xw, bias, g_head)


def _mlstm_step_kernel(q_ref, k_ref, v_ref, o_ref, gi_ref, gf_ref, bi_ref, bf_ref, gh_ref,
                       c0_ref, n0_ref, m0_ref,
                       hg_ref, c_out_ref, n_out_ref, m_out_ref, *, H, DK, DV, TB):
    RB = TB * H
    i_pre = gi_ref[...] + bi_ref[...]
    log_f = _log_sigmoid(gf_ref[...] + bf_ref[...])
    log_inter = log_f + m0_ref[...]
    m_t = jnp.maximum(log_inter, i_pre)
    w_inter = jnp.exp(log_inter - m_t)
    w_intra = jnp.exp(i_pre - m_t)

    q = q_ref[...]
    k = k_ref[...] * (DK ** -0.5)
    v = v_ref[...]
    n_prev = n0_ref[...]
    s = jnp.sum(q * k, axis=1, keepdims=True) * w_intra
    den = w_inter * jnp.sum(q * n_prev, axis=1, keepdims=True) + s

    def block_diag(x):
        wide = jnp.concatenate([x] * H, axis=1)
        head_of_lane = lax.broadcasted_iota(jnp.int32, wide.shape, 1) // DK
        head_of_row = lax.broadcasted_iota(jnp.int32, wide.shape, 0) % H
        return jnp.where(head_of_lane == head_of_row, wide, 0.0)

    q_bd = block_diag(q)
    kw_bd = block_diag(k * w_intra[:, :DK])
    qc = jnp.concatenate(
        [jnp.dot(q_bd[t * H:(t + 1) * H, :], c0_ref[t], preferred_element_type=F32) for t in range(TB)],
        axis=0)

    num = w_inter * qc + s * v
    hh = num / jnp.maximum(jnp.abs(den), jnp.exp(-m_t))
    hn = hh * lax.rsqrt(jnp.mean(hh * hh, axis=-1, keepdims=True) + EPS)
    hg_ref[...] = (jax.nn.sigmoid(o_ref[...]) * (hn * gh_ref[...])).astype(BF16)
    n_out_ref[...] = w_inter[:, :DK] * n_prev + w_intra[:, :DK] * k
    m_out_ref[...] = m_t

    for t in range(TB):
        rows = slice(t * H, (t + 1) * H)
        d_c = lax.dot_general(kw_bd[rows, :], v[rows, :], (((0,), (0,)), ((), ())),
                              preferred_element_type=F32)
        for h in range(H):
            blk = slice(h * DK, (h + 1) * DK)
            c_out_ref[t, blk, :] = w_inter[t * H + h:t * H + h + 1, :] * c0_ref[t, blk, :] + d_c[blk, :]


def _mlstm_sample(xw, b_gates, g_head, c0, n0, m0, TB):
    H, DK, DV = ML_HEADS, 64, 128
    B = xw.shape[0]
    R, RB = B * H, TB * H
    qk_w, v_w = H * DK, H * DV
    lanes = lambda x: jnp.broadcast_to(x.reshape(-1, 1), (x.size, LANES))
    per_block = lambda x: jnp.tile(x, (TB, 1))
    q = xw[:, :qk_w].reshape(R, DK)
    k = xw[:, qk_w:2 * qk_w].reshape(R, DK)
    v = xw[:, 2 * qk_w:2 * qk_w + v_w].reshape(R, DV)
    o = xw[:, 2 * qk_w + v_w:2 * qk_w + 2 * v_w].reshape(R, DV)
    g0 = 2 * qk_w + 2 * v_w
    operands = (
        q, k, v, o,
        lanes(xw[:, g0:g0 + H]), lanes(xw[:, g0 + H:g0 + 2 * H]),
        per_block(lanes(b_gates[:H])), per_block(lanes(b_gates[H:])),
        per_block(g_head.reshape(H, DV)),
        c0.reshape(B, H * DK, DV), n0.reshape(R, DK), lanes(m0),
    )
    row_blk = lambda n: pl.BlockSpec((RB, n), lambda b: (b, 0))
    state_blk = pl.BlockSpec((TB, H * DK, DV), lambda b: (b, 0, 0))
    kern = functools.partial(_mlstm_step_kernel, H=H, DK=DK, DV=DV, TB=TB)
    hg, c_new, n_new, m_new = pl.pallas_call(
        kern,
        out_shape=(
            jax.ShapeDtypeStruct((R, DV), BF16),
            jax.ShapeDtypeStruct((B, H * DK, DV), F32),
            jax.ShapeDtypeStruct((R, DK), F32),
            jax.ShapeDtypeStruct((R, LANES), F32),
        ),
        grid=(B // TB,),
        in_specs=[
            row_blk(DK), row_blk(DK), row_blk(DV), row_blk(DV),
            row_blk(LANES), row_blk(LANES),
            _resident((RB, LANES)), _resident((RB, LANES)), _resident((RB, DV)),
            state_blk, row_blk(DK), row_blk(LANES),
        ],
        out_specs=(row_blk(DV), state_blk, row_blk(DK), row_blk(LANES)),
        compiler_params=_params(1),
        name="mlstm_sample",
    )(*operands)
    return (hg.reshape(B, v_w), c_new.reshape(B, H, DK, DV), n_new.reshape(B, H, DK),
            m_new[:, 0].reshape(B, H))


def _ffn_kernel(h_ref, a_ref, wa_ref, gn_ref, wgu_ref, wd_ref, gf_ref, o_ref, *, d_ff, tf, final_norm):
    h1 = h_ref[...] + jnp.dot(a_ref[...], wa_ref[...], preferred_element_type=F32)
    xn = _rms(h1, gn_ref[...]).astype(BF16)
    acc = h1
    for c in range(d_ff // tf):
        g = jnp.dot(xn, wgu_ref[:, c * tf:(c + 1) * tf], preferred_element_type=F32)
        u = jnp.dot(xn, wgu_ref[:, d_ff + c * tf:d_ff + (c + 1) * tf], preferred_element_type=F32)
        act = (g * jax.nn.sigmoid(g) * u).astype(BF16)
        acc = acc + jnp.dot(act, wd_ref[c * tf:(c + 1) * tf, :], preferred_element_type=F32)
    if final_norm:
        acc = _rms(acc, gf_ref[...])
    o_ref[...] = acc


def _mixer_out_ffn(h, a, w_a, g_ffn, w_gu, w_d, g_final, tm, final_norm):
    M, D = h.shape
    KA = a.shape[1]
    d_ff = w_d.shape[0]
    kern = functools.partial(_ffn_kernel, d_ff=d_ff, tf=256, final_norm=final_norm)
    single = pl.Buffered(1)
    return pl.pallas_call(
        kern,
        out_shape=jax.ShapeDtypeStruct((M, D), F32),
        grid=(M // tm,),
        in_specs=[
            pl.BlockSpec((tm, D), lambda i: (i, 0)),
            pl.BlockSpec((tm, KA), lambda i: (i, 0)),
            pl.BlockSpec((KA, D), lambda i: (0, 0), pipeline_mode=single),
            _resident((1, D)),
            pl.BlockSpec((D, 2 * d_ff), lambda i: (0, 0), pipeline_mode=single),
            pl.BlockSpec((d_ff, D), lambda i: (0, 0), pipeline_mode=single),
            _resident((1, D)),
        ],
        out_specs=pl.BlockSpec((tm, D), lambda i: (i, 0)),
        compiler_params=_params(1),
        name="mixer_out_ffn",
    )(h, a, w_a, g_ffn, w_gu, w_d, g_final)


def _rope_slot(x, cos, sin_lo, sin_hi):
    return x * cos + pltpu.roll(x, 96, 1) * sin_lo + pltpu.roll(x, 32, 1) * sin_hi


def _mla_proj_kernel(h_ref, gn_ref, win_ref, gq_ref, gkv_ref, wuq_ref, wukt_ref,
                     cos_ref, sinlo_ref, sinhi_ref,
                     q_ref, kcat_ref, lat_ref, kr_ref, *maybe_vt_ref, H, vt_tile):
    xn = _rms(h_ref[...], gn_ref[...]).astype(BF16)
    t = jnp.dot(xn, win_ref[...], preferred_element_type=F32)
    c_q = t[:, :Q_LORA]
    c_kv = t[:, Q_LORA:Q_LORA + KV_LORA]
    k_slot = t[:, Q_LORA + KV_LORA:]
    cos, sin_lo, sin_hi = cos_ref[...], sinlo_ref[...], sinhi_ref[...]

    lat = _rms(c_kv, gkv_ref[...])
    k_rot = _rope_slot(k_slot, cos, sin_lo, sin_hi)
    lat_ref[...] = lat
    kr_ref[...] = k_rot[:, :QK_ROPE]
    kcat_ref[:, :KV_LORA] = lat.astype(BF16)
    kcat_ref[:, KV_LORA:] = k_rot.astype(BF16)
    if vt_tile:
        (vt_ref,) = maybe_vt_ref
        for t in range(lat.shape[0] // vt_tile):
            vt_ref[t] = lat[t * vt_tile:(t + 1) * vt_tile, :].T.astype(BF16)

    cqn = _rms(c_q, gq_ref[...]).astype(BF16)
    q = jnp.dot(cqn, wuq_ref[...], preferred_element_type=F32)
    for h in range(H):
        q_nope = q[:, h * QK_NOPE:(h + 1) * QK_NOPE].astype(BF16)
        q_lat = jnp.dot(q_nope, wukt_ref[h], preferred_element_type=F32)
        q_rot = _rope_slot(q[:, (H + h) * LANES:(H + h + 1) * LANES], cos, sin_lo, sin_hi)
        q_ref[:, h * Q_SLOT:h * Q_SLOT + KV_LORA] = q_lat.astype(BF16)
        q_ref[:, h * Q_SLOT + KV_LORA:(h + 1) * Q_SLOT] = q_rot.astype(BF16)


def _mla_project(h, g_norm, w_in, g_q, g_kv, w_uq, w_ukt, cos, sin_lo, sin_hi, tm, table_blocks, vt_tile):
    M, D = h.shape
    H = MLA_HEADS
    kern = functools.partial(_mla_proj_kernel, H=H, vt_tile=vt_tile)
    table = pl.BlockSpec((tm, LANES), lambda i: (i % table_blocks, 0))
    out_shape = [
        jax.ShapeDtypeStruct((M, H * Q_SLOT), BF16),
        jax.ShapeDtypeStruct((M, Q_SLOT), BF16),
        jax.ShapeDtypeStruct((M, KV_LORA), F32),
        jax.ShapeDtypeStruct((M, QK_ROPE), F32),
    ]
    out_specs = [
        pl.BlockSpec((tm, H * Q_SLOT), lambda i: (i, 0)),
        pl.BlockSpec((tm, Q_SLOT), lambda i: (i, 0)),
        pl.BlockSpec((tm, KV_LORA), lambda i: (i, 0)),
        pl.BlockSpec((tm, QK_ROPE), lambda i: (i, 0)),
    ]
    if vt_tile:
        out_shape.append(jax.ShapeDtypeStruct((M // vt_tile, KV_LORA, vt_tile), BF16))
        out_specs.append(pl.BlockSpec((tm // vt_tile, KV_LORA, vt_tile), lambda i: (i, 0, 0)))
    return pl.pallas_call(
        kern,
        out_shape=tuple(out_shape),
        grid=(M // tm,),
        in_specs=[
            pl.BlockSpec((tm, D), lambda i: (i, 0)),
            _resident((1, D)),
            _resident(w_in.shape),
            _resident((1, Q_LORA)),
            _resident((1, KV_LORA)),
            _resident(w_uq.shape),
            _resident(w_ukt.shape),
            table, table, table,
        ],
        out_specs=tuple(out_specs),
        compiler_params=_params(1),
        name="mla_project",
    )(h, g_norm, w_in, g_q, g_kv, w_uq, w_ukt, cos, sin_lo, sin_hi)


def _attn_kernel(q_ref, k_ref, vt_ref, wuv_ref, o_ref, qs_ref, m_ref, l_ref, acc_ref, *, H, T):
    i = pl.program_id(1)
    for h in range(H):
        qs_ref[h * T:(h + 1) * T, :] = q_ref[:, h * Q_SLOT:(h + 1) * Q_SLOT]
    m_ref[...] = jnp.full_like(m_ref, -jnp.inf)
    l_ref[...] = jnp.zeros_like(l_ref)
    acc_ref[...] = jnp.zeros_like(acc_ref)

    def step(j, masked):
        kj = k_ref[0, pl.ds(pl.multiple_of(j * T, T), T), :]
        vtj = vt_ref[j]
        st = lax.dot_general(kj, qs_ref[...], NT_DIMS, preferred_element_type=F32) * MLA_SCALE
        if masked:
            key = lax.broadcasted_iota(jnp.int32, st.shape, 0)
            qry = lax.broadcasted_iota(jnp.int32, st.shape, 1) & (T - 1)
            st = jnp.where(key <= qry, st, -jnp.inf)
        m_prev = m_ref[...]
        m_new = jnp.maximum(m_prev, jnp.max(st, axis=0, keepdims=True))
        alpha = jnp.exp(m_prev - m_new)
        p = jnp.exp(st - m_new)
        l_ref[...] = alpha * l_ref[...] + jnp.sum(p, axis=0, keepdims=True)
        acc_ref[...] = alpha * acc_ref[...] + jnp.dot(vtj, p.astype(BF16), preferred_element_type=F32)
        m_ref[...] = m_new

    def body(j, carry):
        step(j, masked=False)
        return carry

    lax.fori_loop(0, i, body, 0)
    step(i, masked=True)

    o_t = acc_ref[...] / l_ref[...]
    for h in range(H):
        o_lat = o_t[:, h * T:(h + 1) * T].T.astype(BF16)
        v_h = jnp.dot(o_lat, wuv_ref[h], preferred_element_type=F32)
        o_ref[:, h * V_DIM:(h + 1) * V_DIM] = v_h.astype(BF16)


def _attention_prompt(q, kcat, vt, w_uv, B, S, T):
    H = MLA_HEADS
    NQ = S // T
    kern = functools.partial(_attn_kernel, H=H, T=T)
    return pl.pallas_call(
        kern,
        out_shape=jax.ShapeDtypeStruct((B * S, H * V_DIM), BF16),
        grid=(B, NQ),
        in_specs=[
            pl.BlockSpec((T, H * Q_SLOT), lambda b, i: (b * NQ + i, 0)),
            pl.BlockSpec((1, S, Q_SLOT), lambda b, i: (b, 0, 0)),
            pl.BlockSpec((NQ, KV_LORA, T), lambda b, i: (b, 0, 0)),
            _resident(w_uv.shape),
        ],
        out_specs=pl.BlockSpec((T, H * V_DIM), lambda b, i: (b * NQ + i, 0)),
        scratch_shapes=[
            pltpu.VMEM((H * T, Q_SLOT), BF16),
            pltpu.VMEM((1, H * T), F32),
            pltpu.VMEM((1, H * T), F32),
            pltpu.VMEM((KV_LORA, H * T), F32),
        ],
        compiler_params=_params(2),
        name="attention_prompt",
    )(q, kcat.reshape(B, S, Q_SLOT), vt, w_uv)


def _decode_kernel(pt_ref, q_ref, cn_ref, krn_ref, lat_hbm, krt_hbm, o_ref,
                   lat_buf, kr_buf, sem, *, layer, G, P, NCH, RING):
    b = pl.program_id(0)
    nb = pl.num_programs(0)
    total = nb * NCH

    def page_copies(c):
        slot = lax.rem(c, RING)
        cw = jnp.where(c >= total, c - total, c)
        bb, jj = lax.div(cw, NCH), lax.rem(cw, NCH)
        copies = []
        for g in range(G):
            page = pt_ref[bb, jj * G + g]
            copies.append(pltpu.make_async_copy(
                lat_hbm.at[layer, page], lat_buf.at[slot, pl.ds(g * P, P), :], sem.at[0, slot]))
            copies.append(pltpu.make_async_copy(
                krt_hbm.at[layer, page], kr_buf.at[slot, :, pl.ds(g * P, P)], sem.at[1, slot]))
        return copies

    def start(c):
        for cp in page_copies(c):
            cp.start()

    def wait(c):
        for cp in page_copies(c):
            cp.wait()

    @pl.when(b == 0)
    def _():
        for c in range(RING - 1):
            start(jnp.int32(c))

    q = q_ref[0].astype(F32)
    q_lat = q[:, :KV_LORA]
    q_rope = q[:, KV_LORA:KV_LORA + QK_ROPE]
    c_new = cn_ref[0]
    kr_new = krn_ref[0]

    s_new = (jnp.sum(q_lat * c_new, axis=1, keepdims=True)
             + jnp.sum(q_rope * kr_new, axis=1, keepdims=True)) * MLA_SCALE
    m0 = s_new
    l0 = jnp.ones_like(s_new)
    acc0 = jnp.broadcast_to(c_new, (q.shape[0], KV_LORA)).astype(F32)

    def chunk(j, carry):
        m_prev, l_prev, acc = carry
        c = b * NCH + j
        slot = lax.rem(c, RING)
        wait(c)
        kl = lat_buf[slot]
        krt = kr_buf[slot]
        s = (lax.dot_general(q_lat, kl, NT_DIMS, preferred_element_type=F32)
             + jnp.dot(q_rope, krt, preferred_element_type=F32)) * MLA_SCALE
        m_new = jnp.maximum(m_prev, jnp.max(s, axis=1, keepdims=True))
        alpha = jnp.exp(m_prev - m_new)
        p = jnp.exp(s - m_new)
        l_new = alpha * l_prev + jnp.sum(p, axis=1, keepdims=True)
        acc_new = alpha * acc + jnp.dot(p, kl, preferred_element_type=F32)
        start(c + (RING - 1))
        return m_new, l_new, acc_new

    m_fin, l_fin, acc_fin = lax.fori_loop(0, NCH, chunk, (m0, l0, acc0))
    o_ref[0] = acc_fin / l_fin

    @pl.when(b == nb - 1)
    def _():
        for c in range(RING - 1):
            wait(total + c)


def _attention_sample(q3, c_new, kr_new, cache_latent, cache_k_rope_t, page_table, layer, G):
    B, H, _ = q3.shape
    n_pages = page_table.shape[1]
    P = cache_latent.shape[2]
    NCH = n_pages // G
    assert n_pages % G == 0
    RING = 3
    kern = functools.partial(_decode_kernel, layer=layer, G=G, P=P, NCH=NCH, RING=RING)
    grid_spec = pltpu.PrefetchScalarGridSpec(
        num_scalar_prefetch=1,
        grid=(B,),
        in_specs=[
            pl.BlockSpec((1, H, Q_SLOT), lambda b, pt: (b, 0, 0)),
            pl.BlockSpec((1, 1, KV_LORA), lambda b, pt: (b, 0, 0)),
            pl.BlockSpec((1, 1, QK_ROPE), lambda b, pt: (b, 0, 0)),
            pl.BlockSpec(memory_space=pl.ANY),
            pl.BlockSpec(memory_space=pl.ANY),
        ],
        out_specs=pl.BlockSpec((1, H, KV_LORA), lambda b, pt: (b, 0, 0)),
        scratch_shapes=[
            pltpu.VMEM((RING, G * P, KV_LORA), F32),
            pltpu.VMEM((RING, QK_ROPE, G * P), F32),
            pltpu.SemaphoreType.DMA((2, RING)),
        ],
    )
    return pl.pallas_call(
        kern,
        out_shape=jax.ShapeDtypeStruct((B, H, KV_LORA), F32),
        grid_spec=grid_spec,
        compiler_params=_params(1),
        name="attention_sample",
    )(page_table, q3, c_new.reshape(B, 1, KV_LORA), kr_new.reshape(B, 1, QK_ROPE),
      cache_latent, cache_k_rope_t)


def _value_up_kernel(o_ref, wuv_ref, v_ref, *, H):
    for h in range(H):
        o_h = o_ref[:, h * KV_LORA:(h + 1) * KV_LORA].astype(BF16)
        v_ref[:, h * V_DIM:(h + 1) * V_DIM] = jnp.dot(
            o_h, wuv_ref[h], preferred_element_type=F32).astype(BF16)


def _value_up(o_lat, w_uv):
    M = o_lat.shape[0]
    H = MLA_HEADS
    return pl.pallas_call(
        functools.partial(_value_up_kernel, H=H),
        out_shape=jax.ShapeDtypeStruct((M, H * V_DIM), BF16),
        grid=(1,),
        in_specs=[_resident(o_lat.shape), _resident(w_uv.shape)],
        out_specs=_resident((M, H * V_DIM)),
        compiler_params=_params(1),
        name="value_up",
    )(o_lat, w_uv)


def _rope_tables(pos):
    half = QK_ROPE // 2
    inv = ROPE_THETA ** (-jnp.arange(0, QK_ROPE, 2, dtype=F32) / QK_ROPE)
    ang = pos[:, None] * inv[None, :]
    cos, sin = jnp.cos(ang), jnp.sin(ang)
    z = jnp.zeros_like(cos)
    cos_t = jnp.concatenate([cos, cos, z, z], axis=1)
    sin_lo = jnp.concatenate([-sin, z, z, z], axis=1)
    sin_hi = jnp.concatenate([z, sin, z, z], axis=1)
    return cos_t, sin_lo, sin_hi


def _pad_cols(w, n):
    return jnp.pad(w, ((0, 0), (0, n - w.shape[1])))


def kernel(x_prompt, x_sample, state_mlstm_C, state_mlstm_n, state_mlstm_m, cache_latent, cache_k_rope,
           page_table, norm_mix, norm_ffn, norm_final, mlstm_w_in, mlstm_b_gates, mlstm_g_head, mlstm_w_out,
           mla_w_in, mla_g_q, mla_g_kv, mla_w_uq, mla_w_uk, mla_w_uv, mla_w_o, ffn_w_gate_up, ffn_w_down):
    B, S, D = x_prompt.shape
    BS, T, _ = x_sample.shape
    assert T == 1, "sample group is one new token per sequence"
    depth = norm_mix.shape[0]
    H = MLA_HEADS
    past_len = page_table.shape[1] * cache_latent.shape[2]

    hp = x_prompt.reshape(B * S, D)
    hs = x_sample.reshape(BS, D)
    TM = 512
    row = lambda v: v.reshape(1, -1).astype(F32)

    pos_p = jnp.arange(S, dtype=F32)
    pos_s = jnp.broadcast_to(jnp.arange(T, dtype=F32) + past_len, (BS,))
    rope_p = _rope_tables(pos_p)
    rope_s = _rope_tables(pos_s)

    outs = {k: [] for k in ("C_p", "n_p", "m_p", "C_s", "n_s", "m_s", "lat_p", "kr_p", "lat_s", "kr_s")}
    for layer in range(depth):
        j = layer // 2
        last = layer == depth - 1
        g_mix = row(norm_mix[layer])
        if layer % 2 == 0:
            n_gates = 2 * ML_HEADS
            w_in = _pad_cols(mlstm_w_in[j], mlstm_w_in.shape[2] - n_gates + LANES).astype(BF16)
            bias = _pad_cols(mlstm_b_gates[j].reshape(1, -1), LANES).astype(F32)
            g_head = row(mlstm_g_head[j])
            w_a = mlstm_w_out[j].astype(BF16)

            xw_p = _norm_matmul(hp, g_mix, w_in, TM)
            a_p, C_p, n_p, m_p = _mlstm_prompt(xw_p, bias, g_head, B, S, L=256)
            xw_s = _norm_matmul(hs, g_mix, w_in, BS)
            a_s, C_s, n_s, m_s = _mlstm_sample(xw_s, mlstm_b_gates[j].astype(F32), g_head, state_mlstm_C[j],
                                               state_mlstm_n[j], state_mlstm_m[j], TB=16)
            outs["C_p"].append(C_p); outs["n_p"].append(n_p); outs["m_p"].append(m_p.reshape(B, -1))
            outs["C_s"].append(C_s); outs["n_s"].append(n_s); outs["m_s"].append(m_s.reshape(BS, -1))
        else:
            w_in = _pad_cols(mla_w_in[j], Q_LORA + KV_LORA + LANES).astype(BF16)
            wq = mla_w_uq[j].reshape(Q_LORA, H, QK_NOPE + QK_ROPE)
            wq_nope = wq[:, :, :QK_NOPE].reshape(Q_LORA, H * QK_NOPE)
            wq_rope = jnp.pad(wq[:, :, QK_NOPE:], ((0, 0), (0, 0), (0, LANES - QK_ROPE))).reshape(Q_LORA, H * LANES)
            w_uq = jnp.concatenate([wq_nope, wq_rope], axis=1).astype(BF16)
            w_ukt = jnp.transpose(mla_w_uk[j].reshape(KV_LORA, H, QK_NOPE), (1, 2, 0)).astype(BF16)
            w_uv = jnp.transpose(mla_w_uv[j].reshape(KV_LORA, H, V_DIM), (1, 0, 2)).astype(BF16)
            w_a = mla_w_o[j].astype(BF16)
            g_q, g_kv = row(mla_g_q[j]), row(mla_g_kv[j])

            T_ATT = 256
            q_p, kcat_p, lat_p, kr_p, vt_p = _mla_project(hp, g_mix, w_in, g_q, g_kv, w_uq, w_ukt, *rope_p,
                                                          tm=TM, table_blocks=S // TM, vt_tile=T_ATT)
            a_p = _attention_prompt(q_p, kcat_p, vt_p, w_uv, B, S, T=T_ATT)
            q_s, _, lat_s, kr_s = _mla_project(hs, g_mix, w_in, g_q, g_kv, w_uq, w_ukt, *rope_s,
                                               tm=BS, table_blocks=1, vt_tile=0)
            cache_k_rope_t = jnp.swapaxes(cache_k_rope, 2, 3)
            o_s = _attention_sample(q_s.reshape(BS, H, Q_SLOT), lat_s, kr_s, cache_latent, cache_k_rope_t,
                                    page_table, layer=j, G=32)
            a_s = _value_up(o_s.reshape(BS, H * KV_LORA), w_uv)
            outs["lat_p"].append(lat_p.reshape(B, S, KV_LORA)); outs["kr_p"].append(kr_p.reshape(B, S, QK_ROPE))
            outs["lat_s"].append(lat_s.reshape(BS, T, KV_LORA)); outs["kr_s"].append(kr_s.reshape(BS, T, QK_ROPE))

        g_ffn = row(norm_ffn[layer])
        g_fin = row(norm_final)
        w_gu = ffn_w_gate_up[layer].astype(BF16)
        w_d = ffn_w_down[layer].astype(BF16)
        hp = _mixer_out_ffn(hp, a_p, w_a, g_ffn, w_gu, w_d, g_fin, TM, final_norm=last)
        hs = _mixer_out_ffn(hs, a_s, w_a, g_ffn, w_gu, w_d, g_fin, BS, final_norm=last)

    st = jnp.stack
    return (hp.reshape(B, S, D), hs.reshape(BS, T, D),
            st(outs["C_p"]), st(outs["n_p"]), st(outs["m_p"]),
            st(outs["C_s"]), st(outs["n_s"]), st(outs["m_s"]),
            st(outs["lat_p"]), st(outs["kr_p"]), st(outs["lat_s"]), st(outs["kr_s"]))
```

```python
import functools

import jax
import jax.numpy as jnp
from jax import lax
from jax.experimental import pallas as pl
from jax.experimental.pallas import tpu as pltpu

F32 = jnp.float32
BF16 = jnp.bfloat16
EPS = 1e-6
ROPE_THETA = 10000.0

V7X_VMEM_BYTES = 64 * 1024 * 1024
LANES = 128
VMEM_LIMIT_BYTES = V7X_VMEM_BYTES - 8 * 1024 * 1024

ML_HEADS = 8
MLA_HEADS = 8
Q_LORA = 384
KV_LORA = 256
QK_NOPE = 128
QK_ROPE = 64
V_DIM = 128
Q_SLOT = KV_LORA + LANES
MLA_SCALE = (QK_NOPE + QK_ROPE) ** -0.5

NT_DIMS = (((1,), (1,)), ((), ()))


def _params(n_grid_axes):
    return pltpu.CompilerParams(
        dimension_semantics=("arbitrary",) * n_grid_axes,
        vmem_limit_bytes=VMEM_LIMIT_BYTES,
    )


def _rms(x, g):
    return x * lax.rsqrt(jnp.mean(x * x, axis=-1, keepdims=True) + EPS) * g


def _log_sigmoid(x):
    return jnp.minimum(x, 0.0) - jnp.log1p(jnp.exp(-jnp.abs(x)))


def _resident(shape):
    nd = len(shape)
    return pl.BlockSpec(shape, lambda *_: (0,) * nd)


def _norm_matmul_kernel(x_ref, g_ref, w_ref, o_ref):
    xn = _rms(x_ref[...], g_ref[...]).astype(BF16)
    o_ref[...] = jnp.dot(xn, w_ref[...], preferred_element_type=F32)


def _norm_matmul(x, g, w, tm):
    M, D = x.shape
    N = w.shape[1]
    return pl.pallas_call(
        _norm_matmul_kernel,
        out_shape=jax.ShapeDtypeStruct((M, N), F32),
        grid=(M // tm,),
        in_specs=[pl.BlockSpec((tm, D), lambda i: (i, 0)), _resident((1, D)), _resident((D, N))],
        out_specs=pl.BlockSpec((tm, N), lambda i: (i, 0)),
        compiler_params=_params(1),
        name="norm_matmul",
    )(x, g, w)


def _mlstm_kernel(q_ref, k_ref, v_ref, o_ref, gt_ref, bias_ref, gh_ref,
                  hg_ref, c_out_ref, n_out_ref, m_out_ref,
                  caug_ref, mst_ref, *, L, H, DK, DV):
    c = pl.program_id(1)
    n_chunks = pl.num_programs(1)

    @pl.when(c == 0)
    def _():
        caug_ref[...] = jnp.zeros_like(caug_ref)
        mst_ref[...] = jnp.zeros_like(mst_ref)

    gates = gt_ref[...] + bias_ref[...]
    lane = lax.broadcasted_iota(jnp.int32, gates.shape, 1)
    G = jnp.where(lane < H, gates, _log_sigmoid(gates))
    row = lax.broadcasted_iota(jnp.int32, (L, L), 0)
    col = lax.broadcasted_iota(jnp.int32, (L, L), 1)
    causal = col <= row
    cs = jnp.dot(causal.astype(F32), G, precision=lax.Precision.HIGHEST,
                 preferred_element_type=F32)
    GT = G.T
    csT = cs.T

    kT = (k_ref[...] * (DK ** -0.5)).T
    ones = jnp.ones((L, DV), F32)
    heads = range(H)


    b_col = [cs[:, H + h:H + h + 1] for h in heads]
    b_row = [csT[H + h:H + h + 1, :] for h in heads]
    i_row = [GT[h:h + 1, :] for h in heads]
    r_mat = [jnp.where(causal, i_row[h] - b_row[h], -jnp.inf) for h in heads]
    r_max = [jnp.max(r_mat[h], axis=1, keepdims=True) for h in heads]

    m_prev = [mst_ref[h:h + 1, 0:1] for h in heads]
    log_inter = [b_col[h] + m_prev[h] for h in heads]
    m_t = [jnp.maximum(log_inter[h], b_col[h] + r_max[h]) for h in heads]
    w_inter = [jnp.exp(log_inter[h] - m_t[h]) for h in heads]
    W = [jnp.exp(r_mat[h] + (b_col[h] - m_t[h])) for h in heads]

    lane_pair = lax.broadcasted_iota(jnp.int32, (L, 2 * DK), 1)
    S, inter, caug_prev = [None] * H, [None] * H, [None] * H
    for p in range(H // 2):
        q_pair = q_ref[:, p * 2 * DK:(p + 1) * 2 * DK]
        q_lo = jnp.where(lane_pair < DK, q_pair, 0.0)
        q_hi = jnp.where(lane_pair < DK, 0.0, q_pair)
        qm2 = jnp.concatenate([q_lo, q_hi], axis=0).astype(BF16)
        k_pair = (k_ref[:, p * 2 * DK:(p + 1) * 2 * DK] * (DK ** -0.5)).astype(BF16)
        s2 = lax.dot_general(qm2, k_pair, NT_DIMS, preferred_element_type=F32)
        caug_pair = caug_ref[p]
        i2 = jnp.dot(qm2, caug_pair.astype(BF16), preferred_element_type=F32)
        for half in range(2):
            h = 2 * p + half
            S[h] = s2[half * L:(half + 1) * L] * W[h]
            inter[h] = i2[half * L:(half + 1) * L]
            caug_prev[h] = caug_pair[half * DK:(half + 1) * DK, :]

    m_new = [m_t[h][L - 1:L, :] for h in heads]
    b_last = [cs[L - 1:L, H + h:H + h + 1] for h in heads]
    R = []
    for h in heads:
        w_s_row = jnp.exp(b_last[h] - b_row[h] + i_row[h] - m_new[h])
        kwT = (kT[h * DK:(h + 1) * DK, :] * w_s_row).astype(BF16)
        vaug = jnp.concatenate([v_ref[:, h * DV:(h + 1) * DV], ones], axis=1).astype(BF16)
        lhs = jnp.concatenate([S[h].astype(BF16), kwT], axis=0)
        R.append(jnp.dot(lhs, vaug, preferred_element_type=F32))

    hh = []
    for h in heads:
        numden = w_inter[h] * inter[h] + R[h][:L]
        num, den = numden[:, :DV], numden[:, DV:]
        hh.append(num / jnp.maximum(jnp.abs(den), jnp.exp(-m_t[h])))
    ms = [jnp.mean(hh[h] * hh[h], axis=-1, keepdims=True) for h in heads]
    for h in heads:
        hn = hh[h] * lax.rsqrt(ms[h] + EPS)
        og = jax.nn.sigmoid(o_ref[:, h * DV:(h + 1) * DV])
        hg_ref[:, h * DV:(h + 1) * DV] = (og * (hn * gh_ref[:, h * DV:(h + 1) * DV])).astype(BF16)

    for h in heads:
        p, r0 = h // 2, (h % 2) * DK
        w_c = jnp.exp(b_last[h] + m_prev[h] - m_new[h])
        caug_ref[p, r0:r0 + DK, :] = w_c * caug_prev[h] + R[h][L:]
        mst_ref[h:h + 1, :] = jnp.broadcast_to(m_new[h], (1, LANES))

    @pl.when(c == n_chunks - 1)
    def _():
        pick0 = (lax.broadcasted_iota(jnp.int32, (8, DV), 1) == 0).astype(F32)
        for h in range(H):
            p, r0 = h // 2, (h % 2) * DK
            ca = caug_ref[p, r0:r0 + DK, :]
            c_out_ref[0, h] = ca[:, :DV]
            n_rows = lax.dot_general(pick0, ca[:, DV:], NT_DIMS, precision=lax.Precision.HIGHEST,
                                     preferred_element_type=F32)
            n_out_ref[0, h:h + 1, :] = n_rows[0:1, :]
            m_out_ref[0, :, h:h + 1] = mst_ref[h:h + 1, 0:1]


def _mlstm_prompt(xw, bias, g_head, B, S, L):
    H, DK, DV = ML_HEADS, 64, 128
    NC = S // L
    kern = functools.partial(_mlstm_kernel, L=L, H=H, DK=DK, DV=DV)
    qk_w, v_w = H * DK, H * DV
    rows = lambda b, c: b * NC + c
    gate_blk = (2 * qk_w + 2 * v_w) // LANES
    return pl.pallas_call(
        kern,
        out_shape=(
            jax.ShapeDtypeStruct((B * S, v_w), BF16),
            jax.ShapeDtypeStruct((B, H, DK, DV), F32),
            jax.ShapeDtypeStruct((B, H, DK), F32),
            jax.ShapeDtypeStruct((B, 1, H), F32),
        ),
        grid=(B, NC),
        in_specs=[
            pl.BlockSpec((L, qk_w), lambda b, c: (rows(b, c), 0)),
            pl.BlockSpec((L, qk_w), lambda b, c: (rows(b, c), 1)),
            pl.BlockSpec((L, v_w), lambda b, c: (rows(b, c), 1)),
            pl.BlockSpec((L, v_w), lambda b, c: (rows(b, c), 2)),
            pl.BlockSpec((L, LANES), lambda b, c: (rows(b, c), gate_blk)),
            _resident((1, LANES)),
            _resident((1, v_w)),
        ],
        out_specs=(
            pl.BlockSpec((L, v_w), lambda b, c: (rows(b, c), 0)),
            pl.BlockSpec((1, H, DK, DV), lambda b, c: (b, 0, 0, 0)),
            pl.BlockSpec((1, H, DK), lambda b, c: (b, 0, 0)),
            pl.BlockSpec((1, 1, H), lambda b, c: (b, 0, 0)),
        ),
        scratch_shapes=[
            pltpu.VMEM((H // 2, 2 * DK, 2 * DV), F32),
            pltpu.VMEM((H, LANES), F32),
        ],
        compiler_params=_params(2),
        name="mlstm_prompt",
    )(xw, xw, xw, xw, xw, bias, g_head)


def _mlstm_step_kernel(q_ref, k_ref, v_ref, o_ref, gi_ref, gf_ref, bi_ref, bf_ref, gh_ref,
                       c0_ref, n0_ref, m0_ref,
                       hg_ref, c_out_ref, n_out_ref, m_out_ref, *, H, DK, DV, TB):
    RB = TB * H
    i_pre = gi_ref[...] + bi_ref[...]
    log_f = _log_sigmoid(gf_ref[...] + bf_ref[...])
    log_inter = log_f + m0_ref[...]
    m_t = jnp.maximum(log_inter, i_pre)
    w_inter = jnp.exp(log_inter - m_t)
    w_intra = jnp.exp(i_pre - m_t)

    q = q_ref[...]
    k = k_ref[...] * (DK ** -0.5)
    v = v_ref[...]
    n_prev = n0_ref[...]
    s = jnp.sum(q * k, axis=1, keepdims=True) * w_intra
    den = w_inter * jnp.sum(q * n_prev, axis=1, keepdims=True) + s

    def block_diag(x):
        wide = jnp.concatenate([x] * H, axis=1)
        head_of_lane = lax.broadcasted_iota(jnp.int32, wide.shape, 1) // DK
        head_of_row = lax.broadcasted_iota(jnp.int32, wide.shape, 0) % H
        return jnp.where(head_of_lane == head_of_row, wide, 0.0)

    q_bd = block_diag(q)
    kw_bd = block_diag(k * w_intra[:, :DK])
    qc = jnp.concatenate(
        [jnp.dot(q_bd[t * H:(t + 1) * H, :], c0_ref[t], preferred_element_type=F32) for t in range(TB)],
        axis=0)

    num = w_inter * qc + s * v
    hh = num / jnp.maximum(jnp.abs(den), jnp.exp(-m_t))
    hn = hh * lax.rsqrt(jnp.mean(hh * hh, axis=-1, keepdims=True) + EPS)
    hg_ref[...] = (jax.nn.sigmoid(o_ref[...]) * (hn * gh_ref[...])).astype(BF16)
    n_out_ref[...] = w_inter[:, :DK] * n_prev + w_intra[:, :DK] * k
    m_out_ref[...] = m_t

    for t in range(TB):
        rows = slice(t * H, (t + 1) * H)
        d_c = lax.dot_general(kw_bd[rows, :], v[rows, :], (((0,), (0,)), ((), ())),
                              preferred_element_type=F32)
        for h in range(H):
            blk = slice(h * DK, (h + 1) * DK)
            c_out_ref[t, blk, :] = w_inter[t * H + h:t * H + h + 1, :] * c0_ref[t, blk, :] + d_c[blk, :]


def _mlstm_sample(xw, b_gates, g_head, c0, n0, m0, TB):
    H, DK, DV = ML_HEADS, 64, 128
    B = xw.shape[0]
    R, RB = B * H, TB * H
    qk_w, v_w = H * DK, H * DV
    lanes = lambda x: jnp.broadcast_to(x.reshape(-1, 1), (x.size, LANES))
    per_block = lambda x: jnp.tile(x, (TB, 1))
    q = xw[:, :qk_w].reshape(R, DK)
    k = xw[:, qk_w:2 * qk_w].reshape(R, DK)
    v = xw[:, 2 * qk_w:2 * qk_w + v_w].reshape(R, DV)
    o = xw[:, 2 * qk_w + v_w:2 * qk_w + 2 * v_w].reshape(R, DV)
    g0 = 2 * qk_w + 2 * v_w
    operands = (
        q, k, v, o,
        lanes(xw[:, g0:g0 + H]), lanes(xw[:, g0 + H:g0 + 2 * H]),
        per_block(lanes(b_gates[:H])), per_block(lanes(b_gates[H:])),
        per_block(g_head.reshape(H, DV)),
        c0.reshape(B, H * DK, DV), n0.reshape(R, DK), lanes(m0),
    )
    row_blk = lambda n: pl.BlockSpec((RB, n), lambda b: (b, 0))
    state_blk = pl.BlockSpec((TB, H * DK, DV), lambda b: (b, 0, 0))
    kern = functools.partial(_mlstm_step_kernel, H=H, DK=DK, DV=DV, TB=TB)
    hg, c_new, n_new, m_new = pl.pallas_call(
        kern,
        out_shape=(
            jax.ShapeDtypeStruct((R, DV), BF16),
            jax.ShapeDtypeStruct((B, H * DK, DV), F32),
            jax.ShapeDtypeStruct((R, DK), F32),
            jax.ShapeDtypeStruct((R, LANES), F32),
        ),
        grid=(B // TB,),
        in_specs=[
            row_blk(DK), row_blk(DK), row_blk(DV), row_blk(DV),
            row_blk(LANES), row_blk(LANES),
            _resident((RB, LANES)), _resident((RB, LANES)), _resident((RB, DV)),
            state_blk, row_blk(DK), row_blk(LANES),
        ],
        out_specs=(row_blk(DV), state_blk, row_blk(DK), row_blk(LANES)),
        compiler_params=_params(1),
        name="mlstm_sample",
    )(*operands)
    return (hg.reshape(B, v_w), c_new.reshape(B, H, DK, DV), n_new.reshape(B, H, DK),
            m_new[:, 0].reshape(B, H))


def _ffn_kernel(h_ref, a_ref, wa_ref, gn_ref, wgu_ref, wd_ref, gf_ref, o_ref, *, d_ff, tf, final_norm):
    h1 = h_ref[...] + jnp.dot(a_ref[...], wa_ref[...], preferred_element_type=F32)
    xn = _rms(h1, gn_ref[...]).astype(BF16)
    acc = h1
    for c in range(d_ff // tf):
        g = jnp.dot(xn, wgu_ref[:, c * tf:(c + 1) * tf], preferred_element_type=F32)
        u = jnp.dot(xn, wgu_ref[:, d_ff + c * tf:d_ff + (c + 1) * tf], preferred_element_type=F32)
        act = (g * jax.nn.sigmoid(g) * u).astype(BF16)
        acc = acc + jnp.dot(act, wd_ref[c * tf:(c + 1) * tf, :], preferred_element_type=F32)
    if final_norm:
        acc = _rms(acc, gf_ref[...])
    o_ref[...] = acc


def _mixer_out_ffn(h, a, w_a, g_ffn, w_gu, w_d, g_final, tm, final_norm):
    M, D = h.shape
    KA = a.shape[1]
    d_ff = w_d.shape[0]
    kern = functools.partial(_ffn_kernel, d_ff=d_ff, tf=256, final_norm=final_norm)
    single = pl.Buffered(1)
    return pl.pallas_call(
        kern,
        out_shape=jax.ShapeDtypeStruct((M, D), F32),
        grid=(M // tm,),
        in_specs=[
            pl.BlockSpec((tm, D), lambda i: (i, 0)),
            pl.BlockSpec((tm, KA), lambda i: (i, 0)),
            pl.BlockSpec((KA, D), lambda i: (0, 0), pipeline_mode=single),
            _resident((1, D)),
            pl.BlockSpec((D, 2 * d_ff), lambda i: (0, 0), pipeline_mode=single),
            pl.BlockSpec((d_ff, D), lambda i: (0, 0), pipeline_mode=single),
            _resident((1, D)),
        ],
        out_specs=pl.BlockSpec((tm, D), lambda i: (i, 0)),
        compiler_params=_params(1),
        name="mixer_out_ffn",
    )(h, a, w_a, g_ffn, w_gu, w_d, g_final)


def _rope_slot(x, cos, sin_lo, sin_hi):
    return x * cos + pltpu.roll(x, 96, 1) * sin_lo + pltpu.roll(x, 32, 1) * sin_hi


def _mla_proj_kernel(h_ref, gn_ref, win_ref, gq_ref, gkv_ref, wuq_ref, wukt_ref,
                     cos_ref, sinlo_ref, sinhi_ref,
                     q_ref, kcat_ref, lat_ref, kr_ref, *maybe_vt_ref, H, vt_tile):
    xn = _rms(h_ref[...], gn_ref[...]).astype(BF16)
    t = jnp.dot(xn, win_ref[...], preferred_element_type=F32)
    c_q = t[:, :Q_LORA]
    c_kv = t[:, Q_LORA:Q_LORA + KV_LORA]
    k_slot = t[:, Q_LORA + KV_LORA:]
    cos, sin_lo, sin_hi = cos_ref[...], sinlo_ref[...], sinhi_ref[...]

    lat = _rms(c_kv, gkv_ref[...])
    k_rot = _rope_slot(k_slot, cos, sin_lo, sin_hi)
    lat_ref[...] = lat
    kr_ref[...] = k_rot[:, :QK_ROPE]
    kcat_ref[:, :KV_LORA] = lat.astype(BF16)
    kcat_ref[:, KV_LORA:] = k_rot.astype(BF16)
    if vt_tile:
        (vt_ref,) = maybe_vt_ref
        for t in range(lat.shape[0] // vt_tile):
            vt_ref[t] = lat[t * vt_tile:(t + 1) * vt_tile, :].T.astype(BF16)

    cqn = _rms(c_q, gq_ref[...]).astype(BF16)
    q = jnp.dot(cqn, wuq_ref[...], preferred_element_type=F32)
    for h in range(H):
        q_nope = q[:, h * QK_NOPE:(h + 1) * QK_NOPE].astype(BF16)
        q_lat = jnp.dot(q_nope, wukt_ref[h], preferred_element_type=F32)
        q_rot = _rope_slot(q[:, (H + h) * LANES:(H + h + 1) * LANES], cos, sin_lo, sin_hi)
        q_ref[:, h * Q_SLOT:h * Q_SLOT + KV_LORA] = q_lat.astype(BF16)
        q_ref[:, h * Q_SLOT + KV_LORA:(h + 1) * Q_SLOT] = q_rot.astype(BF16)


def _mla_project(h, g_norm, w_in, g_q, g_kv, w_uq, w_ukt, cos, sin_lo, sin_hi, tm, table_blocks, vt_tile):
    M, D = h.shape
    H = MLA_HEADS
    kern = functools.partial(_mla_proj_kernel, H=H, vt_tile=vt_tile)
    table = pl.BlockSpec((tm, LANES), lambda i: (i % table_blocks, 0))
    out_shape = [
        jax.ShapeDtypeStruct((M, H * Q_SLOT), BF16),
        jax.ShapeDtypeStruct((M, Q_SLOT), BF16),
        jax.ShapeDtypeStruct((M, KV_LORA), F32),
        jax.ShapeDtypeStruct((M, QK_ROPE), F32),
    ]
    out_specs = [
        pl.BlockSpec((tm, H * Q_SLOT), lambda i: (i, 0)),
        pl.BlockSpec((tm, Q_SLOT), lambda i: (i, 0)),
        pl.BlockSpec((tm, KV_LORA), lambda i: (i, 0)),
        pl.BlockSpec((tm, QK_ROPE), lambda i: (i, 0)),
    ]
    if vt_tile:
        out_shape.append(jax.ShapeDtypeStruct((M // vt_tile, KV_LORA, vt_tile), BF16))
        out_specs.append(pl.BlockSpec((tm // vt_tile, KV_LORA, vt_tile), lambda i: (i, 0, 0)))
    return pl.pallas_call(
        kern,
        out_shape=tuple(out_shape),
        grid=(M // tm,),
        in_specs=[
            pl.BlockSpec((tm, D), lambda i: (i, 0)),
            _resident((1, D)),
            _resident(w_in.shape),
            _resident((1, Q_LORA)),
            _resident((1, KV_LORA)),
            _resident(w_uq.shape),
            _resident(w_ukt.shape),
            table, table, table,
        ],
        out_specs=tuple(out_specs),
        compiler_params=_params(1),
        name="mla_project",
    )(h, g_norm, w_in, g_q, g_kv, w_uq, w_ukt, cos, sin_lo, sin_hi)


def _attn_kernel(q_ref, k_ref, vt_ref, wuv_ref, o_ref, qs_ref, m_ref, l_ref, acc_ref, sa_ref, sb_ref,
                 *, H, T):
    i = pl.program_id(1)
    for h in range(H):
        qs_ref[h * T:(h + 1) * T, :] = q_ref[:, h * Q_SLOT:(h + 1) * Q_SLOT]
    m_ref[...] = jnp.full_like(m_ref, -jnp.inf)
    l_ref[...] = jnp.zeros_like(l_ref)
    acc_ref[...] = jnp.zeros_like(acc_ref)

    def scores(j):
        kj = k_ref[0, pl.ds(pl.multiple_of(j * T, T), T), :]
        return lax.dot_general(kj, qs_ref[...], NT_DIMS, preferred_element_type=F32) * MLA_SCALE

    def update(j, st):
        vtj = vt_ref[j]
        m_prev = m_ref[...]
        m_new = jnp.maximum(m_prev, jnp.max(st, axis=0, keepdims=True))
        alpha = jnp.exp(m_prev - m_new)
        p = jnp.exp(st - m_new)
        l_ref[...] = alpha * l_ref[...] + jnp.sum(p, axis=0, keepdims=True)
        acc_ref[...] = alpha * acc_ref[...] + jnp.dot(vtj, p.astype(BF16), preferred_element_type=F32)
        m_ref[...] = m_new

    st = scores(i)
    key = lax.broadcasted_iota(jnp.int32, st.shape, 0)
    qry = lax.broadcasted_iota(jnp.int32, st.shape, 1) & (T - 1)
    update(i, jnp.where(key <= qry, st, -jnp.inf))

    @pl.when(i > 0)
    def _():
        sa_ref[...] = scores(0)

    def pair(jj, carry):
        j = 2 * jj
        sb_ref[...] = scores(j + 1)
        update(j, sa_ref[...])
        sa_ref[...] = scores(jnp.minimum(j + 2, i - 1))
        update(j + 1, sb_ref[...])
        return carry

    lax.fori_loop(0, i // 2, pair, 0)

    @pl.when(i % 2 == 1)
    def _():
        update(i - 1, sa_ref[...])

    o_t = acc_ref[...] / l_ref[...]
    for h in range(H):
        o_lat = o_t[:, h * T:(h + 1) * T].T.astype(BF16)
        v_h = jnp.dot(o_lat, wuv_ref[h], preferred_element_type=F32)
        o_ref[:, h * V_DIM:(h + 1) * V_DIM] = v_h.astype(BF16)


def _attention_prompt(q, kcat, vt, w_uv, B, S, T):
    H = MLA_HEADS
    NQ = S // T
    kern = functools.partial(_attn_kernel, H=H, T=T)
    return pl.pallas_call(
        kern,
        out_shape=jax.ShapeDtypeStruct((B * S, H * V_DIM), BF16),
        grid=(B, NQ),
        in_specs=[
            pl.BlockSpec((T, H * Q_SLOT), lambda b, i: (b * NQ + i, 0)),
            pl.BlockSpec((1, S, Q_SLOT), lambda b, i: (b, 0, 0)),
            pl.BlockSpec((NQ, KV_LORA, T), lambda b, i: (b, 0, 0)),
            _resident(w_uv.shape),
        ],
        out_specs=pl.BlockSpec((T, H * V_DIM), lambda b, i: (b * NQ + i, 0)),
        scratch_shapes=[
            pltpu.VMEM((H * T, Q_SLOT), BF16),
            pltpu.VMEM((1, H * T), F32),
            pltpu.VMEM((1, H * T), F32),
            pltpu.VMEM((KV_LORA, H * T), F32),
            pltpu.VMEM((T, H * T), F32),
            pltpu.VMEM((T, H * T), F32),
        ],
        compiler_params=_params(2),
        name="attention_prompt",
    )(q, kcat.reshape(B, S, Q_SLOT), vt, w_uv)


def _decode_kernel(pt_ref, q_ref, cn_ref, krn_ref, lat_hbm, krt_hbm, o_ref,
                   lat_buf, kr_buf, sem, *, layer, G, P, NCH, RING, STREAMS):
    b = pl.program_id(0)
    nb = pl.num_programs(0)
    total = nb * NCH

    def page_copies(c):
        slot = lax.rem(c, RING)
        cw = jnp.where(c >= total, c - total, c)
        bb, jj = lax.div(cw, NCH), lax.rem(cw, NCH)
        copies = []
        for g in range(G):
            page = pt_ref[bb, jj * G + g]
            copies.append(pltpu.make_async_copy(
                lat_hbm.at[layer, page], lat_buf.at[slot, pl.ds(g * P, P), :], sem.at[0, slot]))
            copies.append(pltpu.make_async_copy(
                krt_hbm.at[layer, page], kr_buf.at[slot, :, pl.ds(g * P, P)], sem.at[1, slot]))
        return copies

    def start(c):
        for cp in page_copies(c):
            cp.start()

    def wait(c):
        for cp in page_copies(c):
            cp.wait()

    @pl.when(b == 0)
    def _():
        for c in range(RING - 1):
            start(jnp.int32(c))

    q = q_ref[0].astype(F32)
    q_lat = q[:, :KV_LORA]
    q_rope = q[:, KV_LORA:KV_LORA + QK_ROPE]
    c_new = cn_ref[0]
    kr_new = krn_ref[0]

    s_new = (jnp.sum(q_lat * c_new, axis=1, keepdims=True)
             + jnp.sum(q_rope * kr_new, axis=1, keepdims=True)) * MLA_SCALE
    n_heads = q.shape[0]
    m0 = (s_new,) + (jnp.full_like(s_new, -jnp.inf),) * (STREAMS - 1)
    l0 = (jnp.ones_like(s_new),) + (jnp.zeros_like(s_new),) * (STREAMS - 1)
    acc0 = ((jnp.broadcast_to(c_new, (n_heads, KV_LORA)).astype(F32),)
            + (jnp.zeros((n_heads, KV_LORA), F32),) * (STREAMS - 1))
    W = (G * P) // STREAMS

    def chunk(j, carry):
        m_prev, l_prev, acc = carry
        c = b * NCH + j
        slot = lax.rem(c, RING)
        wait(c)
        m_out, l_out, acc_out = [], [], []
        for i in range(STREAMS):
            kl = lat_buf[slot, i * W:(i + 1) * W, :]
            krt = kr_buf[slot, :, i * W:(i + 1) * W]
            s = (lax.dot_general(q_lat, kl, NT_DIMS, preferred_element_type=F32)
                 + jnp.dot(q_rope, krt, preferred_element_type=F32)) * MLA_SCALE
            m_new = jnp.maximum(m_prev[i], jnp.max(s, axis=1, keepdims=True))
            alpha = jnp.exp(m_prev[i] - m_new)
            p = jnp.exp(s - m_new)
            m_out.append(m_new)
            l_out.append(alpha * l_prev[i] + jnp.sum(p, axis=1, keepdims=True))
            acc_out.append(alpha * acc[i] + jnp.dot(p, kl, preferred_element_type=F32))
        start(c + (RING - 1))
        return tuple(m_out), tuple(l_out), tuple(acc_out)

    m_fin, l_fin, acc_fin = lax.fori_loop(0, NCH, chunk, (m0, l0, acc0))
    m_all = functools.reduce(jnp.maximum, m_fin)
    scale = [jnp.exp(m_i - m_all) for m_i in m_fin]
    l_all = sum(l_i * w_i for l_i, w_i in zip(l_fin, scale))
    acc_all = sum(a_i * w_i for a_i, w_i in zip(acc_fin, scale))
    o_ref[0] = acc_all / l_all

    @pl.when(b == nb - 1)
    def _():
        for c in range(RING - 1):
            wait(total + c)


def _attention_sample(q3, c_new, kr_new, cache_latent, cache_k_rope_t, page_table, layer, G):
    B, H, _ = q3.shape
    n_pages = page_table.shape[1]
    P = cache_latent.shape[2]
    NCH = n_pages // G
    assert n_pages % G == 0
    RING = 3
    kern = functools.partial(_decode_kernel, layer=layer, G=G, P=P, NCH=NCH, RING=RING, STREAMS=2)
    grid_spec = pltpu.PrefetchScalarGridSpec(
        num_scalar_prefetch=1,
        grid=(B,),
        in_specs=[
            pl.BlockSpec((1, H, Q_SLOT), lambda b, pt: (b, 0, 0)),
            pl.BlockSpec((1, 1, KV_LORA), lambda b, pt: (b, 0, 0)),
            pl.BlockSpec((1, 1, QK_ROPE), lambda b, pt: (b, 0, 0)),
            pl.BlockSpec(memory_space=pl.ANY),
            pl.BlockSpec(memory_space=pl.ANY),
        ],
        out_specs=pl.BlockSpec((1, H, KV_LORA), lambda b, pt: (b, 0, 0)),
        scratch_shapes=[
            pltpu.VMEM((RING, G * P, KV_LORA), F32),
            pltpu.VMEM((RING, QK_ROPE, G * P), F32),
            pltpu.SemaphoreType.DMA((2, RING)),
        ],
    )
    return pl.pallas_call(
        kern,
        out_shape=jax.ShapeDtypeStruct((B, H, KV_LORA), F32),
        grid_spec=grid_spec,
        compiler_params=_params(1),
        name="attention_sample",
    )(page_table, q3, c_new.reshape(B, 1, KV_LORA), kr_new.reshape(B, 1, QK_ROPE),
      cache_latent, cache_k_rope_t)


def _value_up_kernel(o_ref, wuv_ref, v_ref, *, H):
    for h in range(H):
        o_h = o_ref[:, h * KV_LORA:(h + 1) * KV_LORA].astype(BF16)
        v_ref[:, h * V_DIM:(h + 1) * V_DIM] = jnp.dot(
            o_h, wuv_ref[h], preferred_element_type=F32).astype(BF16)


def _value_up(o_lat, w_uv):
    M = o_lat.shape[0]
    H = MLA_HEADS
    return pl.pallas_call(
        functools.partial(_value_up_kernel, H=H),
        out_shape=jax.ShapeDtypeStruct((M, H * V_DIM), BF16),
        grid=(1,),
        in_specs=[_resident(o_lat.shape), _resident(w_uv.shape)],
        out_specs=_resident((M, H * V_DIM)),
        compiler_params=_params(1),
        name="value_up",
    )(o_lat, w_uv)


def _rope_tables(pos):
    half = QK_ROPE // 2
    inv = ROPE_THETA ** (-jnp.arange(0, QK_ROPE, 2, dtype=F32) / QK_ROPE)
    ang = pos[:, None] * inv[None, :]
    cos, sin = jnp.cos(ang), jnp.sin(ang)
    z = jnp.zeros_like(cos)
    cos_t = jnp.concatenate([cos, cos, z, z], axis=1)
    sin_lo = jnp.concatenate([-sin, z, z, z], axis=1)
    sin_hi = jnp.concatenate([z, sin, z, z], axis=1)
    return cos_t, sin_lo, sin_hi


def _pad_cols(w, n):
    return jnp.pad(w, ((0, 0), (0, n - w.shape[1])))


def kernel(x_prompt, x_sample, state_mlstm_C, state_mlstm_n, state_mlstm_m, cache_latent, cache_k_rope,
           page_table, norm_mix, norm_ffn, norm_final, mlstm_w_in, mlstm_b_gates, mlstm_g_head, mlstm_w_out,
           mla_w_in, mla_g_q, mla_g_kv, mla_w_uq, mla_w_uk, mla_w_uv, mla_w_o, ffn_w_gate_up, ffn_w_down):
    B, S, D = x_prompt.shape
    BS, T, _ = x_sample.shape
    assert T == 1, "sample group is one new token per sequence"
    depth = norm_mix.shape[0]
    H = MLA_HEADS
    past_len = page_table.shape[1] * cache_latent.shape[2]

    hp = x_prompt.reshape(B * S, D)
    hs = x_sample.reshape(BS, D)
    TM = 512
    row = lambda v: v.reshape(1, -1).astype(F32)

    pos_p = jnp.arange(S, dtype=F32)
    pos_s = jnp.broadcast_to(jnp.arange(T, dtype=F32) + past_len, (BS,))
    rope_p = _rope_tables(pos_p)
    rope_s = _rope_tables(pos_s)

    outs = {k: [] for k in ("C_p", "n_p", "m_p", "C_s", "n_s", "m_s", "lat_p", "kr_p", "lat_s", "kr_s")}
    for layer in range(depth):
        j = layer // 2
        last = layer == depth - 1
        g_mix = row(norm_mix[layer])
        if layer % 2 == 0:
            n_gates = 2 * ML_HEADS
            w_in = _pad_cols(mlstm_w_in[j], mlstm_w_in.shape[2] - n_gates + LANES).astype(BF16)
            bias = _pad_cols(mlstm_b_gates[j].reshape(1, -1), LANES).astype(F32)
            g_head = row(mlstm_g_head[j])
            w_a = mlstm_w_out[j].astype(BF16)

            xw_p = _norm_matmul(hp, g_mix, w_in, TM)
            a_p, C_p, n_p, m_p = _mlstm_prompt(xw_p, bias, g_head, B, S, L=256)
            xw_s = _norm_matmul(hs, g_mix, w_in, BS)
            a_s, C_s, n_s, m_s = _mlstm_sample(xw_s, mlstm_b_gates[j].astype(F32), g_head, state_mlstm_C[j],
                                               state_mlstm_n[j], state_mlstm_m[j], TB=16)
            outs["C_p"].append(C_p); outs["n_p"].append(n_p); outs["m_p"].append(m_p.reshape(B, -1))
            outs["C_s"].append(C_s); outs["n_s"].append(n_s); outs["m_s"].append(m_s.reshape(BS, -1))
        else:
            w_in = _pad_cols(mla_w_in[j], Q_LORA + KV_LORA + LANES).astype(BF16)
            wq = mla_w_uq[j].reshape(Q_LORA, H, QK_NOPE + QK_ROPE)
            wq_nope = wq[:, :, :QK_NOPE].reshape(Q_LORA, H * QK_NOPE)
            wq_rope = jnp.pad(wq[:, :, QK_NOPE:], ((0, 0), (0, 0), (0, LANES - QK_ROPE))).reshape(Q_LORA, H * LANES)
            w_uq = jnp.concatenate([wq_nope, wq_rope], axis=1).astype(BF16)
            w_ukt = jnp.transpose(mla_w_uk[j].reshape(KV_LORA, H, QK_NOPE), (1, 2, 0)).astype(BF16)
            w_uv = jnp.transpose(mla_w_uv[j].reshape(KV_LORA, H, V_DIM), (1, 0, 2)).astype(BF16)
            w_a = mla_w_o[j].astype(BF16)
            g_q, g_kv = row(mla_g_q[j]), row(mla_g_kv[j])

            T_ATT = 256
            q_p, kcat_p, lat_p, kr_p, vt_p = _mla_project(hp, g_mix, w_in, g_q, g_kv, w_uq, w_ukt, *rope_p,
                                                          tm=TM, table_blocks=S // TM, vt_tile=T_ATT)
            a_p = _attention_prompt(q_p, kcat_p, vt_p, w_uv, B, S, T=T_ATT)
            q_s, _, lat_s, kr_s = _mla_project(hs, g_mix, w_in, g_q, g_kv, w_uq, w_ukt, *rope_s,
                                               tm=BS, table_blocks=1, vt_tile=0)
            cache_k_rope_t = jnp.swapaxes(cache_k_rope, 2, 3)
            o_s = _attention_sample(q_s.reshape(BS, H, Q_SLOT), lat_s, kr_s, cache_latent, cache_k_rope_t,
                                    page_table, layer=j, G=64)
            a_s = _value_up(o_s.reshape(BS, H * KV_LORA), w_uv)
            outs["lat_p"].append(lat_p.reshape(B, S, KV_LORA)); outs["kr_p"].append(kr_p.reshape(B, S, QK_ROPE))
            outs["lat_s"].append(lat_s.reshape(BS, T, KV_LORA)); outs["kr_s"].append(kr_s.reshape(BS, T, QK_ROPE))

        g_ffn = row(norm_ffn[layer])
        g_fin = row(norm_final)
        w_gu = ffn_w_gate_up[layer].astype(BF16)
        w_d = ffn_w_down[layer].astype(BF16)
        hp = _mixer_out_ffn(hp, a_p, w_a, g_ffn, w_gu, w_d, g_fin, TM, final_norm=last)
        hs = _mixer_out_ffn(hs, a_s, w_a, g_ffn, w_gu, w_d, g_fin, BS, final_norm=last)

    st = jnp.stack
    return (hp.reshape(B, S, D), hs.reshape(BS, T, D),
            st(outs["C_p"]), st(outs["n_p"]), st(outs["m_p"]),
            st(outs["C_s"]), st(outs["n_s"]), st(outs["m_s"]),
            st(outs["lat_p"]), st(outs["kr_p"]), st(outs["lat_s"]), st(outs["kr_s"]))
```

```python
import functools

import jax
import jax.numpy as jnp
from jax import lax
from jax.experimental import pallas as pl
from jax.experimental.pallas import tpu as pltpu

F32 = jnp.float32
BF16 = jnp.bfloat16
EPS = 1e-6
ROPE_THETA = 10000.0

V7X_VMEM_BYTES = 64 * 1024 * 1024
LANES = 128
VMEM_LIMIT_BYTES = V7X_VMEM_BYTES - 8 * 1024 * 1024

ML_HEADS = 8
MLA_HEADS = 8
Q_LORA = 384
KV_LORA = 256
QK_NOPE = 128
QK_ROPE = 64
V_DIM = 128
Q_SLOT = KV_LORA + LANES
MLA_SCALE = (QK_NOPE + QK_ROPE) ** -0.5

NT_DIMS = (((1,), (1,)), ((), ()))


def _params(n_grid_axes):
    return pltpu.CompilerParams(
        dimension_semantics=("arbitrary",) * n_grid_axes,
        vmem_limit_bytes=VMEM_LIMIT_BYTES,
    )


def _rms(x, g):
    return x * lax.rsqrt(jnp.mean(x * x, axis=-1, keepdims=True) + EPS) * g


def _log_sigmoid(x):
    return jnp.minimum(x, 0.0) - jnp.log1p(jnp.exp(-jnp.abs(x)))


def _resident(shape):
    nd = len(shape)
    return pl.BlockSpec(shape, lambda *_: (0,) * nd)


def _norm_matmul_kernel(x_ref, g_ref, w_ref, o_ref):
    xn = _rms(x_ref[...], g_ref[...]).astype(BF16)
    o_ref[...] = jnp.dot(xn, w_ref[...], preferred_element_type=F32)


def _norm_matmul(x, g, w, tm):
    M, D = x.shape
    N = w.shape[1]
    return pl.pallas_call(
        _norm_matmul_kernel,
        out_shape=jax.ShapeDtypeStruct((M, N), F32),
        grid=(M // tm,),
        in_specs=[pl.BlockSpec((tm, D), lambda i: (i, 0)), _resident((1, D)), _resident((D, N))],
        out_specs=pl.BlockSpec((tm, N), lambda i: (i, 0)),
        compiler_params=_params(1),
        name="norm_matmul",
    )(x, g, w)


def _mlstm_reset(caug_ref, mst_ref):
    caug_ref[...] = jnp.zeros_like(caug_ref)
    mst_ref[...] = jnp.zeros_like(mst_ref)


def _mlstm_chunk_phases(q_ref, k_ref, v_ref, o_ref, gt_ref, bias_ref, gh_ref, hg_ref, caug_ref, mst_ref,
                        *, L, H, DK, DV):
    gates = gt_ref[...] + bias_ref[...]
    lane = lax.broadcasted_iota(jnp.int32, gates.shape, 1)
    G = jnp.where(lane < H, gates, _log_sigmoid(gates))
    row = lax.broadcasted_iota(jnp.int32, (L, L), 0)
    col = lax.broadcasted_iota(jnp.int32, (L, L), 1)
    causal = col <= row
    cs = jnp.dot(causal.astype(F32), G, precision=lax.Precision.HIGHEST,
                 preferred_element_type=F32)
    GT = G.T
    csT = cs.T

    kT = (k_ref[...] * (DK ** -0.5)).T
    ones = jnp.ones((L, DV), F32)
    heads = range(H)


    b_col = [cs[:, H + h:H + h + 1] for h in heads]
    b_row = [csT[H + h:H + h + 1, :] for h in heads]
    i_row = [GT[h:h + 1, :] for h in heads]
    r_mat = [jnp.where(causal, i_row[h] - b_row[h], -jnp.inf) for h in heads]
    r_max = [jnp.max(r_mat[h], axis=1, keepdims=True) for h in heads]
    yield

    m_prev = [mst_ref[h:h + 1, 0:1] for h in heads]
    log_inter = [b_col[h] + m_prev[h] for h in heads]
    m_t = [jnp.maximum(log_inter[h], b_col[h] + r_max[h]) for h in heads]
    w_inter = [jnp.exp(log_inter[h] - m_t[h]) for h in heads]
    W = [jnp.exp(r_mat[h] + (b_col[h] - m_t[h])) for h in heads]

    lane_pair = lax.broadcasted_iota(jnp.int32, (L, 2 * DK), 1)
    S, inter, caug_prev = [None] * H, [None] * H, [None] * H
    for p in range(H // 2):
        q_pair = q_ref[:, p * 2 * DK:(p + 1) * 2 * DK]
        q_lo = jnp.where(lane_pair < DK, q_pair, 0.0)
        q_hi = jnp.where(lane_pair < DK, 0.0, q_pair)
        qm2 = jnp.concatenate([q_lo, q_hi], axis=0).astype(BF16)
        k_pair = (k_ref[:, p * 2 * DK:(p + 1) * 2 * DK] * (DK ** -0.5)).astype(BF16)
        s2 = lax.dot_general(qm2, k_pair, NT_DIMS, preferred_element_type=F32)
        caug_pair = caug_ref[p]
        i2 = jnp.dot(qm2, caug_pair.astype(BF16), preferred_element_type=F32)
        for half in range(2):
            h = 2 * p + half
            S[h] = s2[half * L:(half + 1) * L] * W[h]
            inter[h] = i2[half * L:(half + 1) * L]
            caug_prev[h] = caug_pair[half * DK:(half + 1) * DK, :]
    yield

    m_new = [m_t[h][L - 1:L, :] for h in heads]
    b_last = [cs[L - 1:L, H + h:H + h + 1] for h in heads]
    R = []
    for h in heads:
        w_s_row = jnp.exp(b_last[h] - b_row[h] + i_row[h] - m_new[h])
        kwT = (kT[h * DK:(h + 1) * DK, :] * w_s_row).astype(BF16)
        vaug = jnp.concatenate([v_ref[:, h * DV:(h + 1) * DV], ones], axis=1).astype(BF16)
        lhs = jnp.concatenate([S[h].astype(BF16), kwT], axis=0)
        R.append(jnp.dot(lhs, vaug, preferred_element_type=F32))
    yield

    hh = []
    for h in heads:
        numden = w_inter[h] * inter[h] + R[h][:L]
        num, den = numden[:, :DV], numden[:, DV:]
        hh.append(num / jnp.maximum(jnp.abs(den), jnp.exp(-m_t[h])))
    ms = [jnp.mean(hh[h] * hh[h], axis=-1, keepdims=True) for h in heads]
    for h in heads:
        hn = hh[h] * lax.rsqrt(ms[h] + EPS)
        og = jax.nn.sigmoid(o_ref[:, h * DV:(h + 1) * DV])
        hg_ref[:, h * DV:(h + 1) * DV] = (og * (hn * gh_ref[:, h * DV:(h + 1) * DV])).astype(BF16)

    for h in heads:
        p, r0 = h // 2, (h % 2) * DK
        w_c = jnp.exp(b_last[h] + m_prev[h] - m_new[h])
        caug_ref[p, r0:r0 + DK, :] = w_c * caug_prev[h] + R[h][L:]
        mst_ref[h:h + 1, :] = jnp.broadcast_to(m_new[h], (1, LANES))


def _mlstm_write_state(caug_ref, mst_ref, c_out_ref, n_out_ref, m_out_ref, *, H, DK, DV):
    pick0 = (lax.broadcasted_iota(jnp.int32, (8, DV), 1) == 0).astype(F32)
    for h in range(H):
        p, r0 = h // 2, (h % 2) * DK
        ca = caug_ref[p, r0:r0 + DK, :]
        c_out_ref[0, h] = ca[:, :DV]
        n_rows = lax.dot_general(pick0, ca[:, DV:], NT_DIMS, precision=lax.Precision.HIGHEST,
                                 preferred_element_type=F32)
        n_out_ref[0, h:h + 1, :] = n_rows[0:1, :]
        m_out_ref[0, :, h:h + 1] = mst_ref[h:h + 1, 0:1]


def _mlstm_step_kernel(q_ref, k_ref, v_ref, o_ref, gi_ref, gf_ref, bi_ref, bf_ref, gh_ref,
                       c0_ref, n0_ref, m0_ref,
                       hg_ref, c_out_ref, n_out_ref, m_out_ref, *, H, DK, DV, TB):
    RB = TB * H
    i_pre = gi_ref[...] + bi_ref[...]
    log_f = _log_sigmoid(gf_ref[...] + bf_ref[...])
    log_inter = log_f + m0_ref[...]
    m_t = jnp.maximum(log_inter, i_pre)
    w_inter = jnp.exp(log_inter - m_t)
    w_intra = jnp.exp(i_pre - m_t)

    q = q_ref[...]
    k = k_ref[...] * (DK ** -0.5)
    v = v_ref[...]
    n_prev = n0_ref[...]
    s = jnp.sum(q * k, axis=1, keepdims=True) * w_intra
    den = w_inter * jnp.sum(q * n_prev, axis=1, keepdims=True) + s

    def block_diag(x):
        wide = jnp.concatenate([x] * H, axis=1)
        head_of_lane = lax.broadcasted_iota(jnp.int32, wide.shape, 1) // DK
        head_of_row = lax.broadcasted_iota(jnp.int32, wide.shape, 0) % H
        return jnp.where(head_of_lane == head_of_row, wide, 0.0)

    q_bd = block_diag(q)
    kw_bd = block_diag(k * w_intra[:, :DK])
    qc = jnp.concatenate(
        [jnp.dot(q_bd[t * H:(t + 1) * H, :], c0_ref[t], preferred_element_type=F32) for t in range(TB)],
        axis=0)

    num = w_inter * qc + s * v
    hh = num / jnp.maximum(jnp.abs(den), jnp.exp(-m_t))
    hn = hh * lax.rsqrt(jnp.mean(hh * hh, axis=-1, keepdims=True) + EPS)
    hg_ref[...] = (jax.nn.sigmoid(o_ref[...]) * (hn * gh_ref[...])).astype(BF16)
    n_out_ref[...] = w_inter[:, :DK] * n_prev + w_intra[:, :DK] * k
    m_out_ref[...] = m_t

    for t in range(TB):
        rows = slice(t * H, (t + 1) * H)
        d_c = lax.dot_general(kw_bd[rows, :], v[rows, :], (((0,), (0,)), ((), ())),
                              preferred_element_type=F32)
        for h in range(H):
            blk = slice(h * DK, (h + 1) * DK)
            c_out_ref[t, blk, :] = w_inter[t * H + h:t * H + h + 1, :] * c0_ref[t, blk, :] + d_c[blk, :]


def _mlstm_sample(xw, b_gates, g_head, c0, n0, m0, TB):
    H, DK, DV = ML_HEADS, 64, 128
    B = xw.shape[0]
    R, RB = B * H, TB * H
    qk_w, v_w = H * DK, H * DV
    lanes = lambda x: jnp.broadcast_to(x.reshape(-1, 1), (x.size, LANES))
    per_block = lambda x: jnp.tile(x, (TB, 1))
    q = xw[:, :qk_w].reshape(R, DK)
    k = xw[:, qk_w:2 * qk_w].reshape(R, DK)
    v = xw[:, 2 * qk_w:2 * qk_w + v_w].reshape(R, DV)
    o = xw[:, 2 * qk_w + v_w:2 * qk_w + 2 * v_w].reshape(R, DV)
    g0 = 2 * qk_w + 2 * v_w
    operands = (
        q, k, v, o,
        lanes(xw[:, g0:g0 + H]), lanes(xw[:, g0 + H:g0 + 2 * H]),
        per_block(lanes(b_gates[:H])), per_block(lanes(b_gates[H:])),
        per_block(g_head.reshape(H, DV)),
        c0.reshape(B, H * DK, DV), n0.reshape(R, DK), lanes(m0),
    )
    row_blk = lambda n: pl.BlockSpec((RB, n), lambda b: (b, 0))
    state_blk = pl.BlockSpec((TB, H * DK, DV), lambda b: (b, 0, 0))
    kern = functools.partial(_mlstm_step_kernel, H=H, DK=DK, DV=DV, TB=TB)
    hg, c_new, n_new, m_new = pl.pallas_call(
        kern,
        out_shape=(
            jax.ShapeDtypeStruct((R, DV), BF16),
            jax.ShapeDtypeStruct((B, H * DK, DV), F32),
            jax.ShapeDtypeStruct((R, DK), F32),
            jax.ShapeDtypeStruct((R, LANES), F32),
        ),
        grid=(B // TB,),
        in_specs=[
            row_blk(DK), row_blk(DK), row_blk(DV), row_blk(DV),
            row_blk(LANES), row_blk(LANES),
            _resident((RB, LANES)), _resident((RB, LANES)), _resident((RB, DV)),
            state_blk, row_blk(DK), row_blk(LANES),
        ],
        out_specs=(row_blk(DV), state_blk, row_blk(DK), row_blk(LANES)),
        compiler_params=_params(1),
        name="mlstm_sample",
    )(*operands)
    return (hg.reshape(B, v_w), c_new.reshape(B, H, DK, DV), n_new.reshape(B, H, DK),
            m_new[:, 0].reshape(B, H))


def _ffn_kernel(h_ref, a_ref, wa_ref, gn_ref, wgu_ref, wd_ref, gf_ref, o_ref, *, d_ff, tf, final_norm, a_tile):
    if a_tile:
        proj = jnp.concatenate(
            [lax.dot_general(a_ref[t], wa_ref[...], (((0,), (0,)), ((), ())), preferred_element_type=F32)
             for t in range(a_ref.shape[0])], axis=0)
    else:
        proj = jnp.dot(a_ref[...], wa_ref[...], preferred_element_type=F32)
    h1 = h_ref[...] + proj
    xn = _rms(h1, gn_ref[...]).astype(BF16)
    acc = h1
    for c in range(d_ff // tf):
        g = jnp.dot(xn, wgu_ref[:, c * tf:(c + 1) * tf], preferred_element_type=F32)
        u = jnp.dot(xn, wgu_ref[:, d_ff + c * tf:d_ff + (c + 1) * tf], preferred_element_type=F32)
        act = (g * jax.nn.sigmoid(g) * u).astype(BF16)
        acc = acc + jnp.dot(act, wd_ref[c * tf:(c + 1) * tf, :], preferred_element_type=F32)
    if final_norm:
        acc = _rms(acc, gf_ref[...])
    o_ref[...] = acc


def _mixer_out_ffn(h, a, w_a, g_ffn, w_gu, w_d, g_final, tm, final_norm):
    M, D = h.shape
    KA = w_a.shape[0]
    d_ff = w_d.shape[0]
    a_tile = a.shape[2] if a.ndim == 3 else 0
    kern = functools.partial(_ffn_kernel, d_ff=d_ff, tf=256, final_norm=final_norm, a_tile=a_tile)
    single = pl.Buffered(1)
    a_spec = (pl.BlockSpec((tm // a_tile, KA, a_tile), lambda i: (i, 0, 0)) if a_tile
              else pl.BlockSpec((tm, KA), lambda i: (i, 0)))
    return pl.pallas_call(
        kern,
        out_shape=jax.ShapeDtypeStruct((M, D), F32),
        grid=(M // tm,),
        in_specs=[
            pl.BlockSpec((tm, D), lambda i: (i, 0)),
            a_spec,
            pl.BlockSpec((KA, D), lambda i: (0, 0), pipeline_mode=single),
            _resident((1, D)),
            pl.BlockSpec((D, 2 * d_ff), lambda i: (0, 0), pipeline_mode=single),
            pl.BlockSpec((d_ff, D), lambda i: (0, 0), pipeline_mode=single),
            _resident((1, D)),
        ],
        out_specs=pl.BlockSpec((tm, D), lambda i: (i, 0)),
        compiler_params=_params(1),
        name="mixer_out_ffn",
    )(h, a, w_a, g_ffn, w_gu, w_d, g_final)


def _rope_slot(x, cos, sin_lo, sin_hi):
    return x * cos + pltpu.roll(x, 96, 1) * sin_lo + pltpu.roll(x, 32, 1) * sin_hi


def _mla_proj_kernel(h_ref, gn_ref, win_ref, gq_ref, gkv_ref, wuq_ref, wukt_ref,
                     cos_ref, sinlo_ref, sinhi_ref,
                     q_ref, kcat_ref, lat_ref, kr_ref, *maybe_vt_ref, H, vt_tile):
    xn = _rms(h_ref[...], gn_ref[...]).astype(BF16)
    t = jnp.dot(xn, win_ref[...], preferred_element_type=F32)
    c_q = t[:, :Q_LORA]
    c_kv = t[:, Q_LORA:Q_LORA + KV_LORA]
    k_slot = t[:, Q_LORA + KV_LORA:]
    cos, sin_lo, sin_hi = cos_ref[...], sinlo_ref[...], sinhi_ref[...]

    lat = _rms(c_kv, gkv_ref[...])
    k_rot = _rope_slot(k_slot, cos, sin_lo, sin_hi)
    lat_ref[...] = lat
    kr_ref[...] = k_rot[:, :QK_ROPE]
    kcat_ref[:, :KV_LORA] = lat.astype(BF16)
    kcat_ref[:, KV_LORA:] = k_rot.astype(BF16)
    if vt_tile:
        (vt_ref,) = maybe_vt_ref
        for t in range(lat.shape[0] // vt_tile):
            vt_ref[t] = lat[t * vt_tile:(t + 1) * vt_tile, :].T.astype(BF16)

    cqn = _rms(c_q, gq_ref[...]).astype(BF16)
    q = jnp.dot(cqn, wuq_ref[...], preferred_element_type=F32)
    for h in range(H):
        q_nope = q[:, h * QK_NOPE:(h + 1) * QK_NOPE].astype(BF16)
        q_lat = jnp.dot(q_nope, wukt_ref[h], preferred_element_type=F32)
        q_rot = _rope_slot(q[:, (H + h) * LANES:(H + h + 1) * LANES], cos, sin_lo, sin_hi)
        if vt_tile:
            for t in range(q.shape[0] // vt_tile):
                rows, dst = slice(t * vt_tile, (t + 1) * vt_tile), slice(h * vt_tile, (h + 1) * vt_tile)
                q_ref[t, dst, :KV_LORA] = q_lat[rows].astype(BF16)
                q_ref[t, dst, KV_LORA:] = q_rot[rows].astype(BF16)
        else:
            q_ref[:, h * Q_SLOT:h * Q_SLOT + KV_LORA] = q_lat.astype(BF16)
            q_ref[:, h * Q_SLOT + KV_LORA:(h + 1) * Q_SLOT] = q_rot.astype(BF16)


def _mla_project(h, g_norm, w_in, g_q, g_kv, w_uq, w_ukt, cos, sin_lo, sin_hi, tm, table_blocks, vt_tile):
    M, D = h.shape
    H = MLA_HEADS
    kern = functools.partial(_mla_proj_kernel, H=H, vt_tile=vt_tile)
    table = pl.BlockSpec((tm, LANES), lambda i: (i % table_blocks, 0))
    out_shape = [
        jax.ShapeDtypeStruct((M, H * Q_SLOT), BF16),
        jax.ShapeDtypeStruct((M, Q_SLOT), BF16),
        jax.ShapeDtypeStruct((M, KV_LORA), F32),
        jax.ShapeDtypeStruct((M, QK_ROPE), F32),
    ]
    out_specs = [
        pl.BlockSpec((tm, H * Q_SLOT), lambda i: (i, 0)),
        pl.BlockSpec((tm, Q_SLOT), lambda i: (i, 0)),
        pl.BlockSpec((tm, KV_LORA), lambda i: (i, 0)),
        pl.BlockSpec((tm, QK_ROPE), lambda i: (i, 0)),
    ]
    if vt_tile:
        out_shape[0] = jax.ShapeDtypeStruct((M // vt_tile, H * vt_tile, Q_SLOT), BF16)
        out_specs[0] = pl.BlockSpec((tm // vt_tile, H * vt_tile, Q_SLOT), lambda i: (i, 0, 0))
        out_shape.append(jax.ShapeDtypeStruct((M // vt_tile, KV_LORA, vt_tile), BF16))
        out_specs.append(pl.BlockSpec((tm // vt_tile, KV_LORA, vt_tile), lambda i: (i, 0, 0)))
    return pl.pallas_call(
        kern,
        out_shape=tuple(out_shape),
        grid=(M // tm,),
        in_specs=[
            pl.BlockSpec((tm, D), lambda i: (i, 0)),
            _resident((1, D)),
            _resident(w_in.shape),
            _resident((1, Q_LORA)),
            _resident((1, KV_LORA)),
            _resident(w_uq.shape),
            _resident(w_ukt.shape),
            table, table, table,
        ],
        out_specs=tuple(out_specs),
        compiler_params=_params(1),
        name="mla_project",
    )(h, g_norm, w_in, g_q, g_kv, w_uq, w_ukt, cos, sin_lo, sin_hi)


def _attn_kernel(q_ref, k_ref, vt_ref, wuvt_ref, o_ref, m_ref, l_ref, acc_ref, sa_ref, sb_ref, *, H, T):
    i = pl.program_id(1)

    def scores(j):
        kj = k_ref[0, pl.ds(pl.multiple_of(j * T, T), T), :]
        return lax.dot_general(kj, q_ref[0], NT_DIMS, preferred_element_type=F32) * MLA_SCALE

    def update(j, st):
        vtj = vt_ref[j]
        m_prev = m_ref[...]
        m_new = jnp.maximum(m_prev, jnp.max(st, axis=0, keepdims=True))
        alpha = jnp.exp(m_prev - m_new)
        p = jnp.exp(st - m_new)
        l_ref[...] = alpha * l_ref[...] + jnp.sum(p, axis=0, keepdims=True)
        acc_ref[...] = alpha * acc_ref[...] + jnp.dot(vtj, p.astype(BF16), preferred_element_type=F32)
        m_ref[...] = m_new

    st = scores(i)
    key = lax.broadcasted_iota(jnp.int32, st.shape, 0)
    qry = lax.broadcasted_iota(jnp.int32, st.shape, 1) & (T - 1)
    st = jnp.where(key <= qry, st, -jnp.inf)
    m_first = jnp.max(st, axis=0, keepdims=True)
    p_first = jnp.exp(st - m_first)
    m_ref[...] = m_first
    l_ref[...] = jnp.sum(p_first, axis=0, keepdims=True)
    acc_ref[...] = jnp.dot(vt_ref[i], p_first.astype(BF16), preferred_element_type=F32)

    @pl.when(i > 0)
    def _():
        sa_ref[...] = scores(0)

    def pair(jj, carry):
        j = 2 * jj
        sb_ref[...] = scores(j + 1)
        update(j, sa_ref[...])
        sa_ref[...] = scores(jnp.minimum(j + 2, i - 1))
        update(j + 1, sb_ref[...])
        return carry

    lax.fori_loop(0, i // 2, pair, 0)

    @pl.when(i % 2 == 1)
    def _():
        update(i - 1, sa_ref[...])

    o_t = (acc_ref[...] / l_ref[...]).astype(BF16)
    for h in range(H):
        v_t = jnp.dot(wuvt_ref[h], o_t[:, h * T:(h + 1) * T], preferred_element_type=F32)
        o_ref[0, h * V_DIM:(h + 1) * V_DIM, :] = v_t.astype(BF16)


def _attention_prompt(q, kcat, vt, w_uvt, B, S, T):
    H = MLA_HEADS
    NQ = S // T
    kern = functools.partial(_attn_kernel, H=H, T=T)
    return pl.pallas_call(
        kern,
        out_shape=jax.ShapeDtypeStruct((B * NQ, H * V_DIM, T), BF16),
        grid=(B, NQ),
        in_specs=[
            pl.BlockSpec((1, H * T, Q_SLOT), lambda b, i: (b * NQ + i, 0, 0)),
            pl.BlockSpec((1, S, Q_SLOT), lambda b, i: (b, 0, 0)),
            pl.BlockSpec((NQ, KV_LORA, T), lambda b, i: (b, 0, 0)),
            _resident(w_uvt.shape),
        ],
        out_specs=pl.BlockSpec((1, H * V_DIM, T), lambda b, i: (b * NQ + i, 0, 0)),
        scratch_shapes=[
            pltpu.VMEM((1, H * T), F32),
            pltpu.VMEM((1, H * T), F32),
            pltpu.VMEM((KV_LORA, H * T), F32),
            pltpu.VMEM((T, H * T), F32),
            pltpu.VMEM((T, H * T), F32),
        ],
        compiler_params=_params(2),
        name="attention_prompt",
    )(q, kcat.reshape(B, S, Q_SLOT), vt, w_uvt)


def _decode_mlstm_kernel(pt_ref,
                         q_ref, cn_ref, krn_ref, lat_hbm, krt_hbm,
                         mq_ref, mk_ref, mv_ref, mo_ref, gt_ref, bias_ref, gh_ref,
                         o_ref, hg_ref, c_out_ref, n_out_ref, m_out_ref,
                         lat_buf, kr_buf, sem, caug_ref, mst_ref,
                         *, layer, G, P, NCH, RING, STREAMS, L, H_ML, DK, DV, NC):
    s = pl.program_id(0)
    ns = pl.num_programs(0)
    total = ns * NCH
    ml = dict(L=L, H=H_ML, DK=DK, DV=DV)

    def page_copies(c):
        slot = lax.rem(c, RING)
        cw = jnp.where(c >= total, c - total, c)
        bb, jj = lax.div(cw, NCH), lax.rem(cw, NCH)
        copies = []
        for g in range(G):
            page = pt_ref[bb, jj * G + g]
            copies.append(pltpu.make_async_copy(
                lat_hbm.at[layer, page], lat_buf.at[slot, pl.ds(g * P, P), :], sem.at[0, slot]))
            copies.append(pltpu.make_async_copy(
                krt_hbm.at[layer, page], kr_buf.at[slot, :, pl.ds(g * P, P)], sem.at[1, slot]))
        return copies

    def start(c):
        for cp in page_copies(c):
            cp.start()

    def wait(c):
        for cp in page_copies(c):
            cp.wait()

    @pl.when(s == 0)
    def _():
        for c in range(RING - 1):
            start(jnp.int32(c))

    ml_chunk = lax.rem(s, NC)

    @pl.when(ml_chunk == 0)
    def _():
        _mlstm_reset(caug_ref, mst_ref)

    q = q_ref[0].astype(F32)
    q_lat = q[:, :KV_LORA]
    q_rope = q[:, KV_LORA:KV_LORA + QK_ROPE]
    c_new = cn_ref[0]
    kr_new = krn_ref[0]

    s_new = (jnp.sum(q_lat * c_new, axis=1, keepdims=True)
             + jnp.sum(q_rope * kr_new, axis=1, keepdims=True)) * MLA_SCALE
    n_heads = q.shape[0]
    m_run = [s_new] + [jnp.full_like(s_new, -jnp.inf)] * (STREAMS - 1)
    l_run = [jnp.ones_like(s_new)] + [jnp.zeros_like(s_new)] * (STREAMS - 1)
    acc = ([jnp.broadcast_to(c_new, (n_heads, KV_LORA)).astype(F32)]
           + [jnp.zeros((n_heads, KV_LORA), F32)] * (STREAMS - 1))
    W = (G * P) // STREAMS

    ml_pieces = _mlstm_chunk_phases(mq_ref, mk_ref, mv_ref, mo_ref, gt_ref, bias_ref, gh_ref, hg_ref,
                                    caug_ref, mst_ref, **ml)
    assert NCH >= 2
    for j in range(NCH):
        c = s * NCH + j
        slot = lax.rem(c, RING)
        wait(c)
        kl = [lat_buf[slot, i * W:(i + 1) * W, :] for i in range(STREAMS)]
        sc = [(lax.dot_general(q_lat, kl[i], NT_DIMS, preferred_element_type=F32)
               + jnp.dot(q_rope, kr_buf[slot, :, i * W:(i + 1) * W], preferred_element_type=F32)) * MLA_SCALE
              for i in range(STREAMS)]
        if j in (0, NCH - 1):
            next(ml_pieces)
        for i in range(STREAMS):
            m_new = jnp.maximum(m_run[i], jnp.max(sc[i], axis=1, keepdims=True))
            alpha = jnp.exp(m_run[i] - m_new)
            p = jnp.exp(sc[i] - m_new)
            m_run[i] = m_new
            l_run[i] = alpha * l_run[i] + jnp.sum(p, axis=1, keepdims=True)
            acc[i] = alpha * acc[i] + jnp.dot(p, kl[i], preferred_element_type=F32)
        if j in (0, NCH - 1):
            next(ml_pieces, None)
        start(c + (RING - 1))

    m_all = functools.reduce(jnp.maximum, m_run)
    scale = [jnp.exp(m_i - m_all) for m_i in m_run]
    l_all = sum(l_i * w_i for l_i, w_i in zip(l_run, scale))
    acc_all = sum(a_i * w_i for a_i, w_i in zip(acc, scale))
    o_ref[0] = acc_all / l_all

    @pl.when(ml_chunk == NC - 1)
    def _():
        _mlstm_write_state(caug_ref, mst_ref, c_out_ref, n_out_ref, m_out_ref, H=H_ML, DK=DK, DV=DV)

    @pl.when(s == ns - 1)
    def _():
        for c in range(RING - 1):
            wait(total + c)


def _attention_sample_with_mlstm_prompt(q3, c_new, kr_new, cache_latent, cache_k_rope_t, page_table, layer, G,
                                        xw, bias, g_head, B, S, L):
    BS, H, _ = q3.shape
    n_pages = page_table.shape[1]
    P = cache_latent.shape[2]
    NCH = n_pages // G
    assert n_pages % G == 0
    RING = 3
    H_ML, DK, DV = ML_HEADS, 64, 128
    NC = S // L
    assert B * NC == BS, "one mLSTM chunk per decode sequence"
    qk_w, v_w = H_ML * DK, H_ML * DV
    gate_blk = (2 * qk_w + 2 * v_w) // LANES
    kern = functools.partial(_decode_mlstm_kernel, layer=layer, G=G, P=P, NCH=NCH, RING=RING, STREAMS=2,
                             L=L, H_ML=H_ML, DK=DK, DV=DV, NC=NC)
    grid_spec = pltpu.PrefetchScalarGridSpec(
        num_scalar_prefetch=1,
        grid=(BS,),
        in_specs=[
            pl.BlockSpec((1, H, Q_SLOT), lambda s, pt: (s, 0, 0)),
            pl.BlockSpec((1, 1, KV_LORA), lambda s, pt: (s, 0, 0)),
            pl.BlockSpec((1, 1, QK_ROPE), lambda s, pt: (s, 0, 0)),
            pl.BlockSpec(memory_space=pl.ANY),
            pl.BlockSpec(memory_space=pl.ANY),
            pl.BlockSpec((L, qk_w), lambda s, pt: (s, 0)),
            pl.BlockSpec((L, qk_w), lambda s, pt: (s, 1)),
            pl.BlockSpec((L, v_w), lambda s, pt: (s, 1)),
            pl.BlockSpec((L, v_w), lambda s, pt: (s, 2)),
            pl.BlockSpec((L, LANES), lambda s, pt: (s, gate_blk)),
            pl.BlockSpec((1, LANES), lambda s, pt: (0, 0)),
            pl.BlockSpec((1, v_w), lambda s, pt: (0, 0)),
        ],
        out_specs=(
            pl.BlockSpec((1, H, KV_LORA), lambda s, pt: (s, 0, 0)),
            pl.BlockSpec((L, v_w), lambda s, pt: (s, 0)),
            pl.BlockSpec((1, H_ML, DK, DV), lambda s, pt: (s // NC, 0, 0, 0)),
            pl.BlockSpec((1, H_ML, DK), lambda s, pt: (s // NC, 0, 0)),
            pl.BlockSpec((1, 1, H_ML), lambda s, pt: (s // NC, 0, 0)),
        ),
        scratch_shapes=[
            pltpu.VMEM((RING, G * P, KV_LORA), F32),
            pltpu.VMEM((RING, QK_ROPE, G * P), F32),
            pltpu.SemaphoreType.DMA((2, RING)),
            pltpu.VMEM((H_ML // 2, 2 * DK, 2 * DV), F32),
            pltpu.VMEM((H_ML, LANES), F32),
        ],
    )
    return pl.pallas_call(
        kern,
        out_shape=(
            jax.ShapeDtypeStruct((BS, H, KV_LORA), F32),
            jax.ShapeDtypeStruct((B * S, v_w), BF16),
            jax.ShapeDtypeStruct((B, H_ML, DK, DV), F32),
            jax.ShapeDtypeStruct((B, H_ML, DK), F32),
            jax.ShapeDtypeStruct((B, 1, H_ML), F32),
        ),
        grid_spec=grid_spec,
        compiler_params=_params(1),
        name="attention_sample_mlstm_prompt",
    )(page_table, q3, c_new.reshape(BS, 1, KV_LORA), kr_new.reshape(BS, 1, QK_ROPE),
      cache_latent, cache_k_rope_t, xw, xw, xw, xw, xw, bias, g_head)


def _value_up_kernel(o_ref, wuv_ref, v_ref, *, H):
    for h in range(H):
        o_h = o_ref[:, h * KV_LORA:(h + 1) * KV_LORA].astype(BF16)
        v_ref[:, h * V_DIM:(h + 1) * V_DIM] = jnp.dot(
            o_h, wuv_ref[h], preferred_element_type=F32).astype(BF16)


def _value_up(o_lat, w_uv):
    M = o_lat.shape[0]
    H = MLA_HEADS
    return pl.pallas_call(
        functools.partial(_value_up_kernel, H=H),
        out_shape=jax.ShapeDtypeStruct((M, H * V_DIM), BF16),
        grid=(1,),
        in_specs=[_resident(o_lat.shape), _resident(w_uv.shape)],
        out_specs=_resident((M, H * V_DIM)),
        compiler_params=_params(1),
        name="value_up",
    )(o_lat, w_uv)


def _rope_tables(pos):
    half = QK_ROPE // 2
    inv = ROPE_THETA ** (-jnp.arange(0, QK_ROPE, 2, dtype=F32) / QK_ROPE)
    ang = pos[:, None] * inv[None, :]
    cos, sin = jnp.cos(ang), jnp.sin(ang)
    z = jnp.zeros_like(cos)
    cos_t = jnp.concatenate([cos, cos, z, z], axis=1)
    sin_lo = jnp.concatenate([-sin, z, z, z], axis=1)
    sin_hi = jnp.concatenate([z, sin, z, z], axis=1)
    return cos_t, sin_lo, sin_hi


def _pad_cols(w, n):
    return jnp.pad(w, ((0, 0), (0, n - w.shape[1])))


def kernel(x_prompt, x_sample, state_mlstm_C, state_mlstm_n, state_mlstm_m, cache_latent, cache_k_rope,
           page_table, norm_mix, norm_ffn, norm_final, mlstm_w_in, mlstm_b_gates, mlstm_g_head, mlstm_w_out,
           mla_w_in, mla_g_q, mla_g_kv, mla_w_uq, mla_w_uk, mla_w_uv, mla_w_o, ffn_w_gate_up, ffn_w_down):
    B, S, D = x_prompt.shape
    BS, T, _ = x_sample.shape
    assert T == 1, "sample group is one new token per sequence"
    depth = norm_mix.shape[0]
    H = MLA_HEADS
    past_len = page_table.shape[1] * cache_latent.shape[2]

    hp = x_prompt.reshape(B * S, D)
    hs = x_sample.reshape(BS, D)
    TM = 512
    row = lambda v: v.reshape(1, -1).astype(F32)

    pos_p = jnp.arange(S, dtype=F32)
    pos_s = jnp.broadcast_to(jnp.arange(T, dtype=F32) + past_len, (BS,))
    rope_p = _rope_tables(pos_p)
    rope_s = _rope_tables(pos_s)

    assert depth == 2, "schedule below is written for one mLSTM layer followed by one MLA layer"
    g_fin = row(norm_final)

    n_gates = 2 * ML_HEADS
    ml_w_in = _pad_cols(mlstm_w_in[0], mlstm_w_in.shape[2] - n_gates + LANES).astype(BF16)
    ml_bias = _pad_cols(mlstm_b_gates[0].reshape(1, -1), LANES).astype(F32)
    ml_g_head = row(mlstm_g_head[0])
    ml_w_out = mlstm_w_out[0].astype(BF16)
    mla_in = _pad_cols(mla_w_in[0], Q_LORA + KV_LORA + LANES).astype(BF16)
    wq = mla_w_uq[0].reshape(Q_LORA, H, QK_NOPE + QK_ROPE)
    wq_nope = wq[:, :, :QK_NOPE].reshape(Q_LORA, H * QK_NOPE)
    wq_rope = jnp.pad(wq[:, :, QK_NOPE:], ((0, 0), (0, 0), (0, LANES - QK_ROPE))).reshape(Q_LORA, H * LANES)
    w_uq = jnp.concatenate([wq_nope, wq_rope], axis=1).astype(BF16)
    w_ukt = jnp.transpose(mla_w_uk[0].reshape(KV_LORA, H, QK_NOPE), (1, 2, 0)).astype(BF16)
    w_uv = jnp.transpose(mla_w_uv[0].reshape(KV_LORA, H, V_DIM), (1, 0, 2)).astype(BF16)
    w_uvt = jnp.transpose(mla_w_uv[0].reshape(KV_LORA, H, V_DIM), (1, 2, 0)).astype(BF16)
    w_o = mla_w_o[0].astype(BF16)
    g_q, g_kv = row(mla_g_q[0]), row(mla_g_kv[0])
    mla_weights = (mla_in, g_q, g_kv, w_uq, w_ukt)
    ffn = [(row(norm_ffn[l]), ffn_w_gate_up[l].astype(BF16), ffn_w_down[l].astype(BF16)) for l in range(depth)]
    g_mix = [row(norm_mix[l]) for l in range(depth)]
    cache_k_rope_t = jnp.swapaxes(cache_k_rope, 2, 3)

    xw_s = _norm_matmul(hs, g_mix[0], ml_w_in, BS)
    a_s, C_s, n_s, m_s = _mlstm_sample(xw_s, mlstm_b_gates[0].astype(F32), ml_g_head, state_mlstm_C[0],
                                       state_mlstm_n[0], state_mlstm_m[0], TB=16)
    hs = _mixer_out_ffn(hs, a_s, ml_w_out, *ffn[0], g_fin, BS, final_norm=False)
    q_s, _, lat_s, kr_s = _mla_project(hs, g_mix[1], *mla_weights, *rope_s, tm=BS, table_blocks=1, vt_tile=0)

    xw_p = _norm_matmul(hp, g_mix[0], ml_w_in, TM)
    o_s, a_p, C_p, n_p, m_p = _attention_sample_with_mlstm_prompt(
        q_s.reshape(BS, H, Q_SLOT), lat_s, kr_s, cache_latent, cache_k_rope_t, page_table, layer=0, G=64,
        xw=xw_p, bias=ml_bias, g_head=ml_g_head, B=B, S=S, L=256)
    hp = _mixer_out_ffn(hp, a_p, ml_w_out, *ffn[0], g_fin, TM, final_norm=False)

    T_ATT = 256
    q_p, kcat_p, lat_p, kr_p, vt_p = _mla_project(hp, g_mix[1], *mla_weights, *rope_p,
                                                  tm=TM, table_blocks=S // TM, vt_tile=T_ATT)
    a_p = _attention_prompt(q_p, kcat_p, vt_p, w_uvt, B, S, T=T_ATT)
    hp = _mixer_out_ffn(hp, a_p, w_o, *ffn[1], g_fin, TM, final_norm=True)

    a_s = _value_up(o_s.reshape(BS, H * KV_LORA), w_uv)
    hs = _mixer_out_ffn(hs, a_s, w_o, *ffn[1], g_fin, BS, final_norm=True)

    return (hp.reshape(B, S, D), hs.reshape(BS, T, D),
            C_p[None], n_p[None], m_p.reshape(1, B, -1),
            C_s[None], n_s[None], m_s.reshape(1, BS, -1),
            lat_p.reshape(1, B, S, KV_LORA), kr_p.reshape(1, B, S, QK_ROPE),
            lat_s.reshape(1, BS, T, KV_LORA), kr_s.reshape(1, BS, T, QK_ROPE))
```

```python
import functools

import jax
import jax.numpy as jnp
from jax import lax
from jax.experimental import pallas as pl
from jax.experimental.pallas import tpu as pltpu

F32 = jnp.float32
BF16 = jnp.bfloat16
EPS = 1e-6
ROPE_THETA = 10000.0

V7X_VMEM_BYTES = 64 * 1024 * 1024
LANES = 128
VMEM_LIMIT_BYTES = V7X_VMEM_BYTES - 8 * 1024 * 1024

ML_HEADS = 8
MLA_HEADS = 8
Q_LORA = 384
KV_LORA = 256
QK_NOPE = 128
QK_ROPE = 64
V_DIM = 128
Q_SLOT = KV_LORA + LANES
MLA_SCALE = (QK_NOPE + QK_ROPE) ** -0.5

NT_DIMS = (((1,), (1,)), ((), ()))


def _params(n_grid_axes):
    return pltpu.CompilerParams(
        dimension_semantics=("arbitrary",) * n_grid_axes,
        vmem_limit_bytes=VMEM_LIMIT_BYTES,
    )


def _rms(x, g):
    return x * lax.rsqrt(jnp.mean(x * x, axis=-1, keepdims=True) + EPS) * g


def _log_sigmoid(x):
    return jnp.minimum(x, 0.0) - jnp.log1p(jnp.exp(-jnp.abs(x)))


def _resident(shape):
    nd = len(shape)
    return pl.BlockSpec(shape, lambda *_: (0,) * nd)


def _norm_matmul_kernel(x_ref, g_ref, w_ref, o_ref):
    xn = _rms(x_ref[...], g_ref[...]).astype(BF16)
    o_ref[...] = jnp.dot(xn, w_ref[...], preferred_element_type=F32)


def _norm_matmul(x, g, w, tm):
    M, D = x.shape
    N = w.shape[1]
    return pl.pallas_call(
        _norm_matmul_kernel,
        out_shape=jax.ShapeDtypeStruct((M, N), F32),
        grid=(M // tm,),
        in_specs=[pl.BlockSpec((tm, D), lambda i: (i, 0)), _resident((1, D)), _resident((D, N))],
        out_specs=pl.BlockSpec((tm, N), lambda i: (i, 0)),
        compiler_params=_params(1),
        name="norm_matmul",
    )(x, g, w)


def _mlstm_reset(caug_ref, mst_ref):
    caug_ref[...] = jnp.zeros_like(caug_ref)
    mst_ref[...] = jnp.zeros_like(mst_ref)


def _mlstm_chunk_phases(q_ref, k_ref, v_ref, o_ref, gt_ref, bias_ref, gh_ref, hg_ref, caug_ref, mst_ref,
                        *, L, H, DK, DV):
    gates = gt_ref[...] + bias_ref[...]
    lane = lax.broadcasted_iota(jnp.int32, gates.shape, 1)
    G = jnp.where(lane < H, gates, _log_sigmoid(gates))
    row = lax.broadcasted_iota(jnp.int32, (L, L), 0)
    col = lax.broadcasted_iota(jnp.int32, (L, L), 1)
    causal = col <= row
    cs = jnp.dot(causal.astype(F32), G, precision=lax.Precision.HIGHEST,
                 preferred_element_type=F32)
    GT = G.T
    csT = cs.T

    kT = (k_ref[...] * (DK ** -0.5)).T
    ones = jnp.ones((L, DV), F32)
    heads = range(H)


    b_col = [cs[:, H + h:H + h + 1] for h in heads]
    b_row = [csT[H + h:H + h + 1, :] for h in heads]
    i_row = [GT[h:h + 1, :] for h in heads]
    r_mat = [jnp.where(causal, i_row[h] - b_row[h], -jnp.inf) for h in heads]
    r_max = [jnp.max(r_mat[h], axis=1, keepdims=True) for h in heads]
    yield

    m_prev = [mst_ref[h:h + 1, 0:1] for h in heads]
    log_inter = [b_col[h] + m_prev[h] for h in heads]
    m_t = [jnp.maximum(log_inter[h], b_col[h] + r_max[h]) for h in heads]
    w_inter = [jnp.exp(log_inter[h] - m_t[h]) for h in heads]
    W = [jnp.exp(r_mat[h] + (b_col[h] - m_t[h])) for h in heads]

    lane_pair = lax.broadcasted_iota(jnp.int32, (L, 2 * DK), 1)
    S, inter, caug_prev = [None] * H, [None] * H, [None] * H
    for p in range(H // 2):
        q_pair = q_ref[:, p * 2 * DK:(p + 1) * 2 * DK]
        q_lo = jnp.where(lane_pair < DK, q_pair, 0.0)
        q_hi = jnp.where(lane_pair < DK, 0.0, q_pair)
        qm2 = jnp.concatenate([q_lo, q_hi], axis=0).astype(BF16)
        k_pair = (k_ref[:, p * 2 * DK:(p + 1) * 2 * DK] * (DK ** -0.5)).astype(BF16)
        s2 = lax.dot_general(qm2, k_pair, NT_DIMS, preferred_element_type=F32)
        caug_pair = caug_ref[p]
        i2 = jnp.dot(qm2, caug_pair.astype(BF16), preferred_element_type=F32)
        for half in range(2):
            h = 2 * p + half
            S[h] = s2[half * L:(half + 1) * L] * W[h]
            inter[h] = i2[half * L:(half + 1) * L]
            caug_prev[h] = caug_pair[half * DK:(half + 1) * DK, :]
    yield

    m_new = [m_t[h][L - 1:L, :] for h in heads]
    b_last = [cs[L - 1:L, H + h:H + h + 1] for h in heads]
    R = []
    for h in heads:
        w_s_row = jnp.exp(b_last[h] - b_row[h] + i_row[h] - m_new[h])
        kwT = (kT[h * DK:(h + 1) * DK, :] * w_s_row).astype(BF16)
        vaug = jnp.concatenate([v_ref[:, h * DV:(h + 1) * DV], ones], axis=1).astype(BF16)
        lhs = jnp.concatenate([S[h].astype(BF16), kwT], axis=0)
        R.append(jnp.dot(lhs, vaug, preferred_element_type=F32))
    yield

    hh = []
    for h in heads:
        numden = w_inter[h] * inter[h] + R[h][:L]
        num, den = numden[:, :DV], numden[:, DV:]
        hh.append(num / jnp.maximum(jnp.abs(den), jnp.exp(-m_t[h])))
    ms = [jnp.mean(hh[h] * hh[h], axis=-1, keepdims=True) for h in heads]
    for h in heads:
        hn = hh[h] * lax.rsqrt(ms[h] + EPS)
        og = jax.nn.sigmoid(o_ref[:, h * DV:(h + 1) * DV])
        hg_ref[:, h * DV:(h + 1) * DV] = (og * (hn * gh_ref[:, h * DV:(h + 1) * DV])).astype(BF16)

    for h in heads:
        p, r0 = h // 2, (h % 2) * DK
        w_c = jnp.exp(b_last[h] + m_prev[h] - m_new[h])
        caug_ref[p, r0:r0 + DK, :] = w_c * caug_prev[h] + R[h][L:]
        mst_ref[h:h + 1, :] = jnp.broadcast_to(m_new[h], (1, LANES))


def _mlstm_write_state(caug_ref, mst_ref, c_out_ref, n_out_ref, m_out_ref, *, H, DK, DV):
    pick0 = (lax.broadcasted_iota(jnp.int32, (8, DV), 1) == 0).astype(F32)
    for h in range(H):
        p, r0 = h // 2, (h % 2) * DK
        ca = caug_ref[p, r0:r0 + DK, :]
        c_out_ref[0, h] = ca[:, :DV]
        n_rows = lax.dot_general(pick0, ca[:, DV:], NT_DIMS, precision=lax.Precision.HIGHEST,
                                 preferred_element_type=F32)
        n_out_ref[0, h:h + 1, :] = n_rows[0:1, :]
        m_out_ref[0, :, h:h + 1] = mst_ref[h:h + 1, 0:1]


def _mlstm_step_kernel(q_ref, k_ref, v_ref, o_ref, gi_ref, gf_ref, bi_ref, bf_ref, gh_ref,
                       c0_ref, n0_ref, m0_ref,
                       hg_ref, c_out_ref, n_out_ref, m_out_ref, *, H, DK, DV, TB):
    RB = TB * H
    i_pre = gi_ref[...] + bi_ref[...]
    log_f = _log_sigmoid(gf_ref[...] + bf_ref[...])
    log_inter = log_f + m0_ref[...]
    m_t = jnp.maximum(log_inter, i_pre)
    w_inter = jnp.exp(log_inter - m_t)
    w_intra = jnp.exp(i_pre - m_t)

    q = q_ref[...]
    k = k_ref[...] * (DK ** -0.5)
    v = v_ref[...]
    n_prev = n0_ref[...]
    s = jnp.sum(q * k, axis=1, keepdims=True) * w_intra
    den = w_inter * jnp.sum(q * n_prev, axis=1, keepdims=True) + s

    def block_diag(x):
        wide = jnp.concatenate([x] * H, axis=1)
        head_of_lane = lax.broadcasted_iota(jnp.int32, wide.shape, 1) // DK
        head_of_row = lax.broadcasted_iota(jnp.int32, wide.shape, 0) % H
        return jnp.where(head_of_lane == head_of_row, wide, 0.0)

    q_bd = block_diag(q)
    kw_bd = block_diag(k * w_intra[:, :DK])
    qc = jnp.concatenate(
        [jnp.dot(q_bd[t * H:(t + 1) * H, :], c0_ref[t], preferred_element_type=F32) for t in range(TB)],
        axis=0)

    num = w_inter * qc + s * v
    hh = num / jnp.maximum(jnp.abs(den), jnp.exp(-m_t))
    hn = hh * lax.rsqrt(jnp.mean(hh * hh, axis=-1, keepdims=True) + EPS)
    hg_ref[...] = (jax.nn.sigmoid(o_ref[...]) * (hn * gh_ref[...])).astype(BF16)
    n_out_ref[...] = w_inter[:, :DK] * n_prev + w_intra[:, :DK] * k
    m_out_ref[...] = m_t

    for t in range(TB):
        rows = slice(t * H, (t + 1) * H)
        d_c = lax.dot_general(kw_bd[rows, :], v[rows, :], (((0,), (0,)), ((), ())),
                              preferred_element_type=F32)
        for h in range(H):
            blk = slice(h * DK, (h + 1) * DK)
            c_out_ref[t, blk, :] = w_inter[t * H + h:t * H + h + 1, :] * c0_ref[t, blk, :] + d_c[blk, :]


def _mlstm_sample(xw, b_gates, g_head, c0, n0, m0, TB):
    H, DK, DV = ML_HEADS, 64, 128
    B = xw.shape[0]
    R, RB = B * H, TB * H
    qk_w, v_w = H * DK, H * DV
    lanes = lambda x: jnp.broadcast_to(x.reshape(-1, 1), (x.size, LANES))
    per_block = lambda x: jnp.tile(x, (TB, 1))
    q = xw[:, :qk_w].reshape(R, DK)
    k = xw[:, qk_w:2 * qk_w].reshape(R, DK)
    v = xw[:, 2 * qk_w:2 * qk_w + v_w].reshape(R, DV)
    o = xw[:, 2 * qk_w + v_w:2 * qk_w + 2 * v_w].reshape(R, DV)
    g0 = 2 * qk_w + 2 * v_w
    operands = (
        q, k, v, o,
        lanes(xw[:, g0:g0 + H]), lanes(xw[:, g0 + H:g0 + 2 * H]),
        per_block(lanes(b_gates[:H])), per_block(lanes(b_gates[H:])),
        per_block(g_head.reshape(H, DV)),
        c0.reshape(B, H * DK, DV), n0.reshape(R, DK), lanes(m0),
    )
    row_blk = lambda n: pl.BlockSpec((RB, n), lambda b: (b, 0))
    state_blk = pl.BlockSpec((TB, H * DK, DV), lambda b: (b, 0, 0))
    kern = functools.partial(_mlstm_step_kernel, H=H, DK=DK, DV=DV, TB=TB)
    hg, c_new, n_new, m_new = pl.pallas_call(
        kern,
        out_shape=(
            jax.ShapeDtypeStruct((R, DV), BF16),
            jax.ShapeDtypeStruct((B, H * DK, DV), F32),
            jax.ShapeDtypeStruct((R, DK), F32),
            jax.ShapeDtypeStruct((R, LANES), F32),
        ),
        grid=(B // TB,),
        in_specs=[
            row_blk(DK), row_blk(DK), row_blk(DV), row_blk(DV),
            row_blk(LANES), row_blk(LANES),
            _resident((RB, LANES)), _resident((RB, LANES)), _resident((RB, DV)),
            state_blk, row_blk(DK), row_blk(LANES),
        ],
        out_specs=(row_blk(DV), state_blk, row_blk(DK), row_blk(LANES)),
        compiler_params=_params(1),
        name="mlstm_sample",
    )(*operands)
    return (hg.reshape(B, v_w), c_new.reshape(B, H, DK, DV), n_new.reshape(B, H, DK),
            m_new[:, 0].reshape(B, H))


def _ffn_kernel(h_ref, a_ref, wa_ref, gn_ref, wgu_ref, wd_ref, gf_ref, o_ref, *, d_ff, tf, final_norm, a_tile):
    if a_tile:
        proj = jnp.concatenate(
            [lax.dot_general(a_ref[t], wa_ref[...], (((0,), (0,)), ((), ())), preferred_element_type=F32)
             for t in range(a_ref.shape[0])], axis=0)
    else:
        proj = jnp.dot(a_ref[...], wa_ref[...], preferred_element_type=F32)
    h1 = h_ref[...] + proj
    xn = _rms(h1, gn_ref[...]).astype(BF16)
    acc = h1
    for c in range(d_ff // tf):
        g = jnp.dot(xn, wgu_ref[:, c * tf:(c + 1) * tf], preferred_element_type=F32)
        u = jnp.dot(xn, wgu_ref[:, d_ff + c * tf:d_ff + (c + 1) * tf], preferred_element_type=F32)
        act = (g * jax.nn.sigmoid(g) * u).astype(BF16)
        acc = acc + jnp.dot(act, wd_ref[c * tf:(c + 1) * tf, :], preferred_element_type=F32)
    if final_norm:
        acc = _rms(acc, gf_ref[...])
    o_ref[...] = acc


def _mixer_out_ffn(h, a, w_a, g_ffn, w_gu, w_d, layer, g_final, tm, final_norm):
    M, D = h.shape
    KA = w_a.shape[0]
    d_ff = w_d.shape[1]
    a_tile = a.shape[2] if a.ndim == 3 else 0
    kern = functools.partial(_ffn_kernel, d_ff=d_ff, tf=256, final_norm=final_norm, a_tile=a_tile)
    single = pl.Buffered(1)
    a_spec = (pl.BlockSpec((tm // a_tile, KA, a_tile), lambda i: (i, 0, 0)) if a_tile
              else pl.BlockSpec((tm, KA), lambda i: (i, 0)))
    return pl.pallas_call(
        kern,
        out_shape=jax.ShapeDtypeStruct((M, D), F32),
        grid=(M // tm,),
        in_specs=[
            pl.BlockSpec((tm, D), lambda i: (i, 0)),
            a_spec,
            pl.BlockSpec((KA, D), lambda i: (0, 0), pipeline_mode=single),
            _resident((1, D)),
            pl.BlockSpec((None, D, 2 * d_ff), lambda i: (layer, 0, 0), pipeline_mode=single),
            pl.BlockSpec((None, d_ff, D), lambda i: (layer, 0, 0), pipeline_mode=single),
            _resident((1, D)),
        ],
        out_specs=pl.BlockSpec((tm, D), lambda i: (i, 0)),
        compiler_params=_params(1),
        name="mixer_out_ffn",
    )(h, a, w_a, g_ffn, w_gu, w_d, g_final)


def _rope_slot(x, cos, sin_lo, sin_hi):
    return x * cos + pltpu.roll(x, 96, 1) * sin_lo + pltpu.roll(x, 32, 1) * sin_hi


def _mla_proj_kernel(h_ref, gn_ref, win_ref, gq_ref, gkv_ref, wuq_ref, wukt_ref,
                     cos_ref, sinlo_ref, sinhi_ref,
                     q_ref, kcat_ref, lat_ref, kr_ref, *maybe_vt_ref, H, vt_tile):
    xn = _rms(h_ref[...], gn_ref[...]).astype(BF16)
    t = jnp.dot(xn, win_ref[...], preferred_element_type=F32)
    c_q = t[:, :Q_LORA]
    c_kv = t[:, Q_LORA:Q_LORA + KV_LORA]
    k_slot = t[:, Q_LORA + KV_LORA:]
    cos, sin_lo, sin_hi = cos_ref[...], sinlo_ref[...], sinhi_ref[...]

    lat = _rms(c_kv, gkv_ref[...])
    k_rot = _rope_slot(k_slot, cos, sin_lo, sin_hi)
    lat_ref[...] = lat
    kr_ref[...] = k_rot[:, :QK_ROPE]
    kcat_ref[:, :KV_LORA] = lat.astype(BF16)
    kcat_ref[:, KV_LORA:] = k_rot.astype(BF16)
    if vt_tile:
        (vt_ref,) = maybe_vt_ref
        for t in range(lat.shape[0] // vt_tile):
            vt_ref[t] = lat[t * vt_tile:(t + 1) * vt_tile, :].T.astype(BF16)

    cqn = _rms(c_q, gq_ref[...]).astype(BF16)
    tm = cqn.shape[0]

    def store_q(h, lanes, val):
        if vt_tile:
            for t in range(tm // vt_tile):
                q_ref[t, h * vt_tile:(h + 1) * vt_tile, lanes] = val[t * vt_tile:(t + 1) * vt_tile].astype(BF16)
        else:
            q_ref[:, slice(h * Q_SLOT + lanes.start, h * Q_SLOT + lanes.stop)] = val.astype(BF16)

    q_rope = jnp.dot(cqn, wuq_ref[:, H * QK_NOPE:], preferred_element_type=F32)
    for h in range(H):
        store_q(h, slice(KV_LORA, Q_SLOT), _rope_slot(q_rope[:, h * LANES:(h + 1) * LANES], cos, sin_lo, sin_hi))
    q_nope = jnp.dot(cqn, wuq_ref[:, :H * QK_NOPE], preferred_element_type=F32)
    for h in range(H):
        q_lat = jnp.dot(q_nope[:, h * QK_NOPE:(h + 1) * QK_NOPE].astype(BF16), wukt_ref[h],
                        preferred_element_type=F32)
        store_q(h, slice(0, KV_LORA), q_lat)


def _mla_project(h, g_norm, w_in, g_q, g_kv, w_uq, w_ukt, cos, sin_lo, sin_hi, tm, table_blocks, vt_tile):
    M, D = h.shape
    H = MLA_HEADS
    kern = functools.partial(_mla_proj_kernel, H=H, vt_tile=vt_tile)
    table = pl.BlockSpec((tm, LANES), lambda i: (i % table_blocks, 0))
    out_shape = [
        jax.ShapeDtypeStruct((M, H * Q_SLOT), BF16),
        jax.ShapeDtypeStruct((M, Q_SLOT), BF16),
        jax.ShapeDtypeStruct((M, KV_LORA), F32),
        jax.ShapeDtypeStruct((M, QK_ROPE), F32),
    ]
    out_specs = [
        pl.BlockSpec((tm, H * Q_SLOT), lambda i: (i, 0)),
        pl.BlockSpec((tm, Q_SLOT), lambda i: (i, 0)),
        pl.BlockSpec((tm, KV_LORA), lambda i: (i, 0)),
        pl.BlockSpec((tm, QK_ROPE), lambda i: (i, 0)),
    ]
    if vt_tile:
        out_shape[0] = jax.ShapeDtypeStruct((M // vt_tile, H * vt_tile, Q_SLOT), BF16)
        out_specs[0] = pl.BlockSpec((tm // vt_tile, H * vt_tile, Q_SLOT), lambda i: (i, 0, 0))
        out_shape.append(jax.ShapeDtypeStruct((M // vt_tile, KV_LORA, vt_tile), BF16))
        out_specs.append(pl.BlockSpec((tm // vt_tile, KV_LORA, vt_tile), lambda i: (i, 0, 0)))
    return pl.pallas_call(
        kern,
        out_shape=tuple(out_shape),
        grid=(M // tm,),
        in_specs=[
            pl.BlockSpec((tm, D), lambda i: (i, 0)),
            _resident((1, D)),
            _resident(w_in.shape),
            _resident((1, Q_LORA)),
            _resident((1, KV_LORA)),
            _resident(w_uq.shape),
            _resident(w_ukt.shape),
            table, table, table,
        ],
        out_specs=tuple(out_specs),
        compiler_params=_params(1),
        name="mla_project",
    )(h, g_norm, w_in, g_q, g_kv, w_uq, w_ukt, cos, sin_lo, sin_hi)


def _attn_kernel(q_ref, k_ref, vt_ref, wuvt_ref, o_ref, m_ref, l_ref, acc_ref, sa_ref, sb_ref, *, H, T):
    i = pl.program_id(1)

    def scores(j):
        kj = k_ref[0, pl.ds(pl.multiple_of(j * T, T), T), :]
        return lax.dot_general(kj, q_ref[0], NT_DIMS, preferred_element_type=F32) * MLA_SCALE

    def update(j, st):
        vtj = vt_ref[j]
        m_prev = m_ref[...]
        m_new = jnp.maximum(m_prev, jnp.max(st, axis=0, keepdims=True))
        alpha = jnp.exp(m_prev - m_new)
        p = jnp.exp(st - m_new)
        l_ref[...] = alpha * l_ref[...] + jnp.sum(p, axis=0, keepdims=True)
        acc_ref[...] = alpha * acc_ref[...] + jnp.dot(vtj, p.astype(BF16), preferred_element_type=F32)
        m_ref[...] = m_new

    st = scores(i)
    key = lax.broadcasted_iota(jnp.int32, st.shape, 0)
    qry = lax.broadcasted_iota(jnp.int32, st.shape, 1) & (T - 1)
    st = jnp.where(key <= qry, st, -jnp.inf)
    m_first = jnp.max(st, axis=0, keepdims=True)
    p_first = jnp.exp(st - m_first)
    m_ref[...] = m_first
    l_ref[...] = jnp.sum(p_first, axis=0, keepdims=True)
    acc_ref[...] = jnp.dot(vt_ref[i], p_first.astype(BF16), preferred_element_type=F32)

    @pl.when(i > 0)
    def _():
        sa_ref[...] = scores(0)

    def pair(jj, carry):
        j = 2 * jj
        sb_ref[...] = scores(j + 1)
        update(j, sa_ref[...])
        sa_ref[...] = scores(jnp.minimum(j + 2, i - 1))
        update(j + 1, sb_ref[...])
        return carry

    lax.fori_loop(0, i // 2, pair, 0)

    @pl.when(i % 2 == 1)
    def _():
        update(i - 1, sa_ref[...])

    o_t = (acc_ref[...] / l_ref[...]).astype(BF16)
    for h in range(H):
        v_t = jnp.dot(wuvt_ref[h], o_t[:, h * T:(h + 1) * T], preferred_element_type=F32)
        o_ref[0, h * V_DIM:(h + 1) * V_DIM, :] = v_t.astype(BF16)


def _attention_prompt(q, kcat, vt, w_uvt, B, S, T):
    H = MLA_HEADS
    NQ = S // T
    kern = functools.partial(_attn_kernel, H=H, T=T)
    return pl.pallas_call(
        kern,
        out_shape=jax.ShapeDtypeStruct((B * NQ, H * V_DIM, T), BF16),
        grid=(B, NQ),
        in_specs=[
            pl.BlockSpec((1, H * T, Q_SLOT), lambda b, i: (b * NQ + i, 0, 0)),
            pl.BlockSpec((1, S, Q_SLOT), lambda b, i: (b, 0, 0)),
            pl.BlockSpec((NQ, KV_LORA, T), lambda b, i: (b, 0, 0)),
            _resident(w_uvt.shape),
        ],
        out_specs=pl.BlockSpec((1, H * V_DIM, T), lambda b, i: (b * NQ + i, 0, 0)),
        scratch_shapes=[
            pltpu.VMEM((1, H * T), F32),
            pltpu.VMEM((1, H * T), F32),
            pltpu.VMEM((KV_LORA, H * T), F32),
            pltpu.VMEM((T, H * T), F32),
            pltpu.VMEM((T, H * T), F32),
        ],
        compiler_params=_params(2),
        name="attention_prompt",
    )(q, kcat.reshape(B, S, Q_SLOT), vt, w_uvt)


def _decode_mlstm_kernel(pt_ref,
                         q_ref, cn_ref, krn_ref, lat_hbm, krt_hbm,
                         mq_ref, mk_ref, mv_ref, mo_ref, gt_ref, bias_ref, gh_ref,
                         o_ref, hg_ref, c_out_ref, n_out_ref, m_out_ref,
                         lat_buf, kr_buf, sem, caug_ref, mst_ref,
                         *, layer, G, P, NCH, RING, STREAMS, L, H_ML, DK, DV, NC):
    s = pl.program_id(0)
    ns = pl.num_programs(0)
    total = ns * NCH
    ml = dict(L=L, H=H_ML, DK=DK, DV=DV)

    def page_copies(c):
        slot = lax.rem(c, RING)
        cw = jnp.where(c >= total, c - total, c)
        bb, jj = lax.div(cw, NCH), lax.rem(cw, NCH)
        copies = []
        for g in range(G):
            page = pt_ref[bb, jj * G + g]
            copies.append(pltpu.make_async_copy(
                lat_hbm.at[layer, page], lat_buf.at[slot, pl.ds(g * P, P), :], sem.at[0, slot]))
            copies.append(pltpu.make_async_copy(
                krt_hbm.at[layer, page], kr_buf.at[slot, :, pl.ds(g * P, P)], sem.at[1, slot]))
        return copies

    def start(c):
        for cp in page_copies(c):
            cp.start()

    def wait(c):
        for cp in page_copies(c):
            cp.wait()

    @pl.when(s == 0)
    def _():
        for c in range(RING - 1):
            start(jnp.int32(c))

    ml_chunk = lax.rem(s, NC)

    @pl.when(ml_chunk == 0)
    def _():
        _mlstm_reset(caug_ref, mst_ref)

    q = q_ref[0].astype(F32)
    q_lat = q[:, :KV_LORA]
    q_rope = q[:, KV_LORA:KV_LORA + QK_ROPE]
    c_new = cn_ref[0]
    kr_new = krn_ref[0]

    s_new = (jnp.sum(q_lat * c_new, axis=1, keepdims=True)
             + jnp.sum(q_rope * kr_new, axis=1, keepdims=True)) * MLA_SCALE
    n_heads = q.shape[0]
    m_run = [s_new] + [jnp.full_like(s_new, -jnp.inf)] * (STREAMS - 1)
    l_run = [jnp.ones_like(s_new)] + [jnp.zeros_like(s_new)] * (STREAMS - 1)
    acc = ([jnp.broadcast_to(c_new, (n_heads, KV_LORA)).astype(F32)]
           + [jnp.zeros((n_heads, KV_LORA), F32)] * (STREAMS - 1))
    W = (G * P) // STREAMS

    ml_pieces = _mlstm_chunk_phases(mq_ref, mk_ref, mv_ref, mo_ref, gt_ref, bias_ref, gh_ref, hg_ref,
                                    caug_ref, mst_ref, **ml)
    assert NCH >= 2
    for j in range(NCH):
        c = s * NCH + j
        slot = lax.rem(c, RING)
        wait(c)
        kl = [lat_buf[slot, i * W:(i + 1) * W, :] for i in range(STREAMS)]
        sc = [(lax.dot_general(q_lat, kl[i], NT_DIMS, preferred_element_type=F32)
               + jnp.dot(q_rope, kr_buf[slot, :, i * W:(i + 1) * W], preferred_element_type=F32)) * MLA_SCALE
              for i in range(STREAMS)]
        if j in (0, NCH - 1):
            next(ml_pieces)
        for i in range(STREAMS):
            m_new = jnp.maximum(m_run[i], jnp.max(sc[i], axis=1, keepdims=True))
            alpha = jnp.exp(m_run[i] - m_new)
            p = jnp.exp(sc[i] - m_new)
            m_run[i] = m_new
            l_run[i] = alpha * l_run[i] + jnp.sum(p, axis=1, keepdims=True)
            acc[i] = alpha * acc[i] + jnp.dot(p, kl[i], preferred_element_type=F32)
        if j in (0, NCH - 1):
            next(ml_pieces, None)
        start(c + (RING - 1))

    m_all = functools.reduce(jnp.maximum, m_run)
    scale = [jnp.exp(m_i - m_all) for m_i in m_run]
    l_all = sum(l_i * w_i for l_i, w_i in zip(l_run, scale))
    acc_all = sum(a_i * w_i for a_i, w_i in zip(acc, scale))
    o_ref[0] = acc_all / l_all

    @pl.when(ml_chunk == NC - 1)
    def _():
        _mlstm_write_state(caug_ref, mst_ref, c_out_ref, n_out_ref, m_out_ref, H=H_ML, DK=DK, DV=DV)

    @pl.when(s == ns - 1)
    def _():
        for c in range(RING - 1):
            wait(total + c)


def _attention_sample_with_mlstm_prompt(q3, c_new, kr_new, cache_latent, cache_k_rope_t, page_table, layer, G,
                                        xw, bias, g_head, B, S, L):
    BS, H, _ = q3.shape
    n_pages = page_table.shape[1]
    P = cache_latent.shape[2]
    NCH = n_pages // G
    assert n_pages % G == 0
    RING = 3
    H_ML, DK, DV = ML_HEADS, 64, 128
    NC = S // L
    assert B * NC == BS, "one mLSTM chunk per decode sequence"
    qk_w, v_w = H_ML * DK, H_ML * DV
    gate_blk = (2 * qk_w + 2 * v_w) // LANES
    kern = functools.partial(_decode_mlstm_kernel, layer=layer, G=G, P=P, NCH=NCH, RING=RING, STREAMS=2,
                             L=L, H_ML=H_ML, DK=DK, DV=DV, NC=NC)
    grid_spec = pltpu.PrefetchScalarGridSpec(
        num_scalar_prefetch=1,
        grid=(BS,),
        in_specs=[
            pl.BlockSpec((1, H, Q_SLOT), lambda s, pt: (s, 0, 0)),
            pl.BlockSpec((1, 1, KV_LORA), lambda s, pt: (s, 0, 0)),
            pl.BlockSpec((1, 1, QK_ROPE), lambda s, pt: (s, 0, 0)),
            pl.BlockSpec(memory_space=pl.ANY),
            pl.BlockSpec(memory_space=pl.ANY),
            pl.BlockSpec((L, qk_w), lambda s, pt: (s, 0)),
            pl.BlockSpec((L, qk_w), lambda s, pt: (s, 1)),
            pl.BlockSpec((L, v_w), lambda s, pt: (s, 1)),
            pl.BlockSpec((L, v_w), lambda s, pt: (s, 2)),
            pl.BlockSpec((L, LANES), lambda s, pt: (s, gate_blk)),
            pl.BlockSpec((1, LANES), lambda s, pt: (0, 0)),
            pl.BlockSpec((1, v_w), lambda s, pt: (0, 0)),
        ],
        out_specs=(
            pl.BlockSpec((1, H, KV_LORA), lambda s, pt: (s, 0, 0)),
            pl.BlockSpec((L, v_w), lambda s, pt: (s, 0)),
            pl.BlockSpec((1, H_ML, DK, DV), lambda s, pt: (s // NC, 0, 0, 0)),
            pl.BlockSpec((1, H_ML, DK), lambda s, pt: (s // NC, 0, 0)),
            pl.BlockSpec((1, 1, H_ML), lambda s, pt: (s // NC, 0, 0)),
        ),
        scratch_shapes=[
            pltpu.VMEM((RING, G * P, KV_LORA), F32),
            pltpu.VMEM((RING, QK_ROPE, G * P), F32),
            pltpu.SemaphoreType.DMA((2, RING)),
            pltpu.VMEM((H_ML // 2, 2 * DK, 2 * DV), F32),
            pltpu.VMEM((H_ML, LANES), F32),
        ],
    )
    return pl.pallas_call(
        kern,
        out_shape=(
            jax.ShapeDtypeStruct((BS, H, KV_LORA), F32),
            jax.ShapeDtypeStruct((B * S, v_w), BF16),
            jax.ShapeDtypeStruct((B, H_ML, DK, DV), F32),
            jax.ShapeDtypeStruct((B, H_ML, DK), F32),
            jax.ShapeDtypeStruct((B, 1, H_ML), F32),
        ),
        grid_spec=grid_spec,
        compiler_params=_params(1),
        name="attention_sample_mlstm_prompt",
    )(page_table, q3, c_new.reshape(BS, 1, KV_LORA), kr_new.reshape(BS, 1, QK_ROPE),
      cache_latent, cache_k_rope_t, xw, xw, xw, xw, xw, bias, g_head)


def _value_up_kernel(o_ref, wuv_ref, v_ref, *, H):
    for h in range(H):
        o_h = o_ref[:, h * KV_LORA:(h + 1) * KV_LORA].astype(BF16)
        v_ref[:, h * V_DIM:(h + 1) * V_DIM] = jnp.dot(
            o_h, wuv_ref[h], preferred_element_type=F32).astype(BF16)


def _value_up(o_lat, w_uv):
    M = o_lat.shape[0]
    H = MLA_HEADS
    return pl.pallas_call(
        functools.partial(_value_up_kernel, H=H),
        out_shape=jax.ShapeDtypeStruct((M, H * V_DIM), BF16),
        grid=(1,),
        in_specs=[_resident(o_lat.shape), _resident(w_uv.shape)],
        out_specs=_resident((M, H * V_DIM)),
        compiler_params=_params(1),
        name="value_up",
    )(o_lat, w_uv)


def _rope_tables(pos):
    half = QK_ROPE // 2
    inv = ROPE_THETA ** (-jnp.arange(0, QK_ROPE, 2, dtype=F32) / QK_ROPE)
    ang = pos[:, None] * inv[None, :]
    cos, sin = jnp.cos(ang), jnp.sin(ang)
    z = jnp.zeros_like(cos)
    cos_t = jnp.concatenate([cos, cos, z, z], axis=1)
    sin_lo = jnp.concatenate([-sin, z, z, z], axis=1)
    sin_hi = jnp.concatenate([z, sin, z, z], axis=1)
    return cos_t, sin_lo, sin_hi


def _pad_cols(w, n):
    return jnp.pad(w, ((0, 0), (0, n - w.shape[1])))


def kernel(x_prompt, x_sample, state_mlstm_C, state_mlstm_n, state_mlstm_m, cache_latent, cache_k_rope,
           page_table, norm_mix, norm_ffn, norm_final, mlstm_w_in, mlstm_b_gates, mlstm_g_head, mlstm_w_out,
           mla_w_in, mla_g_q, mla_g_kv, mla_w_uq, mla_w_uk, mla_w_uv, mla_w_o, ffn_w_gate_up, ffn_w_down):
    B, S, D = x_prompt.shape
    BS, T, _ = x_sample.shape
    assert T == 1, "sample group is one new token per sequence"
    depth = norm_mix.shape[0]
    H = MLA_HEADS
    past_len = page_table.shape[1] * cache_latent.shape[2]

    hp = x_prompt.reshape(B * S, D)
    hs = x_sample.reshape(BS, D)
    TM = 512
    row = lambda v: v.reshape(1, -1).astype(F32)

    pos_p = jnp.arange(S, dtype=F32)
    pos_s = jnp.broadcast_to(jnp.arange(T, dtype=F32) + past_len, (BS,))
    rope_p = _rope_tables(pos_p)
    rope_s = _rope_tables(pos_s)

    assert depth == 2, "schedule below is written for one mLSTM layer followed by one MLA layer"
    g_fin = row(norm_final)

    n_gates = 2 * ML_HEADS
    ml_w_in = _pad_cols(mlstm_w_in[0], mlstm_w_in.shape[2] - n_gates + LANES).astype(BF16)
    ml_bias = _pad_cols(mlstm_b_gates[0].reshape(1, -1), LANES).astype(F32)
    ml_g_head = row(mlstm_g_head[0])
    ml_w_out = mlstm_w_out[0].astype(BF16)
    mla_in = _pad_cols(mla_w_in[0], Q_LORA + KV_LORA + LANES).astype(BF16)
    wq = mla_w_uq[0].reshape(Q_LORA, H, QK_NOPE + QK_ROPE)
    wq_nope = wq[:, :, :QK_NOPE].reshape(Q_LORA, H * QK_NOPE)
    wq_rope = jnp.pad(wq[:, :, QK_NOPE:], ((0, 0), (0, 0), (0, LANES - QK_ROPE))).reshape(Q_LORA, H * LANES)
    w_uq = jnp.concatenate([wq_nope, wq_rope], axis=1).astype(BF16)
    w_ukt = jnp.transpose(mla_w_uk[0].reshape(KV_LORA, H, QK_NOPE), (1, 2, 0)).astype(BF16)
    w_uv = jnp.transpose(mla_w_uv[0].reshape(KV_LORA, H, V_DIM), (1, 0, 2)).astype(BF16)
    w_uvt = jnp.transpose(mla_w_uv[0].reshape(KV_LORA, H, V_DIM), (1, 2, 0)).astype(BF16)
    w_o = mla_w_o[0].astype(BF16)
    g_q, g_kv = row(mla_g_q[0]), row(mla_g_kv[0])
    mla_weights = (mla_in, g_q, g_kv, w_uq, w_ukt)
    w_gu_all, w_d_all = ffn_w_gate_up.astype(BF16), ffn_w_down.astype(BF16)
    ffn = [(row(norm_ffn[l]), w_gu_all, w_d_all, l) for l in range(depth)]
    g_mix = [row(norm_mix[l]) for l in range(depth)]
    cache_k_rope_t = jnp.swapaxes(cache_k_rope, 2, 3)

    xw_s = _norm_matmul(hs, g_mix[0], ml_w_in, BS)
    a_s, C_s, n_s, m_s = _mlstm_sample(xw_s, mlstm_b_gates[0].astype(F32), ml_g_head, state_mlstm_C[0],
                                       state_mlstm_n[0], state_mlstm_m[0], TB=16)
    hs = _mixer_out_ffn(hs, a_s, ml_w_out, *ffn[0], g_fin, BS, final_norm=False)
    q_s, _, lat_s, kr_s = _mla_project(hs, g_mix[1], *mla_weights, *rope_s, tm=BS, table_blocks=1, vt_tile=0)

    xw_p = _norm_matmul(hp, g_mix[0], ml_w_in, TM)
    o_s, a_p, C_p, n_p, m_p = _attention_sample_with_mlstm_prompt(
        q_s.reshape(BS, H, Q_SLOT), lat_s, kr_s, cache_latent, cache_k_rope_t, page_table, layer=0, G=64,
        xw=xw_p, bias=ml_bias, g_head=ml_g_head, B=B, S=S, L=256)
    hp = _mixer_out_ffn(hp, a_p, ml_w_out, *ffn[0], g_fin, TM, final_norm=False)

    T_ATT = 512
    q_p, kcat_p, lat_p, kr_p, vt_p = _mla_project(hp, g_mix[1], *mla_weights, *rope_p,
                                                  tm=TM, table_blocks=S // TM, vt_tile=T_ATT)
    a_p = _attention_prompt(q_p, kcat_p, vt_p, w_uvt, B, S, T=T_ATT)
    hp = _mixer_out_ffn(hp, a_p, w_o, *ffn[1], g_fin, TM, final_norm=True)

    a_s = _value_up(o_s.reshape(BS, H * KV_LORA), w_uv)
    hs = _mixer_out_ffn(hs, a_s, w_o, *ffn[1], g_fin, BS, final_norm=True)

    return (hp.reshape(B, S, D), hs.reshape(BS, T, D),
            C_p[None], n_p[None], m_p.reshape(1, B, -1),
            C_s[None], n_s[None], m_s.reshape(1, BS, -1),
            lat_p.reshape(1, B, S, KV_LORA), kr_p.reshape(1, B, S, QK_ROPE),
            lat_s.reshape(1, BS, T, KV_LORA), kr_s.reshape(1, BS, T, QK_ROPE))
```

```python
import functools

import jax
import jax.numpy as jnp
from jax import lax
from jax.experimental import pallas as pl
from jax.experimental.pallas import tpu as pltpu

F32 = jnp.float32
BF16 = jnp.bfloat16
EPS = 1e-6
ROPE_THETA = 10000.0

V7X_VMEM_BYTES = 64 * 1024 * 1024
LANES = 128
VMEM_LIMIT_BYTES = V7X_VMEM_BYTES - 8 * 1024 * 1024

ML_HEADS = 8
MLA_HEADS = 8
Q_LORA = 384
KV_LORA = 256
QK_NOPE = 128
QK_ROPE = 64
V_DIM = 128
Q_SLOT = KV_LORA + LANES
MLA_SCALE = (QK_NOPE + QK_ROPE) ** -0.5

NT_DIMS = (((1,), (1,)), ((), ()))


def _params(n_grid_axes):
    return pltpu.CompilerParams(
        dimension_semantics=("arbitrary",) * n_grid_axes,
        vmem_limit_bytes=VMEM_LIMIT_BYTES,
    )


def _rms(x, g):
    return x * lax.rsqrt(jnp.mean(x * x, axis=-1, keepdims=True) + EPS) * g


def _log_sigmoid(x):
    return jnp.minimum(x, 0.0) - jnp.log1p(jnp.exp(-jnp.abs(x)))


def _resident(shape):
    nd = len(shape)
    return pl.BlockSpec(shape, lambda *_: (0,) * nd)


def _norm_matmul_kernel(x_ref, g_ref, w_ref, o_ref):
    xn = _rms(x_ref[...], g_ref[...]).astype(BF16)
    o_ref[...] = jnp.dot(xn, w_ref[...], preferred_element_type=F32)


def _norm_matmul(x, g, w, tm):
    M, D = x.shape
    N = w.shape[1]
    return pl.pallas_call(
        _norm_matmul_kernel,
        out_shape=jax.ShapeDtypeStruct((M, N), F32),
        grid=(M // tm,),
        in_specs=[pl.BlockSpec((tm, D), lambda i: (i, 0)), _resident((1, D)), _resident((D, N))],
        out_specs=pl.BlockSpec((tm, N), lambda i: (i, 0)),
        compiler_params=_params(1),
        name="norm_matmul",
    )(x, g, w)


def _mlstm_reset(caug_ref, mst_ref):
    caug_ref[...] = jnp.zeros_like(caug_ref)
    mst_ref[...] = jnp.zeros_like(mst_ref)


def _mlstm_chunk_phases(q_ref, k_ref, v_ref, o_ref, gt_ref, bias_ref, gh_ref, hg_ref, caug_ref, mst_ref,
                        *, L, H, DK, DV):
    gates = gt_ref[...] + bias_ref[...]
    lane = lax.broadcasted_iota(jnp.int32, gates.shape, 1)
    G = jnp.where(lane < H, gates, _log_sigmoid(gates))
    row = lax.broadcasted_iota(jnp.int32, (L, L), 0)
    col = lax.broadcasted_iota(jnp.int32, (L, L), 1)
    causal = col <= row
    cs = jnp.dot(causal.astype(F32), G, precision=lax.Precision.HIGHEST,
                 preferred_element_type=F32)
    GT = G.T
    csT = cs.T

    kT = (k_ref[...] * (DK ** -0.5)).T
    ones = jnp.ones((L, DV), F32)
    heads = range(H)


    b_col = [cs[:, H + h:H + h + 1] for h in heads]
    b_row = [csT[H + h:H + h + 1, :] for h in heads]
    i_row = [GT[h:h + 1, :] for h in heads]
    r_mat = [jnp.where(causal, i_row[h] - b_row[h], -jnp.inf) for h in heads]
    r_max = [jnp.max(r_mat[h], axis=1, keepdims=True) for h in heads]
    yield

    m_prev = [mst_ref[h:h + 1, 0:1] for h in heads]
    log_inter = [b_col[h] + m_prev[h] for h in heads]
    m_t = [jnp.maximum(log_inter[h], b_col[h] + r_max[h]) for h in heads]
    w_inter = [jnp.exp(log_inter[h] - m_t[h]) for h in heads]
    W = [jnp.exp(r_mat[h] + (b_col[h] - m_t[h])) for h in heads]

    lane_pair = lax.broadcasted_iota(jnp.int32, (L, 2 * DK), 1)
    S, inter, caug_prev = [None] * H, [None] * H, [None] * H
    for p in range(H // 2):
        q_pair = q_ref[:, p * 2 * DK:(p + 1) * 2 * DK]
        q_lo = jnp.where(lane_pair < DK, q_pair, 0.0)
        q_hi = jnp.where(lane_pair < DK, 0.0, q_pair)
        qm2 = jnp.concatenate([q_lo, q_hi], axis=0).astype(BF16)
        k_pair = (k_ref[:, p * 2 * DK:(p + 1) * 2 * DK] * (DK ** -0.5)).astype(BF16)
        s2 = lax.dot_general(qm2, k_pair, NT_DIMS, preferred_element_type=F32)
        caug_pair = caug_ref[p]
        i2 = jnp.dot(qm2, caug_pair.astype(BF16), preferred_element_type=F32)
        for half in range(2):
            h = 2 * p + half
            S[h] = s2[half * L:(half + 1) * L] * W[h]
            inter[h] = i2[half * L:(half + 1) * L]
            caug_prev[h] = caug_pair[half * DK:(half + 1) * DK, :]
    yield

    m_new = [m_t[h][L - 1:L, :] for h in heads]
    b_last = [cs[L - 1:L, H + h:H + h + 1] for h in heads]
    R = []
    for h in heads:
        w_s_row = jnp.exp(b_last[h] - b_row[h] + i_row[h] - m_new[h])
        kwT = (kT[h * DK:(h + 1) * DK, :] * w_s_row).astype(BF16)
        vaug = jnp.concatenate([v_ref[:, h * DV:(h + 1) * DV], ones], axis=1).astype(BF16)
        lhs = jnp.concatenate([S[h].astype(BF16), kwT], axis=0)
        R.append(jnp.dot(lhs, vaug, preferred_element_type=F32))
    yield

    hh = []
    for h in heads:
        numden = w_inter[h] * inter[h] + R[h][:L]
        num, den = numden[:, :DV], numden[:, DV:]
        hh.append(num / jnp.maximum(jnp.abs(den), jnp.exp(-m_t[h])))
    ms = [jnp.mean(hh[h] * hh[h], axis=-1, keepdims=True) for h in heads]
    for h in heads:
        hn = hh[h] * lax.rsqrt(ms[h] + EPS)
        og = jax.nn.sigmoid(o_ref[:, h * DV:(h + 1) * DV])
        hg_ref[:, h * DV:(h + 1) * DV] = (og * (hn * gh_ref[:, h * DV:(h + 1) * DV])).astype(BF16)

    for h in heads:
        p, r0 = h // 2, (h % 2) * DK
        w_c = jnp.exp(b_last[h] + m_prev[h] - m_new[h])
        caug_ref[p, r0:r0 + DK, :] = w_c * caug_prev[h] + R[h][L:]
        mst_ref[h:h + 1, :] = jnp.broadcast_to(m_new[h], (1, LANES))


def _mlstm_write_state(caug_ref, mst_ref, c_out_ref, n_out_ref, m_out_ref, *, H, DK, DV):
    pick0 = (lax.broadcasted_iota(jnp.int32, (8, DV), 1) == 0).astype(F32)
    for h in range(H):
        p, r0 = h // 2, (h % 2) * DK
        ca = caug_ref[p, r0:r0 + DK, :]
        c_out_ref[0, h] = ca[:, :DV]
        n_rows = lax.dot_general(pick0, ca[:, DV:], NT_DIMS, precision=lax.Precision.HIGHEST,
                                 preferred_element_type=F32)
        n_out_ref[0, h:h + 1, :] = n_rows[0:1, :]
        m_out_ref[0, :, h:h + 1] = mst_ref[h:h + 1, 0:1]


def _mlstm_step_kernel(q_ref, k_ref, v_ref, o_ref, gi_ref, gf_ref, bi_ref, bf_ref, gh_ref,
                       c0_ref, n0_ref, m0_ref,
                       hg_ref, c_out_ref, n_out_ref, m_out_ref, *, H, DK, DV, TB):
    RB = TB * H
    i_pre = gi_ref[...] + bi_ref[...]
    log_f = _log_sigmoid(gf_ref[...] + bf_ref[...])
    log_inter = log_f + m0_ref[...]
    m_t = jnp.maximum(log_inter, i_pre)
    w_inter = jnp.exp(log_inter - m_t)
    w_intra = jnp.exp(i_pre - m_t)

    q = q_ref[...]
    k = k_ref[...] * (DK ** -0.5)
    v = v_ref[...]
    n_prev = n0_ref[...]
    s = jnp.sum(q * k, axis=1, keepdims=True) * w_intra
    den = w_inter * jnp.sum(q * n_prev, axis=1, keepdims=True) + s

    def block_diag(x):
        wide = jnp.concatenate([x] * H, axis=1)
        head_of_lane = lax.broadcasted_iota(jnp.int32, wide.shape, 1) // DK
        head_of_row = lax.broadcasted_iota(jnp.int32, wide.shape, 0) % H
        return jnp.where(head_of_lane == head_of_row, wide, 0.0)

    q_bd = block_diag(q)
    kw_bd = block_diag(k * w_intra[:, :DK])
    qc = jnp.concatenate(
        [jnp.dot(q_bd[t * H:(t + 1) * H, :], c0_ref[t], preferred_element_type=F32) for t in range(TB)],
        axis=0)

    num = w_inter * qc + s * v
    hh = num / jnp.maximum(jnp.abs(den), jnp.exp(-m_t))
    hn = hh * lax.rsqrt(jnp.mean(hh * hh, axis=-1, keepdims=True) + EPS)
    hg_ref[...] = (jax.nn.sigmoid(o_ref[...]) * (hn * gh_ref[...])).astype(BF16)
    n_out_ref[...] = w_inter[:, :DK] * n_prev + w_intra[:, :DK] * k
    m_out_ref[...] = m_t

    for t in range(TB):
        rows = slice(t * H, (t + 1) * H)
        d_c = lax.dot_general(kw_bd[rows, :], v[rows, :], (((0,), (0,)), ((), ())),
                              preferred_element_type=F32)
        for h in range(H):
            blk = slice(h * DK, (h + 1) * DK)
            c_out_ref[t, blk, :] = w_inter[t * H + h:t * H + h + 1, :] * c0_ref[t, blk, :] + d_c[blk, :]


def _mlstm_sample(xw, b_gates, g_head, c0, n0, m0, TB):
    H, DK, DV = ML_HEADS, 64, 128
    B = xw.shape[0]
    R, RB = B * H, TB * H
    qk_w, v_w = H * DK, H * DV
    lanes = lambda x: jnp.broadcast_to(x.reshape(-1, 1), (x.size, LANES))
    per_block = lambda x: jnp.tile(x, (TB, 1))
    q = xw[:, :qk_w].reshape(R, DK)
    k = xw[:, qk_w:2 * qk_w].reshape(R, DK)
    v = xw[:, 2 * qk_w:2 * qk_w + v_w].reshape(R, DV)
    o = xw[:, 2 * qk_w + v_w:2 * qk_w + 2 * v_w].reshape(R, DV)
    g0 = 2 * qk_w + 2 * v_w
    operands = (
        q, k, v, o,
        lanes(xw[:, g0:g0 + H]), lanes(xw[:, g0 + H:g0 + 2 * H]),
        per_block(lanes(b_gates[:H])), per_block(lanes(b_gates[H:])),
        per_block(g_head.reshape(H, DV)),
        c0.reshape(B, H * DK, DV), n0.reshape(R, DK), lanes(m0),
    )
    row_blk = lambda n: pl.BlockSpec((RB, n), lambda b: (b, 0))
    state_blk = pl.BlockSpec((TB, H * DK, DV), lambda b: (b, 0, 0))
    kern = functools.partial(_mlstm_step_kernel, H=H, DK=DK, DV=DV, TB=TB)
    hg, c_new, n_new, m_new = pl.pallas_call(
        kern,
        out_shape=(
            jax.ShapeDtypeStruct((R, DV), BF16),
            jax.ShapeDtypeStruct((B, H * DK, DV), F32),
            jax.ShapeDtypeStruct((R, DK), F32),
            jax.ShapeDtypeStruct((R, LANES), F32),
        ),
        grid=(B // TB,),
        in_specs=[
            row_blk(DK), row_blk(DK), row_blk(DV), row_blk(DV),
            row_blk(LANES), row_blk(LANES),
            _resident((RB, LANES)), _resident((RB, LANES)), _resident((RB, DV)),
            state_blk, row_blk(DK), row_blk(LANES),
        ],
        out_specs=(row_blk(DV), state_blk, row_blk(DK), row_blk(LANES)),
        compiler_params=_params(1),
        name="mlstm_sample",
    )(*operands)
    return (hg.reshape(B, v_w), c_new.reshape(B, H, DK, DV), n_new.reshape(B, H, DK),
            m_new[:, 0].reshape(B, H))


def _ffn_kernel(h_ref, a_ref, wa_ref, gn_ref, wgu_ref, wd_ref, gf_ref, o_ref, *, d_ff, tf, final_norm, a_tile):
    if a_tile:
        proj = jnp.concatenate(
            [lax.dot_general(a_ref[t], wa_ref[...], (((0,), (0,)), ((), ())), preferred_element_type=F32)
             for t in range(a_ref.shape[0])], axis=0)
    else:
        proj = jnp.dot(a_ref[...], wa_ref[...], preferred_element_type=F32)
    h1 = h_ref[...] + proj
    xn = _rms(h1, gn_ref[...]).astype(BF16)
    acc = h1
    for c in range(d_ff // tf):
        g = jnp.dot(xn, wgu_ref[:, c * tf:(c + 1) * tf], preferred_element_type=F32)
        u = jnp.dot(xn, wgu_ref[:, d_ff + c * tf:d_ff + (c + 1) * tf], preferred_element_type=F32)
        act = (g * jax.nn.sigmoid(g) * u).astype(BF16)
        acc = acc + jnp.dot(act, wd_ref[c * tf:(c + 1) * tf, :], preferred_element_type=F32)
    if final_norm:
        acc = _rms(acc, gf_ref[...])
    o_ref[...] = acc


def _mixer_out_ffn(h, a, w_a, g_ffn, w_gu, w_d, layer, g_final, tm, final_norm):
    M, D = h.shape
    KA = w_a.shape[0]
    d_ff = w_d.shape[1]
    a_tile = a.shape[2] if a.ndim == 3 else 0
    kern = functools.partial(_ffn_kernel, d_ff=d_ff, tf=256, final_norm=final_norm, a_tile=a_tile)
    single = pl.Buffered(1)
    a_spec = (pl.BlockSpec((tm // a_tile, KA, a_tile), lambda i: (i, 0, 0)) if a_tile
              else pl.BlockSpec((tm, KA), lambda i: (i, 0)))
    return pl.pallas_call(
        kern,
        out_shape=jax.ShapeDtypeStruct((M, D), F32),
        grid=(M // tm,),
        in_specs=[
            pl.BlockSpec((tm, D), lambda i: (i, 0)),
            a_spec,
            pl.BlockSpec((KA, D), lambda i: (0, 0), pipeline_mode=single),
            _resident((1, D)),
            pl.BlockSpec((None, D, 2 * d_ff), lambda i: (layer, 0, 0), pipeline_mode=single),
            pl.BlockSpec((None, d_ff, D), lambda i: (layer, 0, 0), pipeline_mode=single),
            _resident((1, D)),
        ],
        out_specs=pl.BlockSpec((tm, D), lambda i: (i, 0)),
        compiler_params=_params(1),
        name="mixer_out_ffn",
    )(h, a, w_a, g_ffn, w_gu, w_d, g_final)


def _rope_slot(x, cos, sin_lo, sin_hi):
    return x * cos + pltpu.roll(x, 96, 1) * sin_lo + pltpu.roll(x, 32, 1) * sin_hi


def _mla_proj_kernel(h_ref, gn_ref, win_ref, gq_ref, gkv_ref, wuq_ref, wukt_ref,
                     cos_ref, sinlo_ref, sinhi_ref,
                     q_ref, kcat_ref, lat_ref, kr_ref, *maybe_vt_ref, H, vt_tile):
    xn = _rms(h_ref[...], gn_ref[...]).astype(BF16)
    t = jnp.dot(xn, win_ref[...], preferred_element_type=F32)
    c_q = t[:, :Q_LORA]
    c_kv = t[:, Q_LORA:Q_LORA + KV_LORA]
    k_slot = t[:, Q_LORA + KV_LORA:]
    cos, sin_lo, sin_hi = cos_ref[...], sinlo_ref[...], sinhi_ref[...]

    lat = _rms(c_kv, gkv_ref[...])
    k_rot = _rope_slot(k_slot, cos, sin_lo, sin_hi)
    lat_ref[...] = lat
    kr_ref[...] = k_rot[:, :QK_ROPE]
    kcat_ref[:, :KV_LORA] = lat.astype(BF16)
    kcat_ref[:, KV_LORA:] = k_rot.astype(BF16)
    if vt_tile:
        (vt_ref,) = maybe_vt_ref
        for t in range(lat.shape[0] // vt_tile):
            vt_ref[t] = lat[t * vt_tile:(t + 1) * vt_tile, :].T.astype(BF16)

    cqn = _rms(c_q, gq_ref[...]).astype(BF16)
    tm = cqn.shape[0]

    def store_q(h, lanes, val):
        if vt_tile:
            sub = vt_tile // 2
            for t in range(tm // vt_tile):
                for half in range(2):
                    src = slice(t * vt_tile + half * sub, t * vt_tile + (half + 1) * sub)
                    dst = slice((half * H + h) * sub, (half * H + h + 1) * sub)
                    q_ref[t, dst, lanes] = val[src].astype(BF16)
        else:
            q_ref[:, slice(h * Q_SLOT + lanes.start, h * Q_SLOT + lanes.stop)] = val.astype(BF16)

    q_rope = jnp.dot(cqn, wuq_ref[:, H * QK_NOPE:], preferred_element_type=F32)
    for h in range(H):
        store_q(h, slice(KV_LORA, Q_SLOT), _rope_slot(q_rope[:, h * LANES:(h + 1) * LANES], cos, sin_lo, sin_hi))
    q_nope = jnp.dot(cqn, wuq_ref[:, :H * QK_NOPE], preferred_element_type=F32)
    for h in range(H):
        q_lat = jnp.dot(q_nope[:, h * QK_NOPE:(h + 1) * QK_NOPE].astype(BF16), wukt_ref[h],
                        preferred_element_type=F32)
        store_q(h, slice(0, KV_LORA), q_lat)


def _mla_project(h, g_norm, w_in, g_q, g_kv, w_uq, w_ukt, cos, sin_lo, sin_hi, tm, table_blocks, vt_tile):
    M, D = h.shape
    H = MLA_HEADS
    kern = functools.partial(_mla_proj_kernel, H=H, vt_tile=vt_tile)
    table = pl.BlockSpec((tm, LANES), lambda i: (i % table_blocks, 0))
    out_shape = [
        jax.ShapeDtypeStruct((M, H * Q_SLOT), BF16),
        jax.ShapeDtypeStruct((M, Q_SLOT), BF16),
        jax.ShapeDtypeStruct((M, KV_LORA), F32),
        jax.ShapeDtypeStruct((M, QK_ROPE), F32),
    ]
    out_specs = [
        pl.BlockSpec((tm, H * Q_SLOT), lambda i: (i, 0)),
        pl.BlockSpec((tm, Q_SLOT), lambda i: (i, 0)),
        pl.BlockSpec((tm, KV_LORA), lambda i: (i, 0)),
        pl.BlockSpec((tm, QK_ROPE), lambda i: (i, 0)),
    ]
    if vt_tile:
        out_shape[0] = jax.ShapeDtypeStruct((M // vt_tile, H * vt_tile, Q_SLOT), BF16)
        out_specs[0] = pl.BlockSpec((tm // vt_tile, H * vt_tile, Q_SLOT), lambda i: (i, 0, 0))
        out_shape.append(jax.ShapeDtypeStruct((M // vt_tile, KV_LORA, vt_tile), BF16))
        out_specs.append(pl.BlockSpec((tm // vt_tile, KV_LORA, vt_tile), lambda i: (i, 0, 0)))
    return pl.pallas_call(
        kern,
        out_shape=tuple(out_shape),
        grid=(M // tm,),
        in_specs=[
            pl.BlockSpec((tm, D), lambda i: (i, 0)),
            _resident((1, D)),
            _resident(w_in.shape),
            _resident((1, Q_LORA)),
            _resident((1, KV_LORA)),
            _resident(w_uq.shape),
            _resident(w_ukt.shape),
            table, table, table,
        ],
        out_specs=tuple(out_specs),
        compiler_params=_params(1),
        name="mla_project",
    )(h, g_norm, w_in, g_q, g_kv, w_uq, w_ukt, cos, sin_lo, sin_hi)


def _attn_kernel(q_ref, k_ref, vt_ref, wuvt_ref, o_ref, m_ref, l_ref, acc_ref, sa_ref, sb_ref, *, H, T):
    i = pl.program_id(1)

    def scores(j):
        kj = k_ref[0, pl.ds(pl.multiple_of(j * T, T), T), :]
        return lax.dot_general(kj, q_ref[0], NT_DIMS, preferred_element_type=F32) * MLA_SCALE

    def update(j, st):
        vtj = vt_ref[j]
        m_prev = m_ref[...]
        m_new = jnp.maximum(m_prev, jnp.max(st, axis=0, keepdims=True))
        alpha = jnp.exp(m_prev - m_new)
        p = jnp.exp(st - m_new)
        l_ref[...] = alpha * l_ref[...] + jnp.sum(p, axis=0, keepdims=True)
        acc_ref[...] = alpha * acc_ref[...] + jnp.dot(vtj, p.astype(BF16), preferred_element_type=F32)
        m_ref[...] = m_new

    SUB, C2 = T // 2, H * T // 2
    base = pl.multiple_of(i * T, T)
    k_a = k_ref[0, pl.ds(base, SUB), :]
    k_b = k_ref[0, pl.ds(base + SUB, SUB), :]
    vt_i = vt_ref[i]
    vt_a, vt_b = vt_i[:, :SUB], vt_i[:, SUB:]
    s_a = lax.dot_general(k_a, q_ref[0], NT_DIMS, preferred_element_type=F32) * MLA_SCALE
    s_b = lax.dot_general(k_b, q_ref[0, C2:, :], NT_DIMS, preferred_element_type=F32) * MLA_SCALE
    key = lax.broadcasted_iota(jnp.int32, (SUB, C2), 0)
    qry = lax.broadcasted_iota(jnp.int32, (SUB, C2), 1) & (SUB - 1)
    causal = key <= qry
    s_a0 = jnp.where(causal, s_a[:, :C2], -jnp.inf)
    s_a1 = s_a[:, C2:]
    s_b1 = jnp.where(causal, s_b, -jnp.inf)
    m_0 = jnp.max(s_a0, axis=0, keepdims=True)
    m_1 = jnp.maximum(jnp.max(s_a1, axis=0, keepdims=True), jnp.max(s_b1, axis=0, keepdims=True))
    p_a0, p_a1, p_b1 = jnp.exp(s_a0 - m_0), jnp.exp(s_a1 - m_1), jnp.exp(s_b1 - m_1)
    m_ref[:, :C2] = m_0
    m_ref[:, C2:] = m_1
    l_ref[:, :C2] = jnp.sum(p_a0, axis=0, keepdims=True)
    l_ref[:, C2:] = jnp.sum(p_a1, axis=0, keepdims=True) + jnp.sum(p_b1, axis=0, keepdims=True)
    acc_ref[:, :C2] = jnp.dot(vt_a, p_a0.astype(BF16), preferred_element_type=F32)
    acc_ref[:, C2:] = (jnp.dot(vt_a, p_a1.astype(BF16), preferred_element_type=F32)
                       + jnp.dot(vt_b, p_b1.astype(BF16), preferred_element_type=F32))

    @pl.when(i > 0)
    def _():
        sa_ref[...] = scores(0)

    def pair(jj, carry):
        j = 2 * jj
        sb_ref[...] = scores(j + 1)
        update(j, sa_ref[...])
        sa_ref[...] = scores(jnp.minimum(j + 2, i - 1))
        update(j + 1, sb_ref[...])
        return carry

    lax.fori_loop(0, i // 2, pair, 0)

    @pl.when(i % 2 == 1)
    def _():
        update(i - 1, sa_ref[...])

    o_t = (acc_ref[...] / l_ref[...]).astype(BF16)
    for half in range(2):
        for h in range(H):
            cols = slice((half * H + h) * SUB, (half * H + h + 1) * SUB)
            v_t = jnp.dot(wuvt_ref[h], o_t[:, cols], preferred_element_type=F32)
            o_ref[0, h * V_DIM:(h + 1) * V_DIM, half * SUB:(half + 1) * SUB] = v_t.astype(BF16)


def _attention_prompt(q, kcat, vt, w_uvt, B, S, T):
    H = MLA_HEADS
    NQ = S // T
    kern = functools.partial(_attn_kernel, H=H, T=T)
    return pl.pallas_call(
        kern,
        out_shape=jax.ShapeDtypeStruct((B * NQ, H * V_DIM, T), BF16),
        grid=(B, NQ),
        in_specs=[
            pl.BlockSpec((1, H * T, Q_SLOT), lambda b, i: (b * NQ + i, 0, 0)),
            pl.BlockSpec((1, S, Q_SLOT), lambda b, i: (b, 0, 0)),
            pl.BlockSpec((NQ, KV_LORA, T), lambda b, i: (b, 0, 0)),
            _resident(w_uvt.shape),
        ],
        out_specs=pl.BlockSpec((1, H * V_DIM, T), lambda b, i: (b * NQ + i, 0, 0)),
        scratch_shapes=[
            pltpu.VMEM((1, H * T), F32),
            pltpu.VMEM((1, H * T), F32),
            pltpu.VMEM((KV_LORA, H * T), F32),
            pltpu.VMEM((T, H * T), F32),
            pltpu.VMEM((T, H * T), F32),
        ],
        compiler_params=_params(2),
        name="attention_prompt",
    )(q, kcat.reshape(B, S, Q_SLOT), vt, w_uvt)


def _decode_mlstm_kernel(pt_ref,
                         q_ref, cn_ref, krn_ref, lat_hbm, krt_hbm,
                         mq_ref, mk_ref, mv_ref, mo_ref, gt_ref, bias_ref, gh_ref,
                         o_ref, hg_ref, c_out_ref, n_out_ref, m_out_ref,
                         lat_buf, kr_buf, sem, caug_ref, mst_ref,
                         *, layer, G, P, NCH, RING, STREAMS, L, H_ML, DK, DV, NC):
    s = pl.program_id(0)
    ns = pl.num_programs(0)
    total = ns * NCH
    ml = dict(L=L, H=H_ML, DK=DK, DV=DV)

    def page_copies(c):
        slot = lax.rem(c, RING)
        cw = jnp.where(c >= total, c - total, c)
        bb, jj = lax.div(cw, NCH), lax.rem(cw, NCH)
        copies = []
        for g in range(G):
            page = pt_ref[bb, jj * G + g]
            copies.append(pltpu.make_async_copy(
                lat_hbm.at[layer, page], lat_buf.at[slot, pl.ds(g * P, P), :], sem.at[0, slot]))
            copies.append(pltpu.make_async_copy(
                krt_hbm.at[layer, page], kr_buf.at[slot, :, pl.ds(g * P, P)], sem.at[1, slot]))
        return copies

    def start(c):
        for cp in page_copies(c):
            cp.start()

    def wait(c):
        for cp in page_copies(c):
            cp.wait()

    @pl.when(s == 0)
    def _():
        for c in range(RING - 1):
            start(jnp.int32(c))

    ml_chunk = lax.rem(s, NC)

    @pl.when(ml_chunk == 0)
    def _():
        _mlstm_reset(caug_ref, mst_ref)

    q = q_ref[0].astype(F32)
    q_lat = q[:, :KV_LORA]
    q_rope = q[:, KV_LORA:KV_LORA + QK_ROPE]
    c_new = cn_ref[0]
    kr_new = krn_ref[0]

    s_new = (jnp.sum(q_lat * c_new, axis=1, keepdims=True)
             + jnp.sum(q_rope * kr_new, axis=1, keepdims=True)) * MLA_SCALE
    n_heads = q.shape[0]
    m_run = [s_new] + [jnp.full_like(s_new, -jnp.inf)] * (STREAMS - 1)
    l_run = [jnp.ones_like(s_new)] + [jnp.zeros_like(s_new)] * (STREAMS - 1)
    acc = ([jnp.broadcast_to(c_new, (n_heads, KV_LORA)).astype(F32)]
           + [jnp.zeros((n_heads, KV_LORA), F32)] * (STREAMS - 1))
    W = (G * P) // STREAMS

    ml_pieces = _mlstm_chunk_phases(mq_ref, mk_ref, mv_ref, mo_ref, gt_ref, bias_ref, gh_ref, hg_ref,
                                    caug_ref, mst_ref, **ml)
    assert NCH >= 2
    for j in range(NCH):
        c = s * NCH + j
        slot = lax.rem(c, RING)
        wait(c)
        kl = [lat_buf[slot, i * W:(i + 1) * W, :] for i in range(STREAMS)]
        sc = [(lax.dot_general(q_lat, kl[i], NT_DIMS, preferred_element_type=F32)
               + jnp.dot(q_rope, kr_buf[slot, :, i * W:(i + 1) * W], preferred_element_type=F32)) * MLA_SCALE
              for i in range(STREAMS)]
        if j in (0, NCH - 1):
            next(ml_pieces)
        for i in range(STREAMS):
            m_new = jnp.maximum(m_run[i], jnp.max(sc[i], axis=1, keepdims=True))
            alpha = jnp.exp(m_run[i] - m_new)
            p = jnp.exp(sc[i] - m_new)
            m_run[i] = m_new
            l_run[i] = alpha * l_run[i] + jnp.sum(p, axis=1, keepdims=True)
            acc[i] = alpha * acc[i] + jnp.dot(p, kl[i], preferred_element_type=F32)
        if j in (0, NCH - 1):
            next(ml_pieces, None)
        start(c + (RING - 1))

    m_all = functools.reduce(jnp.maximum, m_run)
    scale = [jnp.exp(m_i - m_all) for m_i in m_run]
    l_all = sum(l_i * w_i for l_i, w_i in zip(l_run, scale))
    acc_all = sum(a_i * w_i for a_i, w_i in zip(acc, scale))
    o_ref[0] = acc_all / l_all

    @pl.when(ml_chunk == NC - 1)
    def _():
        _mlstm_write_state(caug_ref, mst_ref, c_out_ref, n_out_ref, m_out_ref, H=H_ML, DK=DK, DV=DV)

    @pl.when(s == ns - 1)
    def _():
        for c in range(RING - 1):
            wait(total + c)


def _attention_sample_with_mlstm_prompt(q3, c_new, kr_new, cache_latent, cache_k_rope_t, page_table, layer, G,
                                        xw, bias, g_head, B, S, L):
    BS, H, _ = q3.shape
    n_pages = page_table.shape[1]
    P = cache_latent.shape[2]
    NCH = n_pages // G
    assert n_pages % G == 0
    RING = 3
    H_ML, DK, DV = ML_HEADS, 64, 128
    NC = S // L
    assert B * NC == BS, "one mLSTM chunk per decode sequence"
    qk_w, v_w = H_ML * DK, H_ML * DV
    gate_blk = (2 * qk_w + 2 * v_w) // LANES
    kern = functools.partial(_decode_mlstm_kernel, layer=layer, G=G, P=P, NCH=NCH, RING=RING, STREAMS=2,
                             L=L, H_ML=H_ML, DK=DK, DV=DV, NC=NC)
    grid_spec = pltpu.PrefetchScalarGridSpec(
        num_scalar_prefetch=1,
        grid=(BS,),
        in_specs=[
            pl.BlockSpec((1, H, Q_SLOT), lambda s, pt: (s, 0, 0)),
            pl.BlockSpec((1, 1, KV_LORA), lambda s, pt: (s, 0, 0)),
            pl.BlockSpec((1, 1, QK_ROPE), lambda s, pt: (s, 0, 0)),
            pl.BlockSpec(memory_space=pl.ANY),
            pl.BlockSpec(memory_space=pl.ANY),
            pl.BlockSpec((L, qk_w), lambda s, pt: (s, 0)),
            pl.BlockSpec((L, qk_w), lambda s, pt: (s, 1)),
            pl.BlockSpec((L, v_w), lambda s, pt: (s, 1)),
            pl.BlockSpec((L, v_w), lambda s, pt: (s, 2)),
            pl.BlockSpec((L, LANES), lambda s, pt: (s, gate_blk)),
            pl.BlockSpec((1, LANES), lambda s, pt: (0, 0)),
            pl.BlockSpec((1, v_w), lambda s, pt: (0, 0)),
        ],
        out_specs=(
            pl.BlockSpec((1, H, KV_LORA), lambda s, pt: (s, 0, 0)),
            pl.BlockSpec((L, v_w), lambda s, pt: (s, 0)),
            pl.BlockSpec((1, H_ML, DK, DV), lambda s, pt: (s // NC, 0, 0, 0)),
            pl.BlockSpec((1, H_ML, DK), lambda s, pt: (s // NC, 0, 0)),
            pl.BlockSpec((1, 1, H_ML), lambda s, pt: (s // NC, 0, 0)),
        ),
        scratch_shapes=[
            pltpu.VMEM((RING, G * P, KV_LORA), F32),
            pltpu.VMEM((RING, QK_ROPE, G * P), F32),
            pltpu.SemaphoreType.DMA((2, RING)),
            pltpu.VMEM((H_ML // 2, 2 * DK, 2 * DV), F32),
            pltpu.VMEM((H_ML, LANES), F32),
        ],
    )
    return pl.pallas_call(
        kern,
        out_shape=(
            jax.ShapeDtypeStruct((BS, H, KV_LORA), F32),
            jax.ShapeDtypeStruct((B * S, v_w), BF16),
            jax.ShapeDtypeStruct((B, H_ML, DK, DV), F32),
            jax.ShapeDtypeStruct((B, H_ML, DK), F32),
            jax.ShapeDtypeStruct((B, 1, H_ML), F32),
        ),
        grid_spec=grid_spec,
        compiler_params=_params(1),
        name="attention_sample_mlstm_prompt",
    )(page_table, q3, c_new.reshape(BS, 1, KV_LORA), kr_new.reshape(BS, 1, QK_ROPE),
      cache_latent, cache_k_rope_t, xw, xw, xw, xw, xw, bias, g_head)


def _value_up_kernel(o_ref, wuv_ref, v_ref, *, H):
    for h in range(H):
        o_h = o_ref[:, h * KV_LORA:(h + 1) * KV_LORA].astype(BF16)
        v_ref[:, h * V_DIM:(h + 1) * V_DIM] = jnp.dot(
            o_h, wuv_ref[h], preferred_element_type=F32).astype(BF16)


def _value_up(o_lat, w_uv):
    M = o_lat.shape[0]
    H = MLA_HEADS
    return pl.pallas_call(
        functools.partial(_value_up_kernel, H=H),
        out_shape=jax.ShapeDtypeStruct((M, H * V_DIM), BF16),
        grid=(1,),
        in_specs=[_resident(o_lat.shape), _resident(w_uv.shape)],
        out_specs=_resident((M, H * V_DIM)),
        compiler_params=_params(1),
        name="value_up",
    )(o_lat, w_uv)


def _rope_tables(pos):
    half = QK_ROPE // 2
    inv = ROPE_THETA ** (-jnp.arange(0, QK_ROPE, 2, dtype=F32) / QK_ROPE)
    ang = pos[:, None] * inv[None, :]
    cos, sin = jnp.cos(ang), jnp.sin(ang)
    z = jnp.zeros_like(cos)
    cos_t = jnp.concatenate([cos, cos, z, z], axis=1)
    sin_lo = jnp.concatenate([-sin, z, z, z], axis=1)
    sin_hi = jnp.concatenate([z, sin, z, z], axis=1)
    return cos_t, sin_lo, sin_hi


def _pad_cols(w, n):
    return jnp.pad(w, ((0, 0), (0, n - w.shape[1])))


def kernel(x_prompt, x_sample, state_mlstm_C, state_mlstm_n, state_mlstm_m, cache_latent, cache_k_rope,
           page_table, norm_mix, norm_ffn, norm_final, mlstm_w_in, mlstm_b_gates, mlstm_g_head, mlstm_w_out,
           mla_w_in, mla_g_q, mla_g_kv, mla_w_uq, mla_w_uk, mla_w_uv, mla_w_o, ffn_w_gate_up, ffn_w_down):
    B, S, D = x_prompt.shape
    BS, T, _ = x_sample.shape
    assert T == 1, "sample group is one new token per sequence"
    depth = norm_mix.shape[0]
    H = MLA_HEADS
    past_len = page_table.shape[1] * cache_latent.shape[2]

    hp = x_prompt.reshape(B * S, D)
    hs = x_sample.reshape(BS, D)
    TM = 512
    row = lambda v: v.reshape(1, -1).astype(F32)

    pos_p = jnp.arange(S, dtype=F32)
    pos_s = jnp.broadcast_to(jnp.arange(T, dtype=F32) + past_len, (BS,))
    rope_p = _rope_tables(pos_p)
    rope_s = _rope_tables(pos_s)

    assert depth == 2, "schedule below is written for one mLSTM layer followed by one MLA layer"
    g_fin = row(norm_final)

    n_gates = 2 * ML_HEADS
    ml_w_in = _pad_cols(mlstm_w_in[0], mlstm_w_in.shape[2] - n_gates + LANES).astype(BF16)
    ml_bias = _pad_cols(mlstm_b_gates[0].reshape(1, -1), LANES).astype(F32)
    ml_g_head = row(mlstm_g_head[0])
    ml_w_out = mlstm_w_out[0].astype(BF16)
    mla_in = _pad_cols(mla_w_in[0], Q_LORA + KV_LORA + LANES).astype(BF16)
    wq = mla_w_uq[0].reshape(Q_LORA, H, QK_NOPE + QK_ROPE)
    wq_nope = wq[:, :, :QK_NOPE].reshape(Q_LORA, H * QK_NOPE)
    wq_rope = jnp.pad(wq[:, :, QK_NOPE:], ((0, 0), (0, 0), (0, LANES - QK_ROPE))).reshape(Q_LORA, H * LANES)
    w_uq = jnp.concatenate([wq_nope, wq_rope], axis=1).astype(BF16)
    w_ukt = jnp.transpose(mla_w_uk[0].reshape(KV_LORA, H, QK_NOPE), (1, 2, 0)).astype(BF16)
    w_uv = jnp.transpose(mla_w_uv[0].reshape(KV_LORA, H, V_DIM), (1, 0, 2)).astype(BF16)
    w_uvt = jnp.transpose(mla_w_uv[0].reshape(KV_LORA, H, V_DIM), (1, 2, 0)).astype(BF16)
    w_o = mla_w_o[0].astype(BF16)
    g_q, g_kv = row(mla_g_q[0]), row(mla_g_kv[0])
    mla_weights = (mla_in, g_q, g_kv, w_uq, w_ukt)
    w_gu_all, w_d_all = ffn_w_gate_up.astype(BF16), ffn_w_down.astype(BF16)
    ffn = [(row(norm_ffn[l]), w_gu_all, w_d_all, l) for l in range(depth)]
    g_mix = [row(norm_mix[l]) for l in range(depth)]
    cache_k_rope_t = jnp.swapaxes(cache_k_rope, 2, 3)

    xw_s = _norm_matmul(hs, g_mix[0], ml_w_in, BS)
    a_s, C_s, n_s, m_s = _mlstm_sample(xw_s, mlstm_b_gates[0].astype(F32), ml_g_head, state_mlstm_C[0],
                                       state_mlstm_n[0], state_mlstm_m[0], TB=16)
    hs = _mixer_out_ffn(hs, a_s, ml_w_out, *ffn[0], g_fin, BS, final_norm=False)
    q_s, _, lat_s, kr_s = _mla_project(hs, g_mix[1], *mla_weights, *rope_s, tm=BS, table_blocks=1, vt_tile=0)

    xw_p = _norm_matmul(hp, g_mix[0], ml_w_in, TM)
    o_s, a_p, C_p, n_p, m_p = _attention_sample_with_mlstm_prompt(
        q_s.reshape(BS, H, Q_SLOT), lat_s, kr_s, cache_latent, cache_k_rope_t, page_table, layer=0, G=64,
        xw=xw_p, bias=ml_bias, g_head=ml_g_head, B=B, S=S, L=256)
    hp = _mixer_out_ffn(hp, a_p, ml_w_out, *ffn[0], g_fin, 2 * TM, final_norm=False)

    T_ATT = 512
    q_p, kcat_p, lat_p, kr_p, vt_p = _mla_project(hp, g_mix[1], *mla_weights, *rope_p,
                                                  tm=TM, table_blocks=S // TM, vt_tile=T_ATT)
    a_p = _attention_prompt(q_p, kcat_p, vt_p, w_uvt, B, S, T=T_ATT)
    hp = _mixer_out_ffn(hp, a_p, w_o, *ffn[1], g_fin, 2 * TM, final_norm=True)

    a_s = _value_up(o_s.reshape(BS, H * KV_LORA), w_uv)
    hs = _mixer_out_ffn(hs, a_s, w_o, *ffn[1], g_fin, BS, final_norm=True)

    return (hp.reshape(B, S, D), hs.reshape(BS, T, D),
            C_p[None], n_p[None], m_p.reshape(1, B, -1),
            C_s[None], n_s[None], m_s.reshape(1, BS, -1),
            lat_p.reshape(1, B, S, KV_LORA), kr_p.reshape(1, B, S, QK_ROPE),
            lat_s.reshape(1, BS, T, KV_LORA), kr_s.reshape(1, BS, T, QK_ROPE))
```

```python
import functools

import jax
import jax.numpy as jnp
from jax import lax
from jax.experimental import pallas as pl
from jax.experimental.pallas import tpu as pltpu

F32 = jnp.float32
BF16 = jnp.bfloat16
EPS = 1e-6
ROPE_THETA = 10000.0

V7X_VMEM_BYTES = 64 * 1024 * 1024
LANES = 128
VMEM_LIMIT_BYTES = V7X_VMEM_BYTES - 8 * 1024 * 1024

ML_HEADS = 8
MLA_HEADS = 8
Q_LORA = 384
KV_LORA = 256
QK_NOPE = 128
QK_ROPE = 64
V_DIM = 128
Q_SLOT = KV_LORA + LANES
MLA_SCALE = (QK_NOPE + QK_ROPE) ** -0.5

NT_DIMS = (((1,), (1,)), ((), ()))


def _params(n_grid_axes):
    return pltpu.CompilerParams(
        dimension_semantics=("arbitrary",) * n_grid_axes,
        vmem_limit_bytes=VMEM_LIMIT_BYTES,
    )


def _rms(x, g):
    return x * lax.rsqrt(jnp.mean(x * x, axis=-1, keepdims=True) + EPS) * g


def _log_sigmoid(x):
    return jnp.minimum(x, 0.0) - jnp.log1p(jnp.exp(-jnp.abs(x)))


def _resident(shape):
    nd = len(shape)
    return pl.BlockSpec(shape, lambda *_: (0,) * nd)


def _norm_matmul_kernel(x_ref, g_ref, w_ref, lo_ref, hi_ref):
    xn = _rms(x_ref[...], g_ref[...]).astype(BF16)
    y = jnp.dot(xn, w_ref[...], preferred_element_type=F32)
    n_lo = lo_ref.shape[1]
    lo_ref[...] = y[:, :n_lo].astype(BF16)
    hi_ref[...] = y[:, n_lo:]


def _norm_matmul(x, g, w, tm, n_lo):
    M, D = x.shape
    N = w.shape[1]
    return pl.pallas_call(
        _norm_matmul_kernel,
        out_shape=(jax.ShapeDtypeStruct((M, n_lo), BF16), jax.ShapeDtypeStruct((M, N - n_lo), F32)),
        grid=(M // tm,),
        in_specs=[pl.BlockSpec((tm, D), lambda i: (i, 0)), _resident((1, D)), _resident((D, N))],
        out_specs=(pl.BlockSpec((tm, n_lo), lambda i: (i, 0)), pl.BlockSpec((tm, N - n_lo), lambda i: (i, 0))),
        compiler_params=_params(1),
        name="norm_matmul",
    )(x, g, w)


def _mlstm_reset(caug_ref, mst_ref):
    caug_ref[...] = jnp.zeros_like(caug_ref)
    mst_ref[...] = jnp.zeros_like(mst_ref)


def _mlstm_chunk_phases(q_ref, k_ref, v_ref, o_ref, gt_ref, bias_ref, gh_ref, hg_ref, caug_ref, mst_ref,
                        *, L, H, DK, DV):
    gates = gt_ref[...] + bias_ref[...]
    lane = lax.broadcasted_iota(jnp.int32, gates.shape, 1)
    G = jnp.where(lane < H, gates, _log_sigmoid(gates))
    row = lax.broadcasted_iota(jnp.int32, (L, L), 0)
    col = lax.broadcasted_iota(jnp.int32, (L, L), 1)
    causal = col <= row
    cs = jnp.dot(causal.astype(F32), G, precision=lax.Precision.HIGHEST,
                 preferred_element_type=F32)
    GT = G.T
    csT = cs.T

    kT = (k_ref[...].astype(F32) * (DK ** -0.5)).T
    ones = jnp.ones((L, DV), BF16)
    heads = range(H)


    b_col = [cs[:, H + h:H + h + 1] for h in heads]
    b_row = [csT[H + h:H + h + 1, :] for h in heads]
    i_row = [GT[h:h + 1, :] for h in heads]
    r_mat = [jnp.where(causal, i_row[h] - b_row[h], -jnp.inf) for h in heads]
    r_max = [jnp.max(r_mat[h], axis=1, keepdims=True) for h in heads]
    yield

    m_prev = [mst_ref[h:h + 1, 0:1] for h in heads]
    log_inter = [b_col[h] + m_prev[h] for h in heads]
    m_t = [jnp.maximum(log_inter[h], b_col[h] + r_max[h]) for h in heads]
    w_inter = [jnp.exp(log_inter[h] - m_t[h]) for h in heads]
    W = [jnp.exp(r_mat[h] + (b_col[h] - m_t[h])) for h in heads]

    lane_pair = lax.broadcasted_iota(jnp.int32, (L, 2 * DK), 1)
    S, inter, caug_prev = [None] * H, [None] * H, [None] * H
    for p in range(H // 2):
        q_pair = q_ref[:, p * 2 * DK:(p + 1) * 2 * DK]
        q_lo = jnp.where(lane_pair < DK, q_pair, 0.0)
        q_hi = jnp.where(lane_pair < DK, 0.0, q_pair)
        qm2 = jnp.concatenate([q_lo, q_hi], axis=0).astype(BF16)
        k_pair = (k_ref[:, p * 2 * DK:(p + 1) * 2 * DK] * (DK ** -0.5)).astype(BF16)
        s2 = lax.dot_general(qm2, k_pair, NT_DIMS, preferred_element_type=F32)
        caug_pair = caug_ref[p]
        i2 = jnp.dot(qm2, caug_pair.astype(BF16), preferred_element_type=F32)
        for half in range(2):
            h = 2 * p + half
            S[h] = s2[half * L:(half + 1) * L] * W[h]
            inter[h] = i2[half * L:(half + 1) * L]
            caug_prev[h] = caug_pair[half * DK:(half + 1) * DK, :]
    yield

    m_new = [m_t[h][L - 1:L, :] for h in heads]
    b_last = [cs[L - 1:L, H + h:H + h + 1] for h in heads]
    R = []
    for h in heads:
        w_s_row = jnp.exp(b_last[h] - b_row[h] + i_row[h] - m_new[h])
        kwT = (kT[h * DK:(h + 1) * DK, :] * w_s_row).astype(BF16)
        vaug = jnp.concatenate([v_ref[:, h * DV:(h + 1) * DV], ones], axis=1)
        lhs = jnp.concatenate([S[h].astype(BF16), kwT], axis=0)
        R.append(jnp.dot(lhs, vaug, preferred_element_type=F32))
    yield

    hh = []
    for h in heads:
        numden = w_inter[h] * inter[h] + R[h][:L]
        num, den = numden[:, :DV], numden[:, DV:]
        hh.append(num / jnp.maximum(jnp.abs(den), jnp.exp(-m_t[h])))
    ms = [jnp.mean(hh[h] * hh[h], axis=-1, keepdims=True) for h in heads]
    for h in heads:
        hn = hh[h] * lax.rsqrt(ms[h] + EPS)
        og = jax.nn.sigmoid(o_ref[:, h * DV:(h + 1) * DV])
        hg_ref[:, h * DV:(h + 1) * DV] = (og * (hn * gh_ref[:, h * DV:(h + 1) * DV])).astype(BF16)

    for h in heads:
        p, r0 = h // 2, (h % 2) * DK
        w_c = jnp.exp(b_last[h] + m_prev[h] - m_new[h])
        caug_ref[p, r0:r0 + DK, :] = w_c * caug_prev[h] + R[h][L:]
        mst_ref[h:h + 1, :] = jnp.broadcast_to(m_new[h], (1, LANES))


def _mlstm_write_state(caug_ref, mst_ref, c_out_ref, n_out_ref, m_out_ref, *, H, DK, DV):
    pick0 = (lax.broadcasted_iota(jnp.int32, (8, DV), 1) == 0).astype(F32)
    for h in range(H):
        p, r0 = h // 2, (h % 2) * DK
        ca = caug_ref[p, r0:r0 + DK, :]
        c_out_ref[0, h] = ca[:, :DV]
        n_rows = lax.dot_general(pick0, ca[:, DV:], NT_DIMS, precision=lax.Precision.HIGHEST,
                                 preferred_element_type=F32)
        n_out_ref[0, h:h + 1, :] = n_rows[0:1, :]
        m_out_ref[0, :, h:h + 1] = mst_ref[h:h + 1, 0:1]


def _mlstm_step_kernel(q_ref, k_ref, v_ref, o_ref, gi_ref, gf_ref, bi_ref, bf_ref, gh_ref,
                       c0_ref, n0_ref, m0_ref,
                       hg_ref, c_out_ref, n_out_ref, m_out_ref, *, H, DK, DV, TB):
    RB = TB * H
    i_pre = gi_ref[...] + bi_ref[...]
    log_f = _log_sigmoid(gf_ref[...] + bf_ref[...])
    log_inter = log_f + m0_ref[...]
    m_t = jnp.maximum(log_inter, i_pre)
    w_inter = jnp.exp(log_inter - m_t)
    w_intra = jnp.exp(i_pre - m_t)

    q = q_ref[...].astype(F32)
    k = k_ref[...].astype(F32) * (DK ** -0.5)
    v = v_ref[...].astype(F32)
    n_prev = n0_ref[...]
    s = jnp.sum(q * k, axis=1, keepdims=True) * w_intra
    den = w_inter * jnp.sum(q * n_prev, axis=1, keepdims=True) + s

    def block_diag(x):
        wide = jnp.concatenate([x] * H, axis=1)
        head_of_lane = lax.broadcasted_iota(jnp.int32, wide.shape, 1) // DK
        head_of_row = lax.broadcasted_iota(jnp.int32, wide.shape, 0) % H
        return jnp.where(head_of_lane == head_of_row, wide, 0.0)

    q_bd = block_diag(q)
    kw_bd = block_diag(k * w_intra[:, :DK])
    qc = jnp.concatenate(
        [jnp.dot(q_bd[t * H:(t + 1) * H, :], c0_ref[t], preferred_element_type=F32) for t in range(TB)],
        axis=0)

    num = w_inter * qc + s * v
    hh = num / jnp.maximum(jnp.abs(den), jnp.exp(-m_t))
    hn = hh * lax.rsqrt(jnp.mean(hh * hh, axis=-1, keepdims=True) + EPS)
    hg_ref[...] = (jax.nn.sigmoid(o_ref[...]) * (hn * gh_ref[...])).astype(BF16)
    n_out_ref[...] = w_inter[:, :DK] * n_prev + w_intra[:, :DK] * k
    m_out_ref[...] = m_t

    for t in range(TB):
        rows = slice(t * H, (t + 1) * H)
        d_c = lax.dot_general(kw_bd[rows, :], v[rows, :], (((0,), (0,)), ((), ())),
                              preferred_element_type=F32)
        for h in range(H):
            blk = slice(h * DK, (h + 1) * DK)
            c_out_ref[t, blk, :] = w_inter[t * H + h:t * H + h + 1, :] * c0_ref[t, blk, :] + d_c[blk, :]


def _mlstm_sample(qkv, og, b_gates, g_head, c0, n0, m0, TB):
    H, DK, DV = ML_HEADS, 64, 128
    B = qkv.shape[0]
    R, RB = B * H, TB * H
    qk_w, v_w = H * DK, H * DV
    lanes = lambda x: jnp.broadcast_to(x.reshape(-1, 1), (x.size, LANES))
    per_block = lambda x: jnp.tile(x, (TB, 1))
    q = qkv[:, :qk_w].reshape(R, DK)
    k = qkv[:, qk_w:2 * qk_w].reshape(R, DK)
    v = qkv[:, 2 * qk_w:].reshape(R, DV)
    o = og[:, :v_w].reshape(R, DV)
    operands = (
        q, k, v, o,
        lanes(og[:, v_w:v_w + H]), lanes(og[:, v_w + H:v_w + 2 * H]),
        per_block(lanes(b_gates[:H])), per_block(lanes(b_gates[H:])),
        per_block(g_head.reshape(H, DV)),
        c0.reshape(B, H * DK, DV), n0.reshape(R, DK), lanes(m0),
    )
    row_blk = lambda n: pl.BlockSpec((RB, n), lambda b: (b, 0))
    state_blk = pl.BlockSpec((TB, H * DK, DV), lambda b: (b, 0, 0))
    kern = functools.partial(_mlstm_step_kernel, H=H, DK=DK, DV=DV, TB=TB)
    hg, c_new, n_new, m_new = pl.pallas_call(
        kern,
        out_shape=(
            jax.ShapeDtypeStruct((R, DV), BF16),
            jax.ShapeDtypeStruct((B, H * DK, DV), F32),
            jax.ShapeDtypeStruct((R, DK), F32),
            jax.ShapeDtypeStruct((R, LANES), F32),
        ),
        grid=(B // TB,),
        in_specs=[
            row_blk(DK), row_blk(DK), row_blk(DV), row_blk(DV),
            row_blk(LANES), row_blk(LANES),
            _resident((RB, LANES)), _resident((RB, LANES)), _resident((RB, DV)),
            state_blk, row_blk(DK), row_blk(LANES),
        ],
        out_specs=(row_blk(DV), state_blk, row_blk(DK), row_blk(LANES)),
        compiler_params=_params(1),
        name="mlstm_sample",
    )(*operands)
    return (hg.reshape(B, v_w), c_new.reshape(B, H, DK, DV), n_new.reshape(B, H, DK),
            m_new[:, 0].reshape(B, H))


def _ffn_kernel(h_ref, a_ref, wa_ref, gn_ref, wgu_ref, wd_ref, gf_ref, o_ref, *, d_ff, tf, final_norm, a_tile):
    if a_tile:
        proj = jnp.concatenate(
            [lax.dot_general(a_ref[t], wa_ref[...], (((0,), (0,)), ((), ())), preferred_element_type=F32)
             for t in range(a_ref.shape[0])], axis=0)
    else:
        proj = jnp.dot(a_ref[...], wa_ref[...], preferred_element_type=F32)
    h1 = h_ref[...] + proj
    xn = _rms(h1, gn_ref[...]).astype(BF16)
    acc = h1
    for c in range(d_ff // tf):
        g = jnp.dot(xn, wgu_ref[:, c * tf:(c + 1) * tf], preferred_element_type=F32)
        u = jnp.dot(xn, wgu_ref[:, d_ff + c * tf:d_ff + (c + 1) * tf], preferred_element_type=F32)
        act = (g * jax.nn.sigmoid(g) * u).astype(BF16)
        acc = acc + jnp.dot(act, wd_ref[c * tf:(c + 1) * tf, :], preferred_element_type=F32)
    if final_norm:
        acc = _rms(acc, gf_ref[...])
    o_ref[...] = acc


def _mixer_out_ffn(h, a, w_a, g_ffn, w_gu, w_d, layer, g_final, tm, final_norm):
    M, D = h.shape
    KA = w_a.shape[0]
    d_ff = w_d.shape[1]
    a_tile = a.shape[2] if a.ndim == 3 else 0
    kern = functools.partial(_ffn_kernel, d_ff=d_ff, tf=256, final_norm=final_norm, a_tile=a_tile)
    single = pl.Buffered(1)
    a_spec = (pl.BlockSpec((tm // a_tile, KA, a_tile), lambda i: (i, 0, 0)) if a_tile
              else pl.BlockSpec((tm, KA), lambda i: (i, 0)))
    return pl.pallas_call(
        kern,
        out_shape=jax.ShapeDtypeStruct((M, D), F32),
        grid=(M // tm,),
        in_specs=[
            pl.BlockSpec((tm, D), lambda i: (i, 0)),
            a_spec,
            pl.BlockSpec((KA, D), lambda i: (0, 0), pipeline_mode=single),
            _resident((1, D)),
            pl.BlockSpec((None, D, 2 * d_ff), lambda i: (layer, 0, 0), pipeline_mode=single),
            pl.BlockSpec((None, d_ff, D), lambda i: (layer, 0, 0), pipeline_mode=single),
            _resident((1, D)),
        ],
        out_specs=pl.BlockSpec((tm, D), lambda i: (i, 0)),
        compiler_params=_params(1),
        name="mixer_out_ffn",
    )(h, a, w_a, g_ffn, w_gu, w_d, g_final)


def _rope_slot(x, cos, sin_lo, sin_hi):
    return x * cos + pltpu.roll(x, 96, 1) * sin_lo + pltpu.roll(x, 32, 1) * sin_hi


def _mla_proj_kernel(h_ref, gn_ref, win_ref, gq_ref, gkv_ref, wuq_ref, wukt_ref,
                     cos_ref, sinlo_ref, sinhi_ref,
                     q_ref, kcat_ref, lat_ref, kr_ref, *maybe_vt_ref, H, vt_tile):
    xn = _rms(h_ref[...], gn_ref[...]).astype(BF16)
    t = jnp.dot(xn, win_ref[...], preferred_element_type=F32)
    c_q = t[:, :Q_LORA]
    c_kv = t[:, Q_LORA:Q_LORA + KV_LORA]
    k_slot = t[:, Q_LORA + KV_LORA:]
    cos, sin_lo, sin_hi = cos_ref[...], sinlo_ref[...], sinhi_ref[...]

    lat = _rms(c_kv, gkv_ref[...])
    k_rot = _rope_slot(k_slot, cos, sin_lo, sin_hi)
    lat_ref[...] = lat
    kr_ref[...] = k_rot[:, :QK_ROPE]
    kcat_ref[:, :KV_LORA] = lat.astype(BF16)
    kcat_ref[:, KV_LORA:] = k_rot.astype(BF16)
    if vt_tile:
        (vt_ref,) = maybe_vt_ref
        for t in range(lat.shape[0] // vt_tile):
            vt_ref[t] = lat[t * vt_tile:(t + 1) * vt_tile, :].T.astype(BF16)

    cqn = _rms(c_q, gq_ref[...]).astype(BF16)
    tm = cqn.shape[0]

    def store_q(h, lanes, val):
        if vt_tile:
            sub = vt_tile // 2
            for t in range(tm // vt_tile):
                for half in range(2):
                    src = slice(t * vt_tile + half * sub, t * vt_tile + (half + 1) * sub)
                    dst = slice((half * H + h) * sub, (half * H + h + 1) * sub)
                    q_ref[t, dst, lanes] = val[src].astype(BF16)
        else:
            q_ref[:, slice(h * Q_SLOT + lanes.start, h * Q_SLOT + lanes.stop)] = val.astype(BF16)

    q_rope = jnp.dot(cqn, wuq_ref[:, H * QK_NOPE:], preferred_element_type=F32)
    for h in range(H):
        store_q(h, slice(KV_LORA, Q_SLOT), _rope_slot(q_rope[:, h * LANES:(h + 1) * LANES], cos, sin_lo, sin_hi))
    q_nope = jnp.dot(cqn, wuq_ref[:, :H * QK_NOPE], preferred_element_type=F32)
    for h in range(H):
        q_lat = jnp.dot(q_nope[:, h * QK_NOPE:(h + 1) * QK_NOPE].astype(BF16), wukt_ref[h],
                        preferred_element_type=F32)
        store_q(h, slice(0, KV_LORA), q_lat)


def _mla_project(h, g_norm, w_in, g_q, g_kv, w_uq, w_ukt, cos, sin_lo, sin_hi, tm, table_blocks, vt_tile):
    M, D = h.shape
    H = MLA_HEADS
    kern = functools.partial(_mla_proj_kernel, H=H, vt_tile=vt_tile)
    table = pl.BlockSpec((tm, LANES), lambda i: (i % table_blocks, 0))
    out_shape = [
        jax.ShapeDtypeStruct((M, H * Q_SLOT), BF16),
        jax.ShapeDtypeStruct((M, Q_SLOT), BF16),
        jax.ShapeDtypeStruct((M, KV_LORA), F32),
        jax.ShapeDtypeStruct((M, QK_ROPE), F32),
    ]
    out_specs = [
        pl.BlockSpec((tm, H * Q_SLOT), lambda i: (i, 0)),
        pl.BlockSpec((tm, Q_SLOT), lambda i: (i, 0)),
        pl.BlockSpec((tm, KV_LORA), lambda i: (i, 0)),
        pl.BlockSpec((tm, QK_ROPE), lambda i: (i, 0)),
    ]
    if vt_tile:
        out_shape[0] = jax.ShapeDtypeStruct((M // vt_tile, H * vt_tile, Q_SLOT), BF16)
        out_specs[0] = pl.BlockSpec((tm // vt_tile, H * vt_tile, Q_SLOT), lambda i: (i, 0, 0))
        out_shape.append(jax.ShapeDtypeStruct((M // vt_tile, KV_LORA, vt_tile), BF16))
        out_specs.append(pl.BlockSpec((tm // vt_tile, KV_LORA, vt_tile), lambda i: (i, 0, 0)))
    return pl.pallas_call(
        kern,
        out_shape=tuple(out_shape),
        grid=(M // tm,),
        in_specs=[
            pl.BlockSpec((tm, D), lambda i: (i, 0)),
            _resident((1, D)),
            _resident(w_in.shape),
            _resident((1, Q_LORA)),
            _resident((1, KV_LORA)),
            _resident(w_uq.shape),
            _resident(w_ukt.shape),
            table, table, table,
        ],
        out_specs=tuple(out_specs),
        compiler_params=_params(1),
        name="mla_project",
    )(h, g_norm, w_in, g_q, g_kv, w_uq, w_ukt, cos, sin_lo, sin_hi)


def _attn_kernel(q_ref, k_ref, vt_ref, wuvt_ref, o_ref, m_ref, l_ref, acc_ref, sa_ref, sb_ref, *, H, T):
    i = pl.program_id(1)

    def scores(j):
        kj = k_ref[0, pl.ds(pl.multiple_of(j * T, T), T), :]
        return lax.dot_general(kj, q_ref[0], NT_DIMS, preferred_element_type=F32) * MLA_SCALE

    def update(j, st):
        vtj = vt_ref[j]
        m_prev = m_ref[...]
        m_new = jnp.maximum(m_prev, jnp.max(st, axis=0, keepdims=True))
        alpha = jnp.exp(m_prev - m_new)
        p = jnp.exp(st - m_new)
        l_ref[...] = alpha * l_ref[...] + jnp.sum(p, axis=0, keepdims=True)
        acc_ref[...] = alpha * acc_ref[...] + jnp.dot(vtj, p.astype(BF16), preferred_element_type=F32)
        m_ref[...] = m_new

    SUB, C2 = T // 2, H * T // 2
    base = pl.multiple_of(i * T, T)
    k_a = k_ref[0, pl.ds(base, SUB), :]
    k_b = k_ref[0, pl.ds(base + SUB, SUB), :]
    vt_i = vt_ref[i]
    vt_a, vt_b = vt_i[:, :SUB], vt_i[:, SUB:]
    s_a = lax.dot_general(k_a, q_ref[0], NT_DIMS, preferred_element_type=F32) * MLA_SCALE
    s_b = lax.dot_general(k_b, q_ref[0, C2:, :], NT_DIMS, preferred_element_type=F32) * MLA_SCALE
    key = lax.broadcasted_iota(jnp.int32, (SUB, C2), 0)
    qry = lax.broadcasted_iota(jnp.int32, (SUB, C2), 1) & (SUB - 1)
    causal = key <= qry
    s_a0 = jnp.where(causal, s_a[:, :C2], -jnp.inf)
    s_a1 = s_a[:, C2:]
    s_b1 = jnp.where(causal, s_b, -jnp.inf)
    m_0 = jnp.max(s_a0, axis=0, keepdims=True)
    m_1 = jnp.maximum(jnp.max(s_a1, axis=0, keepdims=True), jnp.max(s_b1, axis=0, keepdims=True))
    p_a0, p_a1, p_b1 = jnp.exp(s_a0 - m_0), jnp.exp(s_a1 - m_1), jnp.exp(s_b1 - m_1)
    m_ref[:, :C2] = m_0
    m_ref[:, C2:] = m_1
    l_ref[:, :C2] = jnp.sum(p_a0, axis=0, keepdims=True)
    l_ref[:, C2:] = jnp.sum(p_a1, axis=0, keepdims=True) + jnp.sum(p_b1, axis=0, keepdims=True)
    acc_ref[:, :C2] = jnp.dot(vt_a, p_a0.astype(BF16), preferred_element_type=F32)
    acc_ref[:, C2:] = (jnp.dot(vt_a, p_a1.astype(BF16), preferred_element_type=F32)
                       + jnp.dot(vt_b, p_b1.astype(BF16), preferred_element_type=F32))

    @pl.when(i > 0)
    def _():
        sa_ref[...] = scores(0)

    def pair(jj, carry):
        j = 2 * jj
        sb_ref[...] = scores(j + 1)
        update(j, sa_ref[...])
        sa_ref[...] = scores(jnp.minimum(j + 2, i - 1))
        update(j + 1, sb_ref[...])
        return carry

    lax.fori_loop(0, i // 2, pair, 0)

    @pl.when(i % 2 == 1)
    def _():
        update(i - 1, sa_ref[...])

    o_t = (acc_ref[...] / l_ref[...]).astype(BF16)
    for half in range(2):
        for h in range(H):
            cols = slice((half * H + h) * SUB, (half * H + h + 1) * SUB)
            v_t = jnp.dot(wuvt_ref[h], o_t[:, cols], preferred_element_type=F32)
            o_ref[0, h * V_DIM:(h + 1) * V_DIM, half * SUB:(half + 1) * SUB] = v_t.astype(BF16)


def _attention_prompt(q, kcat, vt, w_uvt, B, S, T):
    H = MLA_HEADS
    NQ = S // T
    kern = functools.partial(_attn_kernel, H=H, T=T)
    return pl.pallas_call(
        kern,
        out_shape=jax.ShapeDtypeStruct((B * NQ, H * V_DIM, T), BF16),
        grid=(B, NQ),
        in_specs=[
            pl.BlockSpec((1, H * T, Q_SLOT), lambda b, i: (b * NQ + i, 0, 0)),
            pl.BlockSpec((1, S, Q_SLOT), lambda b, i: (b, 0, 0)),
            pl.BlockSpec((NQ, KV_LORA, T), lambda b, i: (b, 0, 0)),
            _resident(w_uvt.shape),
        ],
        out_specs=pl.BlockSpec((1, H * V_DIM, T), lambda b, i: (b * NQ + i, 0, 0)),
        scratch_shapes=[
            pltpu.VMEM((1, H * T), F32),
            pltpu.VMEM((1, H * T), F32),
            pltpu.VMEM((KV_LORA, H * T), F32),
            pltpu.VMEM((T, H * T), F32),
            pltpu.VMEM((T, H * T), F32),
        ],
        compiler_params=_params(2),
        name="attention_prompt",
    )(q, kcat.reshape(B, S, Q_SLOT), vt, w_uvt)


def _decode_mlstm_kernel(pt_ref,
                         q_ref, cn_ref, krn_ref, lat_hbm, krt_hbm,
                         mq_ref, mk_ref, mv_ref, mo_ref, gt_ref, bias_ref, gh_ref,
                         o_ref, hg_ref, c_out_ref, n_out_ref, m_out_ref,
                         lat_buf, kr_buf, sem, caug_ref, mst_ref,
                         *, layer, G, P, NCH, RING, STREAMS, L, H_ML, DK, DV, NC):
    s = pl.program_id(0)
    ns = pl.num_programs(0)
    total = ns * NCH
    ml = dict(L=L, H=H_ML, DK=DK, DV=DV)

    def page_copies(c):
        slot = lax.rem(c, RING)
        cw = jnp.where(c >= total, c - total, c)
        bb, jj = lax.div(cw, NCH), lax.rem(cw, NCH)
        copies = []
        for g in range(G):
            page = pt_ref[bb, jj * G + g]
            copies.append(pltpu.make_async_copy(
                lat_hbm.at[layer, page], lat_buf.at[slot, pl.ds(g * P, P), :], sem.at[0, slot]))
            copies.append(pltpu.make_async_copy(
                krt_hbm.at[layer, page], kr_buf.at[slot, :, pl.ds(g * P, P)], sem.at[1, slot]))
        return copies

    def start(c):
        for cp in page_copies(c):
            cp.start()

    def wait(c):
        for cp in page_copies(c):
            cp.wait()

    @pl.when(s == 0)
    def _():
        for c in range(RING - 1):
            start(jnp.int32(c))

    ml_chunk = lax.rem(s, NC)

    @pl.when(ml_chunk == 0)
    def _():
        _mlstm_reset(caug_ref, mst_ref)

    q = q_ref[0].astype(F32)
    q_lat = q[:, :KV_LORA]
    q_rope = q[:, KV_LORA:KV_LORA + QK_ROPE]
    c_new = cn_ref[0]
    kr_new = krn_ref[0]

    s_new = (jnp.sum(q_lat * c_new, axis=1, keepdims=True)
             + jnp.sum(q_rope * kr_new, axis=1, keepdims=True)) * MLA_SCALE
    n_heads = q.shape[0]
    m_run = [s_new] + [jnp.full_like(s_new, -jnp.inf)] * (STREAMS - 1)
    l_run = [jnp.ones_like(s_new)] + [jnp.zeros_like(s_new)] * (STREAMS - 1)
    acc = ([jnp.broadcast_to(c_new, (n_heads, KV_LORA)).astype(F32)]
           + [jnp.zeros((n_heads, KV_LORA), F32)] * (STREAMS - 1))
    W = (G * P) // STREAMS

    ml_pieces = _mlstm_chunk_phases(mq_ref, mk_ref, mv_ref, mo_ref, gt_ref, bias_ref, gh_ref, hg_ref,
                                    caug_ref, mst_ref, **ml)
    assert NCH >= 2
    for j in range(NCH):
        c = s * NCH + j
        slot = lax.rem(c, RING)
        wait(c)
        kl = [lat_buf[slot, i * W:(i + 1) * W, :] for i in range(STREAMS)]
        sc = [(lax.dot_general(q_lat, kl[i], NT_DIMS, preferred_element_type=F32)
               + jnp.dot(q_rope, kr_buf[slot, :, i * W:(i + 1) * W], preferred_element_type=F32)) * MLA_SCALE
              for i in range(STREAMS)]
        if j in (0, NCH - 1):
            next(ml_pieces)
        for i in range(STREAMS):
            m_new = jnp.maximum(m_run[i], jnp.max(sc[i], axis=1, keepdims=True))
            alpha = jnp.exp(m_run[i] - m_new)
            p = jnp.exp(sc[i] - m_new)
            m_run[i] = m_new
            l_run[i] = alpha * l_run[i] + jnp.sum(p, axis=1, keepdims=True)
            acc[i] = alpha * acc[i] + jnp.dot(p, kl[i], preferred_element_type=F32)
        if j in (0, NCH - 1):
            next(ml_pieces, None)
        start(c + (RING - 1))

    m_all = functools.reduce(jnp.maximum, m_run)
    scale = [jnp.exp(m_i - m_all) for m_i in m_run]
    l_all = sum(l_i * w_i for l_i, w_i in zip(l_run, scale))
    acc_all = sum(a_i * w_i for a_i, w_i in zip(acc, scale))
    o_ref[0] = acc_all / l_all

    @pl.when(ml_chunk == NC - 1)
    def _():
        _mlstm_write_state(caug_ref, mst_ref, c_out_ref, n_out_ref, m_out_ref, H=H_ML, DK=DK, DV=DV)

    @pl.when(s == ns - 1)
    def _():
        for c in range(RING - 1):
            wait(total + c)


def _attention_sample_with_mlstm_prompt(q3, c_new, kr_new, cache_latent, cache_k_rope_t, page_table, layer, G,
                                        qkv, og, bias, g_head, B, S, L):
    BS, H, _ = q3.shape
    n_pages = page_table.shape[1]
    P = cache_latent.shape[2]
    NCH = n_pages // G
    assert n_pages % G == 0
    RING = 3
    H_ML, DK, DV = ML_HEADS, 64, 128
    NC = S // L
    assert B * NC == BS, "one mLSTM chunk per decode sequence"
    qk_w, v_w = H_ML * DK, H_ML * DV
    gate_blk = v_w // LANES
    kern = functools.partial(_decode_mlstm_kernel, layer=layer, G=G, P=P, NCH=NCH, RING=RING, STREAMS=2,
                             L=L, H_ML=H_ML, DK=DK, DV=DV, NC=NC)
    grid_spec = pltpu.PrefetchScalarGridSpec(
        num_scalar_prefetch=1,
        grid=(BS,),
        in_specs=[
            pl.BlockSpec((1, H, Q_SLOT), lambda s, pt: (s, 0, 0)),
            pl.BlockSpec((1, 1, KV_LORA), lambda s, pt: (s, 0, 0)),
            pl.BlockSpec((1, 1, QK_ROPE), lambda s, pt: (s, 0, 0)),
            pl.BlockSpec(memory_space=pl.ANY),
            pl.BlockSpec(memory_space=pl.ANY),
            pl.BlockSpec((L, qk_w), lambda s, pt: (s, 0)),
            pl.BlockSpec((L, qk_w), lambda s, pt: (s, 1)),
            pl.BlockSpec((L, v_w), lambda s, pt: (s, 1)),
            pl.BlockSpec((L, v_w), lambda s, pt: (s, 0)),
            pl.BlockSpec((L, LANES), lambda s, pt: (s, gate_blk)),
            pl.BlockSpec((1, LANES), lambda s, pt: (0, 0)),
            pl.BlockSpec((1, v_w), lambda s, pt: (0, 0)),
        ],
        out_specs=(
            pl.BlockSpec((1, H, KV_LORA), lambda s, pt: (s, 0, 0)),
            pl.BlockSpec((L, v_w), lambda s, pt: (s, 0)),
            pl.BlockSpec((1, H_ML, DK, DV), lambda s, pt: (s // NC, 0, 0, 0)),
            pl.BlockSpec((1, H_ML, DK), lambda s, pt: (s // NC, 0, 0)),
            pl.BlockSpec((1, 1, H_ML), lambda s, pt: (s // NC, 0, 0)),
        ),
        scratch_shapes=[
            pltpu.VMEM((RING, G * P, KV_LORA), F32),
            pltpu.VMEM((RING, QK_ROPE, G * P), F32),
            pltpu.SemaphoreType.DMA((2, RING)),
            pltpu.VMEM((H_ML // 2, 2 * DK, 2 * DV), F32),
            pltpu.VMEM((H_ML, LANES), F32),
        ],
    )
    return pl.pallas_call(
        kern,
        out_shape=(
            jax.ShapeDtypeStruct((BS, H, KV_LORA), F32),
            jax.ShapeDtypeStruct((B * S, v_w), BF16),
            jax.ShapeDtypeStruct((B, H_ML, DK, DV), F32),
            jax.ShapeDtypeStruct((B, H_ML, DK), F32),
            jax.ShapeDtypeStruct((B, 1, H_ML), F32),
        ),
        grid_spec=grid_spec,
        compiler_params=_params(1),
        name="attention_sample_mlstm_prompt",
    )(page_table, q3, c_new.reshape(BS, 1, KV_LORA), kr_new.reshape(BS, 1, QK_ROPE),
      cache_latent, cache_k_rope_t, qkv, qkv, qkv, og, og, bias, g_head)


def _value_up_kernel(o_ref, wuv_ref, v_ref, *, H):
    for h in range(H):
        o_h = o_ref[:, h * KV_LORA:(h + 1) * KV_LORA].astype(BF16)
        v_ref[:, h * V_DIM:(h + 1) * V_DIM] = jnp.dot(
            o_h, wuv_ref[h], preferred_element_type=F32).astype(BF16)


def _value_up(o_lat, w_uv):
    M = o_lat.shape[0]
    H = MLA_HEADS
    return pl.pallas_call(
        functools.partial(_value_up_kernel, H=H),
        out_shape=jax.ShapeDtypeStruct((M, H * V_DIM), BF16),
        grid=(1,),
        in_specs=[_resident(o_lat.shape), _resident(w_uv.shape)],
        out_specs=_resident((M, H * V_DIM)),
        compiler_params=_params(1),
        name="value_up",
    )(o_lat, w_uv)


def _rope_tables(pos):
    half = QK_ROPE // 2
    inv = ROPE_THETA ** (-jnp.arange(0, QK_ROPE, 2, dtype=F32) / QK_ROPE)
    ang = pos[:, None] * inv[None, :]
    cos, sin = jnp.cos(ang), jnp.sin(ang)
    z = jnp.zeros_like(cos)
    cos_t = jnp.concatenate([cos, cos, z, z], axis=1)
    sin_lo = jnp.concatenate([-sin, z, z, z], axis=1)
    sin_hi = jnp.concatenate([z, sin, z, z], axis=1)
    return cos_t, sin_lo, sin_hi


def _pad_cols(w, n):
    return jnp.pad(w, ((0, 0), (0, n - w.shape[1])))


def kernel(x_prompt, x_sample, state_mlstm_C, state_mlstm_n, state_mlstm_m, cache_latent, cache_k_rope,
           page_table, norm_mix, norm_ffn, norm_final, mlstm_w_in, mlstm_b_gates, mlstm_g_head, mlstm_w_out,
           mla_w_in, mla_g_q, mla_g_kv, mla_w_uq, mla_w_uk, mla_w_uv, mla_w_o, ffn_w_gate_up, ffn_w_down):
    B, S, D = x_prompt.shape
    BS, T, _ = x_sample.shape
    assert T == 1, "sample group is one new token per sequence"
    depth = norm_mix.shape[0]
    H = MLA_HEADS
    past_len = page_table.shape[1] * cache_latent.shape[2]

    hp = x_prompt.reshape(B * S, D)
    hs = x_sample.reshape(BS, D)
    TM = 512
    row = lambda v: v.reshape(1, -1).astype(F32)

    pos_p = jnp.arange(S, dtype=F32)
    pos_s = jnp.broadcast_to(jnp.arange(T, dtype=F32) + past_len, (BS,))
    rope_p = _rope_tables(pos_p)
    rope_s = _rope_tables(pos_s)

    assert depth == 2, "schedule below is written for one mLSTM layer followed by one MLA layer"
    g_fin = row(norm_final)

    n_gates = 2 * ML_HEADS
    ml_w_in = _pad_cols(mlstm_w_in[0], mlstm_w_in.shape[2] - n_gates + LANES).astype(BF16)
    ml_bias = _pad_cols(mlstm_b_gates[0].reshape(1, -1), LANES).astype(F32)
    ml_g_head = row(mlstm_g_head[0])
    ml_w_out = mlstm_w_out[0].astype(BF16)
    mla_in = _pad_cols(mla_w_in[0], Q_LORA + KV_LORA + LANES).astype(BF16)
    wq = mla_w_uq[0].reshape(Q_LORA, H, QK_NOPE + QK_ROPE)
    wq_nope = wq[:, :, :QK_NOPE].reshape(Q_LORA, H * QK_NOPE)
    wq_rope = jnp.pad(wq[:, :, QK_NOPE:], ((0, 0), (0, 0), (0, LANES - QK_ROPE))).reshape(Q_LORA, H * LANES)
    w_uq = jnp.concatenate([wq_nope, wq_rope], axis=1).astype(BF16)
    w_ukt = jnp.transpose(mla_w_uk[0].reshape(KV_LORA, H, QK_NOPE), (1, 2, 0)).astype(BF16)
    w_uv = jnp.transpose(mla_w_uv[0].reshape(KV_LORA, H, V_DIM), (1, 0, 2)).astype(BF16)
    w_uvt = jnp.transpose(mla_w_uv[0].reshape(KV_LORA, H, V_DIM), (1, 2, 0)).astype(BF16)
    w_o = mla_w_o[0].astype(BF16)
    g_q, g_kv = row(mla_g_q[0]), row(mla_g_kv[0])
    mla_weights = (mla_in, g_q, g_kv, w_uq, w_ukt)
    w_gu_all, w_d_all = ffn_w_gate_up.astype(BF16), ffn_w_down.astype(BF16)
    ffn = [(row(norm_ffn[l]), w_gu_all, w_d_all, l) for l in range(depth)]
    g_mix = [row(norm_mix[l]) for l in range(depth)]
    cache_k_rope_t = jnp.swapaxes(cache_k_rope, 2, 3)

    n_qkv = 2 * ML_HEADS * 64 + ML_HEADS * 128
    qkv_s, og_s = _norm_matmul(hs, g_mix[0], ml_w_in, BS, n_qkv)
    a_s, C_s, n_s, m_s = _mlstm_sample(qkv_s, og_s, mlstm_b_gates[0].astype(F32), ml_g_head, state_mlstm_C[0],
                                       state_mlstm_n[0], state_mlstm_m[0], TB=16)
    hs = _mixer_out_ffn(hs, a_s, ml_w_out, *ffn[0], g_fin, BS, final_norm=False)
    q_s, _, lat_s, kr_s = _mla_project(hs, g_mix[1], *mla_weights, *rope_s, tm=BS, table_blocks=1, vt_tile=0)

    qkv_p, og_p = _norm_matmul(hp, g_mix[0], ml_w_in, 2 * TM, n_qkv)
    o_s, a_p, C_p, n_p, m_p = _attention_sample_with_mlstm_prompt(
        q_s.reshape(BS, H, Q_SLOT), lat_s, kr_s, cache_latent, cache_k_rope_t, page_table, layer=0, G=64,
        qkv=qkv_p, og=og_p, bias=ml_bias, g_head=ml_g_head, B=B, S=S, L=256)
    hp = _mixer_out_ffn(hp, a_p, ml_w_out, *ffn[0], g_fin, 2 * TM, final_norm=False)

    T_ATT = 512
    q_p, kcat_p, lat_p, kr_p, vt_p = _mla_project(hp, g_mix[1], *mla_weights, *rope_p,
                                                  tm=TM, table_blocks=S // TM, vt_tile=T_ATT)
    a_p = _attention_prompt(q_p, kcat_p, vt_p, w_uvt, B, S, T=T_ATT)
    hp = _mixer_out_ffn(hp, a_p, w_o, *ffn[1], g_fin, 2 * TM, final_norm=True)

    a_s = _value_up(o_s.reshape(BS, H * KV_LORA), w_uv)
    hs = _mixer_out_ffn(hs, a_s, w_o, *ffn[1], g_fin, BS, final_norm=True)

    return (hp.reshape(B, S, D), hs.reshape(BS, T, D),
            C_p[None], n_p[None], m_p.reshape(1, B, -1),
            C_s[None], n_s[None], m_s.reshape(1, BS, -1),
            lat_p.reshape(1, B, S, KV_LORA), kr_p.reshape(1, B, S, QK_ROPE),
            lat_s.reshape(1, BS, T, KV_LORA), kr_s.reshape(1, BS, T, QK_ROPE))
```

```python
import functools

import jax
import jax.numpy as jnp
from jax import lax
from jax.experimental import pallas as pl
from jax.experimental.pallas import tpu as pltpu

F32 = jnp.float32
BF16 = jnp.bfloat16
EPS = 1e-6
ROPE_THETA = 10000.0

V7X_VMEM_BYTES = 64 * 1024 * 1024
LANES = 128
VMEM_LIMIT_BYTES = V7X_VMEM_BYTES - 8 * 1024 * 1024

ML_HEADS = 8
MLA_HEADS = 8
Q_LORA = 384
KV_LORA = 256
QK_NOPE = 128
QK_ROPE = 64
V_DIM = 128
Q_SLOT = KV_LORA + LANES
MLA_SCALE = (QK_NOPE + QK_ROPE) ** -0.5

NT_DIMS = (((1,), (1,)), ((), ()))


def _params(n_grid_axes):
    return pltpu.CompilerParams(
        dimension_semantics=("arbitrary",) * n_grid_axes,
        vmem_limit_bytes=VMEM_LIMIT_BYTES,
    )


def _rms(x, g):
    return x * lax.rsqrt(jnp.mean(x * x, axis=-1, keepdims=True) + EPS) * g


def _log_sigmoid(x):
    return jnp.minimum(x, 0.0) - jnp.log1p(jnp.exp(-jnp.abs(x)))


def _resident(shape):
    nd = len(shape)
    return pl.BlockSpec(shape, lambda *_: (0,) * nd)


def _norm_matmul_kernel(x_ref, g_ref, w_ref, lo_ref, hi_ref):
    xn = _rms(x_ref[...], g_ref[...]).astype(BF16)
    y = jnp.dot(xn, w_ref[...], preferred_element_type=F32)
    n_lo = lo_ref.shape[1]
    lo_ref[...] = y[:, :n_lo].astype(BF16)
    hi_ref[...] = y[:, n_lo:]


def _norm_matmul(x, g, w, tm, n_lo):
    M, D = x.shape
    N = w.shape[1]
    return pl.pallas_call(
        _norm_matmul_kernel,
        out_shape=(jax.ShapeDtypeStruct((M, n_lo), BF16), jax.ShapeDtypeStruct((M, N - n_lo), F32)),
        grid=(M // tm,),
        in_specs=[pl.BlockSpec((tm, D), lambda i: (i, 0)), _resident((1, D)), _resident((D, N))],
        out_specs=(pl.BlockSpec((tm, n_lo), lambda i: (i, 0)), pl.BlockSpec((tm, N - n_lo), lambda i: (i, 0))),
        compiler_params=_params(1),
        name="norm_matmul",
    )(x, g, w)


def _mlstm_reset(caug_ref, mst_ref):
    caug_ref[...] = jnp.zeros_like(caug_ref)
    mst_ref[...] = jnp.zeros_like(mst_ref)


def _mlstm_chunk_phases(q_ref, k_ref, v_ref, o_ref, gt_ref, bias_ref, gh_ref, hg_ref, caug_ref, mst_ref,
                        *, L, H, DK, DV):
    gates = gt_ref[...] + bias_ref[...]
    lane = lax.broadcasted_iota(jnp.int32, gates.shape, 1)
    G = jnp.where(lane < H, gates, _log_sigmoid(gates))
    row = lax.broadcasted_iota(jnp.int32, (L, L), 0)
    col = lax.broadcasted_iota(jnp.int32, (L, L), 1)
    causal = col <= row
    cs = jnp.dot(causal.astype(F32), G, precision=lax.Precision.HIGHEST,
                 preferred_element_type=F32)
    GT = G.T
    csT = cs.T

    kT = (k_ref[...].astype(F32) * (DK ** -0.5)).T
    ones = jnp.ones((L, DV), BF16)
    heads = range(H)


    b_col = [cs[:, H + h:H + h + 1] for h in heads]
    b_row = [csT[H + h:H + h + 1, :] for h in heads]
    i_row = [GT[h:h + 1, :] for h in heads]
    r_mat = [jnp.where(causal, i_row[h] - b_row[h], -jnp.inf) for h in heads]
    r_max = [jnp.max(r_mat[h], axis=1, keepdims=True) for h in heads]
    yield

    m_prev = [mst_ref[h:h + 1, 0:1] for h in heads]
    log_inter = [b_col[h] + m_prev[h] for h in heads]
    m_t = [jnp.maximum(log_inter[h], b_col[h] + r_max[h]) for h in heads]
    w_inter = [jnp.exp(log_inter[h] - m_t[h]) for h in heads]
    W = [jnp.exp(r_mat[h] + (b_col[h] - m_t[h])) for h in heads]

    lane_pair = lax.broadcasted_iota(jnp.int32, (L, 2 * DK), 1)
    S, inter, caug_prev = [None] * H, [None] * H, [None] * H
    for p in range(H // 2):
        q_pair = q_ref[:, p * 2 * DK:(p + 1) * 2 * DK]
        q_lo = jnp.where(lane_pair < DK, q_pair, 0.0)
        q_hi = jnp.where(lane_pair < DK, 0.0, q_pair)
        qm2 = jnp.concatenate([q_lo, q_hi], axis=0).astype(BF16)
        k_pair = (k_ref[:, p * 2 * DK:(p + 1) * 2 * DK] * (DK ** -0.5)).astype(BF16)
        s2 = lax.dot_general(qm2, k_pair, NT_DIMS, preferred_element_type=F32)
        caug_pair = caug_ref[p]
        i2 = jnp.dot(qm2, caug_pair.astype(BF16), preferred_element_type=F32)
        for half in range(2):
            h = 2 * p + half
            S[h] = s2[half * L:(half + 1) * L] * W[h]
            inter[h] = i2[half * L:(half + 1) * L]
            caug_prev[h] = caug_pair[half * DK:(half + 1) * DK, :]
    yield

    m_new = [m_t[h][L - 1:L, :] for h in heads]
    b_last = [cs[L - 1:L, H + h:H + h + 1] for h in heads]
    R = []
    for h in heads:
        w_s_row = jnp.exp(b_last[h] - b_row[h] + i_row[h] - m_new[h])
        kwT = (kT[h * DK:(h + 1) * DK, :] * w_s_row).astype(BF16)
        vaug = jnp.concatenate([v_ref[:, h * DV:(h + 1) * DV], ones], axis=1)
        lhs = jnp.concatenate([S[h].astype(BF16), kwT], axis=0)
        R.append(jnp.dot(lhs, vaug, preferred_element_type=F32))
    yield

    hh = []
    for h in heads:
        numden = w_inter[h] * inter[h] + R[h][:L]
        num, den = numden[:, :DV], numden[:, DV:]
        hh.append(num / jnp.maximum(jnp.abs(den), jnp.exp(-m_t[h])))
    ms = [jnp.mean(hh[h] * hh[h], axis=-1, keepdims=True) for h in heads]
    for h in heads:
        hn = hh[h] * lax.rsqrt(ms[h] + EPS)
        og = jax.nn.sigmoid(o_ref[:, h * DV:(h + 1) * DV])
        hg_ref[:, h * DV:(h + 1) * DV] = (og * (hn * gh_ref[:, h * DV:(h + 1) * DV])).astype(BF16)

    for h in heads:
        p, r0 = h // 2, (h % 2) * DK
        w_c = jnp.exp(b_last[h] + m_prev[h] - m_new[h])
        caug_ref[p, r0:r0 + DK, :] = w_c * caug_prev[h] + R[h][L:]
        mst_ref[h:h + 1, :] = jnp.broadcast_to(m_new[h], (1, LANES))


def _mlstm_write_state(caug_ref, mst_ref, c_out_ref, n_out_ref, m_out_ref, *, H, DK, DV):
    pick0 = (lax.broadcasted_iota(jnp.int32, (8, DV), 1) == 0).astype(F32)
    for h in range(H):
        p, r0 = h // 2, (h % 2) * DK
        ca = caug_ref[p, r0:r0 + DK, :]
        c_out_ref[0, h] = ca[:, :DV]
        n_rows = lax.dot_general(pick0, ca[:, DV:], NT_DIMS, precision=lax.Precision.HIGHEST,
                                 preferred_element_type=F32)
        n_out_ref[0, h:h + 1, :] = n_rows[0:1, :]
        m_out_ref[0, :, h:h + 1] = mst_ref[h:h + 1, 0:1]


def _mlstm_step_kernel(q_ref, k_ref, v_ref, o_ref, gi_ref, gf_ref, bi_ref, bf_ref, gh_ref,
                       c0_ref, n0_ref, m0_ref,
                       hg_ref, c_out_ref, n_out_ref, m_out_ref, *, H, DK, DV, TB):
    RB = TB * H
    i_pre = gi_ref[...] + bi_ref[...]
    log_f = _log_sigmoid(gf_ref[...] + bf_ref[...])
    log_inter = log_f + m0_ref[...]
    m_t = jnp.maximum(log_inter, i_pre)
    w_inter = jnp.exp(log_inter - m_t)
    w_intra = jnp.exp(i_pre - m_t)

    q = q_ref[...].astype(F32)
    k = k_ref[...].astype(F32) * (DK ** -0.5)
    v = v_ref[...].astype(F32)
    n_prev = n0_ref[...]
    s = jnp.sum(q * k, axis=1, keepdims=True) * w_intra
    den = w_inter * jnp.sum(q * n_prev, axis=1, keepdims=True) + s

    def block_diag(x):
        wide = jnp.concatenate([x] * H, axis=1)
        head_of_lane = lax.broadcasted_iota(jnp.int32, wide.shape, 1) // DK
        head_of_row = lax.broadcasted_iota(jnp.int32, wide.shape, 0) % H
        return jnp.where(head_of_lane == head_of_row, wide, 0.0)

    q_bd = block_diag(q)
    kw_bd = block_diag(k * w_intra[:, :DK])
    qc = jnp.concatenate(
        [jnp.dot(q_bd[t * H:(t + 1) * H, :], c0_ref[t], preferred_element_type=F32) for t in range(TB)],
        axis=0)

    num = w_inter * qc + s * v
    hh = num / jnp.maximum(jnp.abs(den), jnp.exp(-m_t))
    hn = hh * lax.rsqrt(jnp.mean(hh * hh, axis=-1, keepdims=True) + EPS)
    hg_ref[...] = (jax.nn.sigmoid(o_ref[...]) * (hn * gh_ref[...])).astype(BF16)
    n_out_ref[...] = w_inter[:, :DK] * n_prev + w_intra[:, :DK] * k
    m_out_ref[...] = m_t

    for t in range(TB):
        rows = slice(t * H, (t + 1) * H)
        d_c = lax.dot_general(kw_bd[rows, :], v[rows, :], (((0,), (0,)), ((), ())),
                              preferred_element_type=F32)
        for h in range(H):
            blk = slice(h * DK, (h + 1) * DK)
            c_out_ref[t, blk, :] = w_inter[t * H + h:t * H + h + 1, :] * c0_ref[t, blk, :] + d_c[blk, :]


def _mlstm_sample(qkv, og, b_gates, g_head, c0, n0, m0, TB):
    H, DK, DV = ML_HEADS, 64, 128
    B = qkv.shape[0]
    R, RB = B * H, TB * H
    qk_w, v_w = H * DK, H * DV
    lanes = lambda x: jnp.broadcast_to(x.reshape(-1, 1), (x.size, LANES))
    per_block = lambda x: jnp.tile(x, (TB, 1))
    q = qkv[:, :qk_w].reshape(R, DK)
    k = qkv[:, qk_w:2 * qk_w].reshape(R, DK)
    v = qkv[:, 2 * qk_w:].reshape(R, DV)
    o = og[:, :v_w].reshape(R, DV)
    operands = (
        q, k, v, o,
        lanes(og[:, v_w:v_w + H]), lanes(og[:, v_w + H:v_w + 2 * H]),
        per_block(lanes(b_gates[:H])), per_block(lanes(b_gates[H:])),
        per_block(g_head.reshape(H, DV)),
        c0.reshape(B, H * DK, DV), n0.reshape(R, DK), lanes(m0),
    )
    row_blk = lambda n: pl.BlockSpec((RB, n), lambda b: (b, 0))
    state_blk = pl.BlockSpec((TB, H * DK, DV), lambda b: (b, 0, 0))
    kern = functools.partial(_mlstm_step_kernel, H=H, DK=DK, DV=DV, TB=TB)
    hg, c_new, n_new, m_new = pl.pallas_call(
        kern,
        out_shape=(
            jax.ShapeDtypeStruct((R, DV), BF16),
            jax.ShapeDtypeStruct((B, H * DK, DV), F32),
            jax.ShapeDtypeStruct((R, DK), F32),
            jax.ShapeDtypeStruct((R, LANES), F32),
        ),
        grid=(B // TB,),
        in_specs=[
            row_blk(DK), row_blk(DK), row_blk(DV), row_blk(DV),
            row_blk(LANES), row_blk(LANES),
            _resident((RB, LANES)), _resident((RB, LANES)), _resident((RB, DV)),
            state_blk, row_blk(DK), row_blk(LANES),
        ],
        out_specs=(row_blk(DV), state_blk, row_blk(DK), row_blk(LANES)),
        compiler_params=_params(1),
        name="mlstm_sample",
    )(*operands)
    return (hg.reshape(B, v_w), c_new.reshape(B, H, DK, DV), n_new.reshape(B, H, DK),
            m_new[:, 0].reshape(B, H))


def _ffn_kernel(h_ref, a_ref, wa_ref, gn_ref, wgu_ref, wd_ref, gf_ref, o_ref, *, d_ff, tf, final_norm, a_tile):
    if a_tile:
        proj = jnp.concatenate(
            [lax.dot_general(a_ref[t], wa_ref[...], (((0,), (0,)), ((), ())), preferred_element_type=F32)
             for t in range(a_ref.shape[0])], axis=0)
    else:
        proj = jnp.dot(a_ref[...], wa_ref[...], preferred_element_type=F32)
    h1 = h_ref[...] + proj
    xn = _rms(h1, gn_ref[...]).astype(BF16)
    acc = h1
    for c in range(d_ff // tf):
        g = jnp.dot(xn, wgu_ref[:, c * tf:(c + 1) * tf], preferred_element_type=F32)
        u = jnp.dot(xn, wgu_ref[:, d_ff + c * tf:d_ff + (c + 1) * tf], preferred_element_type=F32)
        act = (g * jax.nn.sigmoid(g) * u).astype(BF16)
        acc = acc + jnp.dot(act, wd_ref[c * tf:(c + 1) * tf, :], preferred_element_type=F32)
    if final_norm:
        acc = _rms(acc, gf_ref[...])
    o_ref[...] = acc


def _mixer_out_ffn(h, a, w_a, g_ffn, w_gu, w_d, layer, g_final, tm, final_norm):
    M, D = h.shape
    KA = w_a.shape[0]
    d_ff = w_d.shape[1]
    a_tile = a.shape[2] if a.ndim == 3 else 0
    kern = functools.partial(_ffn_kernel, d_ff=d_ff, tf=256, final_norm=final_norm, a_tile=a_tile)
    single = pl.Buffered(1)
    a_spec = (pl.BlockSpec((tm // a_tile, KA, a_tile), lambda i: (i, 0, 0)) if a_tile
              else pl.BlockSpec((tm, KA), lambda i: (i, 0)))
    return pl.pallas_call(
        kern,
        out_shape=jax.ShapeDtypeStruct((M, D), F32),
        grid=(M // tm,),
        in_specs=[
            pl.BlockSpec((tm, D), lambda i: (i, 0)),
            a_spec,
            pl.BlockSpec((KA, D), lambda i: (0, 0), pipeline_mode=single),
            _resident((1, D)),
            pl.BlockSpec((None, D, 2 * d_ff), lambda i: (layer, 0, 0), pipeline_mode=single),
            pl.BlockSpec((None, d_ff, D), lambda i: (layer, 0, 0), pipeline_mode=single),
            _resident((1, D)),
        ],
        out_specs=pl.BlockSpec((tm, D), lambda i: (i, 0)),
        compiler_params=_params(1),
        name="mixer_out_ffn",
    )(h, a, w_a, g_ffn, w_gu, w_d, g_final)


def _rope_slot(x, cos, sin_lo, sin_hi):
    return x * cos + pltpu.roll(x, 96, 1) * sin_lo + pltpu.roll(x, 32, 1) * sin_hi


def _mla_proj_kernel(h_ref, gn_ref, win_ref, gq_ref, gkv_ref, wuq_ref, wukt_ref,
                     cos_ref, sinlo_ref, sinhi_ref,
                     q_ref, kcat_ref, lat_ref, kr_ref, *maybe_vt_ref, H, vt_tile):
    xn = _rms(h_ref[...], gn_ref[...]).astype(BF16)
    t = jnp.dot(xn, win_ref[...], preferred_element_type=F32)
    c_q = t[:, :Q_LORA]
    c_kv = t[:, Q_LORA:Q_LORA + KV_LORA]
    k_slot = t[:, Q_LORA + KV_LORA:]
    cos, sin_lo, sin_hi = cos_ref[...], sinlo_ref[...], sinhi_ref[...]

    lat = _rms(c_kv, gkv_ref[...])
    k_rot = _rope_slot(k_slot, cos, sin_lo, sin_hi)
    lat_ref[...] = lat
    kr_ref[...] = k_rot[:, :QK_ROPE]
    kcat_ref[:, :KV_LORA] = lat.astype(BF16)
    kcat_ref[:, KV_LORA:] = k_rot.astype(BF16)
    if vt_tile:
        (vt_ref,) = maybe_vt_ref
        for t in range(lat.shape[0] // vt_tile):
            vt_ref[t] = lat[t * vt_tile:(t + 1) * vt_tile, :].T.astype(BF16)

    cqn = _rms(c_q, gq_ref[...]).astype(BF16)
    tm = cqn.shape[0]

    def store_q(h, lanes, val):
        if vt_tile:
            sub = vt_tile // 2
            for t in range(tm // vt_tile):
                for half in range(2):
                    src = slice(t * vt_tile + half * sub, t * vt_tile + (half + 1) * sub)
                    dst = slice((half * H + h) * sub, (half * H + h + 1) * sub)
                    q_ref[t, dst, lanes] = val[src].astype(BF16)
        else:
            q_ref[:, slice(h * Q_SLOT + lanes.start, h * Q_SLOT + lanes.stop)] = val.astype(BF16)

    q_rope = jnp.dot(cqn, wuq_ref[:, H * QK_NOPE:], preferred_element_type=F32)
    for h in range(H):
        store_q(h, slice(KV_LORA, Q_SLOT), _rope_slot(q_rope[:, h * LANES:(h + 1) * LANES], cos, sin_lo, sin_hi))
    q_nope = jnp.dot(cqn, wuq_ref[:, :H * QK_NOPE], preferred_element_type=F32)
    for h in range(H):
        q_lat = jnp.dot(q_nope[:, h * QK_NOPE:(h + 1) * QK_NOPE].astype(BF16), wukt_ref[h],
                        preferred_element_type=F32)
        store_q(h, slice(0, KV_LORA), q_lat)


def _mla_project(h, g_norm, w_in, g_q, g_kv, w_uq, w_ukt, cos, sin_lo, sin_hi, tm, table_blocks, vt_tile):
    M, D = h.shape
    H = MLA_HEADS
    kern = functools.partial(_mla_proj_kernel, H=H, vt_tile=vt_tile)
    table = pl.BlockSpec((tm, LANES), lambda i: (i % table_blocks, 0))
    out_shape = [
        jax.ShapeDtypeStruct((M, H * Q_SLOT), BF16),
        jax.ShapeDtypeStruct((M, Q_SLOT), BF16),
        jax.ShapeDtypeStruct((M, KV_LORA), F32),
        jax.ShapeDtypeStruct((M, QK_ROPE), F32),
    ]
    out_specs = [
        pl.BlockSpec((tm, H * Q_SLOT), lambda i: (i, 0)),
        pl.BlockSpec((tm, Q_SLOT), lambda i: (i, 0)),
        pl.BlockSpec((tm, KV_LORA), lambda i: (i, 0)),
        pl.BlockSpec((tm, QK_ROPE), lambda i: (i, 0)),
    ]
    if vt_tile:
        out_shape[0] = jax.ShapeDtypeStruct((M // vt_tile, H * vt_tile, Q_SLOT), BF16)
        out_specs[0] = pl.BlockSpec((tm // vt_tile, H * vt_tile, Q_SLOT), lambda i: (i, 0, 0))
        out_shape.append(jax.ShapeDtypeStruct((M // vt_tile, KV_LORA, vt_tile), BF16))
        out_specs.append(pl.BlockSpec((tm // vt_tile, KV_LORA, vt_tile), lambda i: (i, 0, 0)))
    return pl.pallas_call(
        kern,
        out_shape=tuple(out_shape),
        grid=(M // tm,),
        in_specs=[
            pl.BlockSpec((tm, D), lambda i: (i, 0)),
            _resident((1, D)),
            _resident(w_in.shape),
            _resident((1, Q_LORA)),
            _resident((1, KV_LORA)),
            _resident(w_uq.shape),
            _resident(w_ukt.shape),
            table, table, table,
        ],
        out_specs=tuple(out_specs),
        compiler_params=_params(1),
        name="mla_project",
    )(h, g_norm, w_in, g_q, g_kv, w_uq, w_ukt, cos, sin_lo, sin_hi)


def _attn_kernel(q_ref, k_ref, vt_ref, wuvt_ref, o_ref, m_ref, l_ref, acc_ref, sa_ref, sb_ref, *, H, T):
    i = pl.program_id(1)

    def scores(j):
        kj = k_ref[0, pl.ds(pl.multiple_of(j * T, T), T), :]
        return lax.dot_general(kj, q_ref[0], NT_DIMS, preferred_element_type=F32) * MLA_SCALE

    def update(j, st):
        vtj = vt_ref[j]
        m_prev = m_ref[...]
        m_new = jnp.maximum(m_prev, jnp.max(st, axis=0, keepdims=True))
        alpha = jnp.exp(m_prev - m_new)
        p = jnp.exp(st - m_new)
        l_ref[...] = alpha * l_ref[...] + jnp.sum(p, axis=0, keepdims=True)
        acc_ref[...] = alpha * acc_ref[...] + jnp.dot(vtj, p.astype(BF16), preferred_element_type=F32)
        m_ref[...] = m_new

    SUB, C2 = T // 2, H * T // 2
    base = pl.multiple_of(i * T, T)
    k_a = k_ref[0, pl.ds(base, SUB), :]
    k_b = k_ref[0, pl.ds(base + SUB, SUB), :]
    vt_i = vt_ref[i]
    vt_a, vt_b = vt_i[:, :SUB], vt_i[:, SUB:]
    s_a = lax.dot_general(k_a, q_ref[0], NT_DIMS, preferred_element_type=F32) * MLA_SCALE
    s_b = lax.dot_general(k_b, q_ref[0, C2:, :], NT_DIMS, preferred_element_type=F32) * MLA_SCALE
    key = lax.broadcasted_iota(jnp.int32, (SUB, C2), 0)
    qry = lax.broadcasted_iota(jnp.int32, (SUB, C2), 1) & (SUB - 1)
    causal = key <= qry
    s_a0 = jnp.where(causal, s_a[:, :C2], -jnp.inf)
    s_a1 = s_a[:, C2:]
    s_b1 = jnp.where(causal, s_b, -jnp.inf)
    m_0 = jnp.max(s_a0, axis=0, keepdims=True)
    m_1 = jnp.maximum(jnp.max(s_a1, axis=0, keepdims=True), jnp.max(s_b1, axis=0, keepdims=True))
    p_a0, p_a1, p_b1 = jnp.exp(s_a0 - m_0), jnp.exp(s_a1 - m_1), jnp.exp(s_b1 - m_1)
    m_ref[:, :C2] = m_0
    m_ref[:, C2:] = m_1
    l_ref[:, :C2] = jnp.sum(p_a0, axis=0, keepdims=True)
    l_ref[:, C2:] = jnp.sum(p_a1, axis=0, keepdims=True) + jnp.sum(p_b1, axis=0, keepdims=True)
    acc_ref[:, :C2] = jnp.dot(vt_a, p_a0.astype(BF16), preferred_element_type=F32)
    acc_ref[:, C2:] = (jnp.dot(vt_a, p_a1.astype(BF16), preferred_element_type=F32)
                       + jnp.dot(vt_b, p_b1.astype(BF16), preferred_element_type=F32))

    @pl.when(i > 0)
    def _():
        sa_ref[...] = scores(0)

    def pair(jj, carry):
        j = 2 * jj
        sb_ref[...] = scores(j + 1)
        update(j, sa_ref[...])
        sa_ref[...] = scores(jnp.minimum(j + 2, i - 1))
        update(j + 1, sb_ref[...])
        return carry

    lax.fori_loop(0, i // 2, pair, 0)

    @pl.when(i % 2 == 1)
    def _():
        update(i - 1, sa_ref[...])

    o_t = (acc_ref[...] / l_ref[...]).astype(BF16)
    for half in range(2):
        for h in range(H):
            cols = slice((half * H + h) * SUB, (half * H + h + 1) * SUB)
            v_t = jnp.dot(wuvt_ref[h], o_t[:, cols], preferred_element_type=F32)
            o_ref[0, h * V_DIM:(h + 1) * V_DIM, half * SUB:(half + 1) * SUB] = v_t.astype(BF16)


def _attention_prompt(q, kcat, vt, w_uvt, B, S, T):
    H = MLA_HEADS
    NQ = S // T
    kern = functools.partial(_attn_kernel, H=H, T=T)
    return pl.pallas_call(
        kern,
        out_shape=jax.ShapeDtypeStruct((B * NQ, H * V_DIM, T), BF16),
        grid=(B, NQ),
        in_specs=[
            pl.BlockSpec((1, H * T, Q_SLOT), lambda b, i: (b * NQ + i, 0, 0)),
            pl.BlockSpec((1, S, Q_SLOT), lambda b, i: (b, 0, 0)),
            pl.BlockSpec((NQ, KV_LORA, T), lambda b, i: (b, 0, 0)),
            _resident(w_uvt.shape),
        ],
        out_specs=pl.BlockSpec((1, H * V_DIM, T), lambda b, i: (b * NQ + i, 0, 0)),
        scratch_shapes=[
            pltpu.VMEM((1, H * T), F32),
            pltpu.VMEM((1, H * T), F32),
            pltpu.VMEM((KV_LORA, H * T), F32),
            pltpu.VMEM((T, H * T), F32),
            pltpu.VMEM((T, H * T), F32),
        ],
        compiler_params=_params(2),
        name="attention_prompt",
    )(q, kcat.reshape(B, S, Q_SLOT), vt, w_uvt)


def _decode_mlstm_kernel(pt_ref,
                         q_ref, cn_ref, krn_ref, lat_hbm, krt_hbm,
                         mq_ref, mk_ref, mv_ref, mo_ref, gt_ref, bias_ref, gh_ref,
                         o_ref, hg_ref, c_out_ref, n_out_ref, m_out_ref,
                         lat_buf, kr_buf, sem, caug_ref, mst_ref,
                         *, layer, G, P, NCH, RING, STREAMS, L, H_ML, DK, DV, NC):
    s = pl.program_id(0)
    ns = pl.num_programs(0)
    total = ns * NCH
    ml = dict(L=L, H=H_ML, DK=DK, DV=DV)

    def page_copies(c):
        slot = lax.rem(c, RING)
        cw = jnp.where(c >= total, c - total, c)
        bb, jj = lax.div(cw, NCH), lax.rem(cw, NCH)
        copies = []
        for g in range(G):
            page = pt_ref[bb, jj * G + g]
            copies.append(pltpu.make_async_copy(
                lat_hbm.at[layer, page], lat_buf.at[slot, pl.ds(g * P, P), :], sem.at[0, slot]))
            copies.append(pltpu.make_async_copy(
                krt_hbm.at[layer, page], kr_buf.at[slot, :, pl.ds(g * P, P)], sem.at[1, slot]))
        return copies

    def start(c):
        for cp in page_copies(c):
            cp.start()

    def wait(c):
        for cp in page_copies(c):
            cp.wait()

    @pl.when(s == 0)
    def _():
        for c in range(RING - 1):
            start(jnp.int32(c))

    ml_chunk = lax.rem(s, NC)

    @pl.when(ml_chunk == 0)
    def _():
        _mlstm_reset(caug_ref, mst_ref)

    q = q_ref[0].astype(F32)
    q_lat = q[:, :KV_LORA]
    q_rope = q[:, KV_LORA:KV_LORA + QK_ROPE]
    c_new = cn_ref[0]
    kr_new = krn_ref[0]

    s_new = (jnp.sum(q_lat * c_new, axis=1, keepdims=True)
             + jnp.sum(q_rope * kr_new, axis=1, keepdims=True)) * MLA_SCALE
    n_heads = q.shape[0]
    m_run = [s_new] + [jnp.full_like(s_new, -jnp.inf)] * (STREAMS - 1)
    l_run = [jnp.ones_like(s_new)] + [jnp.zeros_like(s_new)] * (STREAMS - 1)
    acc = ([jnp.broadcast_to(c_new, (n_heads, KV_LORA)).astype(F32)]
           + [jnp.zeros((n_heads, KV_LORA), F32)] * (STREAMS - 1))
    W = (G * P) // STREAMS

    ml_pieces = _mlstm_chunk_phases(mq_ref, mk_ref, mv_ref, mo_ref, gt_ref, bias_ref, gh_ref, hg_ref,
                                    caug_ref, mst_ref, **ml)
    assert NCH >= 2
    for j in range(NCH):
        c = s * NCH + j
        slot = lax.rem(c, RING)
        wait(c)
        kl = [lat_buf[slot, i * W:(i + 1) * W, :] for i in range(STREAMS)]
        sc = [(lax.dot_general(q_lat, kl[i], NT_DIMS, preferred_element_type=F32)
               + jnp.dot(q_rope, kr_buf[slot, :, i * W:(i + 1) * W], preferred_element_type=F32)) * MLA_SCALE
              for i in range(STREAMS)]
        if j in (0, NCH - 1):
            next(ml_pieces)
        for i in range(STREAMS):
            m_new = jnp.maximum(m_run[i], jnp.max(sc[i], axis=1, keepdims=True))
            alpha = jnp.exp(m_run[i] - m_new)
            p = jnp.exp(sc[i] - m_new)
            m_run[i] = m_new
            l_run[i] = alpha * l_run[i] + jnp.sum(p, axis=1, keepdims=True)
            acc[i] = alpha * acc[i] + jnp.dot(p, kl[i], preferred_element_type=F32)
        if j in (0, NCH - 1):
            next(ml_pieces, None)
        start(c + (RING - 1))

    m_all = functools.reduce(jnp.maximum, m_run)
    scale = [jnp.exp(m_i - m_all) for m_i in m_run]
    l_all = sum(l_i * w_i for l_i, w_i in zip(l_run, scale))
    acc_all = sum(a_i * w_i for a_i, w_i in zip(acc, scale))
    o_ref[0] = acc_all / l_all

    @pl.when(ml_chunk == NC - 1)
    def _():
        _mlstm_write_state(caug_ref, mst_ref, c_out_ref, n_out_ref, m_out_ref, H=H_ML, DK=DK, DV=DV)

    @pl.when(s == ns - 1)
    def _():
        for c in range(RING - 1):
            wait(total + c)


def _attention_sample_with_mlstm_prompt(q3, c_new, kr_new, cache_latent, cache_k_rope_t, page_table, layer, G,
                                        qkv, og, bias, g_head, B, S, L):
    BS, H, _ = q3.shape
    n_pages = page_table.shape[1]
    P = cache_latent.shape[2]
    NCH = n_pages // G
    assert n_pages % G == 0
    RING = 3
    H_ML, DK, DV = ML_HEADS, 64, 128
    NC = S // L
    assert B * NC == BS, "one mLSTM chunk per decode sequence"
    qk_w, v_w = H_ML * DK, H_ML * DV
    gate_blk = v_w // LANES
    kern = functools.partial(_decode_mlstm_kernel, layer=layer, G=G, P=P, NCH=NCH, RING=RING, STREAMS=2,
                             L=L, H_ML=H_ML, DK=DK, DV=DV, NC=NC)
    grid_spec = pltpu.PrefetchScalarGridSpec(
        num_scalar_prefetch=1,
        grid=(BS,),
        in_specs=[
            pl.BlockSpec((1, H, Q_SLOT), lambda s, pt: (s, 0, 0)),
            pl.BlockSpec((1, 1, KV_LORA), lambda s, pt: (s, 0, 0)),
            pl.BlockSpec((1, 1, QK_ROPE), lambda s, pt: (s, 0, 0)),
            pl.BlockSpec(memory_space=pl.ANY),
            pl.BlockSpec(memory_space=pl.ANY),
            pl.BlockSpec((L, qk_w), lambda s, pt: (s, 0)),
            pl.BlockSpec((L, qk_w), lambda s, pt: (s, 1)),
            pl.BlockSpec((L, v_w), lambda s, pt: (s, 1)),
            pl.BlockSpec((L, v_w), lambda s, pt: (s, 0)),
            pl.BlockSpec((L, LANES), lambda s, pt: (s, gate_blk)),
            pl.BlockSpec((1, LANES), lambda s, pt: (0, 0)),
            pl.BlockSpec((1, v_w), lambda s, pt: (0, 0)),
        ],
        out_specs=(
            pl.BlockSpec((1, H, KV_LORA), lambda s, pt: (s, 0, 0)),
            pl.BlockSpec((L, v_w), lambda s, pt: (s, 0)),
            pl.BlockSpec((1, H_ML, DK, DV), lambda s, pt: (s // NC, 0, 0, 0)),
            pl.BlockSpec((1, H_ML, DK), lambda s, pt: (s // NC, 0, 0)),
            pl.BlockSpec((1, 1, H_ML), lambda s, pt: (s // NC, 0, 0)),
        ),
        scratch_shapes=[
            pltpu.VMEM((RING, G * P, KV_LORA), F32),
            pltpu.VMEM((RING, QK_ROPE, G * P), F32),
            pltpu.SemaphoreType.DMA((2, RING)),
            pltpu.VMEM((H_ML // 2, 2 * DK, 2 * DV), F32),
            pltpu.VMEM((H_ML, LANES), F32),
        ],
    )
    return pl.pallas_call(
        kern,
        out_shape=(
            jax.ShapeDtypeStruct((BS, H, KV_LORA), F32),
            jax.ShapeDtypeStruct((B * S, v_w), BF16),
            jax.ShapeDtypeStruct((B, H_ML, DK, DV), F32),
            jax.ShapeDtypeStruct((B, H_ML, DK), F32),
            jax.ShapeDtypeStruct((B, 1, H_ML), F32),
        ),
        grid_spec=grid_spec,
        compiler_params=_params(1),
        name="attention_sample_mlstm_prompt",
    )(page_table, q3, c_new.reshape(BS, 1, KV_LORA), kr_new.reshape(BS, 1, QK_ROPE),
      cache_latent, cache_k_rope_t, qkv, qkv, qkv, og, og, bias, g_head)


def _value_up_kernel(o_ref, wuv_ref, v_ref, *, H):
    for h in range(H):
        o_h = o_ref[:, h * KV_LORA:(h + 1) * KV_LORA].astype(BF16)
        v_ref[:, h * V_DIM:(h + 1) * V_DIM] = jnp.dot(
            o_h, wuv_ref[h], preferred_element_type=F32).astype(BF16)


def _value_up(o_lat, w_uv):
    M = o_lat.shape[0]
    H = MLA_HEADS
    return pl.pallas_call(
        functools.partial(_value_up_kernel, H=H),
        out_shape=jax.ShapeDtypeStruct((M, H * V_DIM), BF16),
        grid=(1,),
        in_specs=[_resident(o_lat.shape), _resident(w_uv.shape)],
        out_specs=_resident((M, H * V_DIM)),
        compiler_params=_params(1),
        name="value_up",
    )(o_lat, w_uv)


def _rope_tables(pos):
    half = QK_ROPE // 2
    inv = ROPE_THETA ** (-jnp.arange(0, QK_ROPE, 2, dtype=F32) / QK_ROPE)
    ang = pos[:, None] * inv[None, :]
    cos, sin = jnp.cos(ang), jnp.sin(ang)
    z = jnp.zeros_like(cos)
    cos_t = jnp.concatenate([cos, cos, z, z], axis=1)
    sin_lo = jnp.concatenate([-sin, z, z, z], axis=1)
    sin_hi = jnp.concatenate([z, sin, z, z], axis=1)
    return cos_t, sin_lo, sin_hi


def _pad_cols(w, n):
    return jnp.pad(w, ((0, 0), (0, n - w.shape[1])))


def kernel(x_prompt, x_sample, state_mlstm_C, state_mlstm_n, state_mlstm_m, cache_latent, cache_k_rope,
           page_table, norm_mix, norm_ffn, norm_final, mlstm_w_in, mlstm_b_gates, mlstm_g_head, mlstm_w_out,
           mla_w_in, mla_g_q, mla_g_kv, mla_w_uq, mla_w_uk, mla_w_uv, mla_w_o, ffn_w_gate_up, ffn_w_down):
    B, S, D = x_prompt.shape
    BS, T, _ = x_sample.shape
    assert T == 1, "sample group is one new token per sequence"
    depth = norm_mix.shape[0]
    H = MLA_HEADS
    past_len = page_table.shape[1] * cache_latent.shape[2]

    hp = x_prompt.reshape(B * S, D)
    hs = x_sample.reshape(BS, D)
    row = lambda v: v.reshape(1, -1).astype(F32)

    ROW_TILE = 1024
    ML_CHUNK = 256
    T_ATT = 512
    PAGES_PER_CHUNK = 64
    SAMPLE_SEQS_PER_STEP = 16

    pos_p = jnp.arange(S, dtype=F32)
    pos_s = jnp.broadcast_to(jnp.arange(T, dtype=F32) + past_len, (BS,))
    rope_p = _rope_tables(pos_p)
    rope_s = _rope_tables(pos_s)

    assert depth == 2, "schedule below is written for one mLSTM layer followed by one MLA layer"
    g_fin = row(norm_final)

    n_gates = 2 * ML_HEADS
    ml_w_in = _pad_cols(mlstm_w_in[0], mlstm_w_in.shape[2] - n_gates + LANES).astype(BF16)
    ml_bias = _pad_cols(mlstm_b_gates[0].reshape(1, -1), LANES).astype(F32)
    ml_g_head = row(mlstm_g_head[0])
    ml_w_out = mlstm_w_out[0].astype(BF16)
    mla_in = _pad_cols(mla_w_in[0], Q_LORA + KV_LORA + LANES).astype(BF16)
    wq = mla_w_uq[0].reshape(Q_LORA, H, QK_NOPE + QK_ROPE)
    wq_nope = wq[:, :, :QK_NOPE].reshape(Q_LORA, H * QK_NOPE)
    wq_rope = jnp.pad(wq[:, :, QK_NOPE:], ((0, 0), (0, 0), (0, LANES - QK_ROPE))).reshape(Q_LORA, H * LANES)
    w_uq = jnp.concatenate([wq_nope, wq_rope], axis=1).astype(BF16)
    w_ukt = jnp.transpose(mla_w_uk[0].reshape(KV_LORA, H, QK_NOPE), (1, 2, 0)).astype(BF16)
    w_uv = jnp.transpose(mla_w_uv[0].reshape(KV_LORA, H, V_DIM), (1, 0, 2)).astype(BF16)
    w_uvt = jnp.transpose(mla_w_uv[0].reshape(KV_LORA, H, V_DIM), (1, 2, 0)).astype(BF16)
    w_o = mla_w_o[0].astype(BF16)
    g_q, g_kv = row(mla_g_q[0]), row(mla_g_kv[0])
    mla_weights = (mla_in, g_q, g_kv, w_uq, w_ukt)
    w_gu_all, w_d_all = ffn_w_gate_up.astype(BF16), ffn_w_down.astype(BF16)
    ffn = [(row(norm_ffn[l]), w_gu_all, w_d_all, l) for l in range(depth)]
    g_mix = [row(norm_mix[l]) for l in range(depth)]
    cache_k_rope_t = jnp.swapaxes(cache_k_rope, 2, 3)

    n_qkv = 2 * ML_HEADS * 64 + ML_HEADS * 128
    qkv_s, og_s = _norm_matmul(hs, g_mix[0], ml_w_in, BS, n_qkv)
    a_s, C_s, n_s, m_s = _mlstm_sample(qkv_s, og_s, mlstm_b_gates[0].astype(F32), ml_g_head, state_mlstm_C[0],
                                       state_mlstm_n[0], state_mlstm_m[0], TB=SAMPLE_SEQS_PER_STEP)
    hs = _mixer_out_ffn(hs, a_s, ml_w_out, *ffn[0], g_fin, BS, final_norm=False)
    q_s, _, lat_s, kr_s = _mla_project(hs, g_mix[1], *mla_weights, *rope_s, tm=BS, table_blocks=1, vt_tile=0)

    qkv_p, og_p = _norm_matmul(hp, g_mix[0], ml_w_in, ROW_TILE, n_qkv)
    o_s, a_p, C_p, n_p, m_p = _attention_sample_with_mlstm_prompt(
        q_s.reshape(BS, H, Q_SLOT), lat_s, kr_s, cache_latent, cache_k_rope_t, page_table, layer=0,
        G=PAGES_PER_CHUNK, qkv=qkv_p, og=og_p, bias=ml_bias, g_head=ml_g_head, B=B, S=S, L=ML_CHUNK)
    hp = _mixer_out_ffn(hp, a_p, ml_w_out, *ffn[0], g_fin, ROW_TILE, final_norm=False)

    q_p, kcat_p, lat_p, kr_p, vt_p = _mla_project(hp, g_mix[1], *mla_weights, *rope_p,
                                                  tm=ROW_TILE, table_blocks=S // ROW_TILE, vt_tile=T_ATT)
    a_p = _attention_prompt(q_p, kcat_p, vt_p, w_uvt, B, S, T=T_ATT)
    hp = _mixer_out_ffn(hp, a_p, w_o, *ffn[1], g_fin, ROW_TILE, final_norm=True)

    a_s = _value_up(o_s.reshape(BS, H * KV_LORA), w_uv)
    hs = _mixer_out_ffn(hs, a_s, w_o, *ffn[1], g_fin, BS, final_norm=True)

    return (hp.reshape(B, S, D), hs.reshape(BS, T, D),
            C_p[None], n_p[None], m_p.reshape(1, B, -1),
            C_s[None], n_s[None], m_s.reshape(1, BS, -1),
            lat_p.reshape(1, B, S, KV_LORA), kr_p.reshape(1, B, S, QK_ROPE),
            lat_s.reshape(1, BS, T, KV_LORA), kr_s.reshape(1, BS, T, QK_ROPE))
```

```python
import functools

import jax
import jax.numpy as jnp
from jax import lax
from jax.experimental import pallas as pl
from jax.experimental.pallas import tpu as pltpu

F32 = jnp.float32
BF16 = jnp.bfloat16
EPS = 1e-6
ROPE_THETA = 10000.0

V7X_VMEM_BYTES = 64 * 1024 * 1024
LANES = 128
VMEM_LIMIT_BYTES = V7X_VMEM_BYTES - 8 * 1024 * 1024

ML_HEADS = 8
MLA_HEADS = 8
Q_LORA = 384
KV_LORA = 256
QK_NOPE = 128
QK_ROPE = 64
V_DIM = 128
Q_SLOT = KV_LORA + LANES
MLA_SCALE = (QK_NOPE + QK_ROPE) ** -0.5

NT_DIMS = (((1,), (1,)), ((), ()))


def _params(n_grid_axes):
    return pltpu.CompilerParams(
        dimension_semantics=("arbitrary",) * n_grid_axes,
        vmem_limit_bytes=VMEM_LIMIT_BYTES,
    )


def _rms(x, g):
    return x * lax.rsqrt(jnp.mean(x * x, axis=-1, keepdims=True) + EPS) * g


def _log_sigmoid(x):
    return jnp.minimum(x, 0.0) - jnp.log1p(jnp.exp(-jnp.abs(x)))


def _resident(shape):
    nd = len(shape)
    return pl.BlockSpec(shape, lambda *_: (0,) * nd)


def _norm_matmul_kernel(x_ref, g_ref, w_ref, lo_ref, hi_ref):
    xn = _rms(x_ref[...], g_ref[...]).astype(BF16)
    y = jnp.dot(xn, w_ref[...], preferred_element_type=F32)
    n_lo = lo_ref.shape[1]
    lo_ref[...] = y[:, :n_lo].astype(BF16)
    hi_ref[...] = y[:, n_lo:]


def _norm_matmul(x, g, w, tm, n_lo):
    M, D = x.shape
    N = w.shape[1]
    return pl.pallas_call(
        _norm_matmul_kernel,
        out_shape=(jax.ShapeDtypeStruct((M, n_lo), BF16), jax.ShapeDtypeStruct((M, N - n_lo), F32)),
        grid=(M // tm,),
        in_specs=[pl.BlockSpec((tm, D), lambda i: (i, 0)), _resident((1, D)), _resident((D, N))],
        out_specs=(pl.BlockSpec((tm, n_lo), lambda i: (i, 0)), pl.BlockSpec((tm, N - n_lo), lambda i: (i, 0))),
        compiler_params=_params(1),
        name="norm_matmul",
    )(x, g, w)


def _mlstm_reset(caug_ref, mst_ref):
    caug_ref[...] = jnp.zeros_like(caug_ref)
    mst_ref[...] = jnp.zeros_like(mst_ref)


def _mlstm_chunk_phases(q_ref, k_ref, v_ref, o_ref, gt_ref, bias_ref, gh_ref, hg_ref, caug_ref, mst_ref,
                        *, L, H, DK, DV):
    gates = gt_ref[...] + bias_ref[...]
    lane = lax.broadcasted_iota(jnp.int32, gates.shape, 1)
    G = jnp.where(lane < H, gates, _log_sigmoid(gates))
    row = lax.broadcasted_iota(jnp.int32, (L, L), 0)
    col = lax.broadcasted_iota(jnp.int32, (L, L), 1)
    causal = col <= row
    cs = jnp.dot(causal.astype(F32), G, precision=lax.Precision.HIGHEST,
                 preferred_element_type=F32)
    GT = G.T
    csT = cs.T

    kT = (k_ref[...].astype(F32) * (DK ** -0.5)).T
    ones = jnp.ones((L, DV), BF16)
    heads = range(H)


    b_col = [cs[:, H + h:H + h + 1] for h in heads]
    b_row = [csT[H + h:H + h + 1, :] for h in heads]
    i_row = [GT[h:h + 1, :] for h in heads]
    r_mat = [jnp.where(causal, i_row[h] - b_row[h], -jnp.inf) for h in heads]
    r_max = [jnp.max(r_mat[h], axis=1, keepdims=True) for h in heads]
    yield

    m_prev = [mst_ref[h:h + 1, 0:1] for h in heads]
    log_inter = [b_col[h] + m_prev[h] for h in heads]
    m_t = [jnp.maximum(log_inter[h], b_col[h] + r_max[h]) for h in heads]
    w_inter = [jnp.exp(log_inter[h] - m_t[h]) for h in heads]
    W = [jnp.exp(r_mat[h] + (b_col[h] - m_t[h])) for h in heads]

    lane_pair = lax.broadcasted_iota(jnp.int32, (L, 2 * DK), 1)
    S, inter, caug_prev = [None] * H, [None] * H, [None] * H
    for p in range(H // 2):
        q_pair = q_ref[:, p * 2 * DK:(p + 1) * 2 * DK]
        q_lo = jnp.where(lane_pair < DK, q_pair, 0.0)
        q_hi = jnp.where(lane_pair < DK, 0.0, q_pair)
        qm2 = jnp.concatenate([q_lo, q_hi], axis=0).astype(BF16)
        k_pair = (k_ref[:, p * 2 * DK:(p + 1) * 2 * DK] * (DK ** -0.5)).astype(BF16)
        s2 = lax.dot_general(qm2, k_pair, NT_DIMS, preferred_element_type=F32)
        caug_pair = caug_ref[p]
        i2 = jnp.dot(qm2, caug_pair.astype(BF16), preferred_element_type=F32)
        for half in range(2):
            h = 2 * p + half
            S[h] = s2[half * L:(half + 1) * L] * W[h]
            inter[h] = i2[half * L:(half + 1) * L]
            caug_prev[h] = caug_pair[half * DK:(half + 1) * DK, :]
    yield

    m_new = [m_t[h][L - 1:L, :] for h in heads]
    b_last = [cs[L - 1:L, H + h:H + h + 1] for h in heads]
    R = []
    for h in heads:
        w_s_row = jnp.exp(b_last[h] - b_row[h] + i_row[h] - m_new[h])
        kwT = (kT[h * DK:(h + 1) * DK, :] * w_s_row).astype(BF16)
        vaug = jnp.concatenate([v_ref[:, h * DV:(h + 1) * DV], ones], axis=1)
        lhs = jnp.concatenate([S[h].astype(BF16), kwT], axis=0)
        R.append(jnp.dot(lhs, vaug, preferred_element_type=F32))
    yield

    hh = []
    for h in heads:
        numden = w_inter[h] * inter[h] + R[h][:L]
        num, den = numden[:, :DV], numden[:, DV:]
        hh.append(num / jnp.maximum(jnp.abs(den), jnp.exp(-m_t[h])))
    ms = [jnp.mean(hh[h] * hh[h], axis=-1, keepdims=True) for h in heads]
    for h in heads:
        hn = hh[h] * lax.rsqrt(ms[h] + EPS)
        og = jax.nn.sigmoid(o_ref[:, h * DV:(h + 1) * DV])
        hg_ref[:, h * DV:(h + 1) * DV] = (og * (hn * gh_ref[:, h * DV:(h + 1) * DV])).astype(BF16)

    for h in heads:
        p, r0 = h // 2, (h % 2) * DK
        w_c = jnp.exp(b_last[h] + m_prev[h] - m_new[h])
        caug_ref[p, r0:r0 + DK, :] = w_c * caug_prev[h] + R[h][L:]
        mst_ref[h:h + 1, :] = jnp.broadcast_to(m_new[h], (1, LANES))


def _mlstm_write_state(caug_ref, mst_ref, c_out_ref, n_out_ref, m_out_ref, *, H, DK, DV):
    pick0 = (lax.broadcasted_iota(jnp.int32, (8, DV), 1) == 0).astype(F32)
    for h in range(H):
        p, r0 = h // 2, (h % 2) * DK
        ca = caug_ref[p, r0:r0 + DK, :]
        c_out_ref[0, h] = ca[:, :DV]
        n_rows = lax.dot_general(pick0, ca[:, DV:], NT_DIMS, precision=lax.Precision.HIGHEST,
                                 preferred_element_type=F32)
        n_out_ref[0, h:h + 1, :] = n_rows[0:1, :]
        m_out_ref[0, :, h:h + 1] = mst_ref[h:h + 1, 0:1]


def _mlstm_step_kernel(q_ref, k_ref, v_ref, o_ref, gi_ref, gf_ref, bi_ref, bf_ref, gh_ref,
                       c0_ref, n0_ref, m0_ref,
                       hg_ref, c_out_ref, n_out_ref, m_out_ref, *, H, DK, DV, TB):
    RB = TB * H
    i_pre = gi_ref[...] + bi_ref[...]
    log_f = _log_sigmoid(gf_ref[...] + bf_ref[...])
    log_inter = log_f + m0_ref[...]
    m_t = jnp.maximum(log_inter, i_pre)
    w_inter = jnp.exp(log_inter - m_t)
    w_intra = jnp.exp(i_pre - m_t)

    q = q_ref[...].astype(F32)
    k = k_ref[...].astype(F32) * (DK ** -0.5)
    v = v_ref[...].astype(F32)
    n_prev = n0_ref[...]
    s = jnp.sum(q * k, axis=1, keepdims=True) * w_intra
    den = w_inter * jnp.sum(q * n_prev, axis=1, keepdims=True) + s

    def block_diag(x):
        wide = jnp.concatenate([x] * H, axis=1)
        head_of_lane = lax.broadcasted_iota(jnp.int32, wide.shape, 1) // DK
        head_of_row = lax.broadcasted_iota(jnp.int32, wide.shape, 0) % H
        return jnp.where(head_of_lane == head_of_row, wide, 0.0)

    q_bd = block_diag(q)
    kw_bd = block_diag(k * w_intra[:, :DK])
    qc = jnp.concatenate(
        [jnp.dot(q_bd[t * H:(t + 1) * H, :], c0_ref[t], preferred_element_type=F32) for t in range(TB)],
        axis=0)

    num = w_inter * qc + s * v
    hh = num / jnp.maximum(jnp.abs(den), jnp.exp(-m_t))
    hn = hh * lax.rsqrt(jnp.mean(hh * hh, axis=-1, keepdims=True) + EPS)
    hg_ref[...] = (jax.nn.sigmoid(o_ref[...]) * (hn * gh_ref[...])).astype(BF16)
    n_out_ref[...] = w_inter[:, :DK] * n_prev + w_intra[:, :DK] * k
    m_out_ref[...] = m_t

    for t in range(TB):
        rows = slice(t * H, (t + 1) * H)
        d_c = lax.dot_general(kw_bd[rows, :], v[rows, :], (((0,), (0,)), ((), ())),
                              preferred_element_type=F32)
        for h in range(H):
            blk = slice(h * DK, (h + 1) * DK)
            c_out_ref[t, blk, :] = w_inter[t * H + h:t * H + h + 1, :] * c0_ref[t, blk, :] + d_c[blk, :]


def _mlstm_sample(qkv, og, b_gates, g_head, c0, n0, m0, TB):
    H, DK, DV = ML_HEADS, 64, 128
    B = qkv.shape[0]
    R, RB = B * H, TB * H
    qk_w, v_w = H * DK, H * DV
    lanes = lambda x: jnp.broadcast_to(x.reshape(-1, 1), (x.size, LANES))
    per_block = lambda x: jnp.tile(x, (TB, 1))
    q = qkv[:, :qk_w].reshape(R, DK)
    k = qkv[:, qk_w:2 * qk_w].reshape(R, DK)
    v = qkv[:, 2 * qk_w:].reshape(R, DV)
    o = og[:, :v_w].reshape(R, DV)
    operands = (
        q, k, v, o,
        lanes(og[:, v_w:v_w + H]), lanes(og[:, v_w + H:v_w + 2 * H]),
        per_block(lanes(b_gates[:H])), per_block(lanes(b_gates[H:])),
        per_block(g_head.reshape(H, DV)),
        c0.reshape(B, H * DK, DV), n0.reshape(R, DK), lanes(m0),
    )
    row_blk = lambda n: pl.BlockSpec((RB, n), lambda b: (b, 0))
    state_blk = pl.BlockSpec((TB, H * DK, DV), lambda b: (b, 0, 0))
    kern = functools.partial(_mlstm_step_kernel, H=H, DK=DK, DV=DV, TB=TB)
    hg, c_new, n_new, m_new = pl.pallas_call(
        kern,
        out_shape=(
            jax.ShapeDtypeStruct((R, DV), BF16),
            jax.ShapeDtypeStruct((B, H * DK, DV), F32),
            jax.ShapeDtypeStruct((R, DK), F32),
            jax.ShapeDtypeStruct((R, LANES), F32),
        ),
        grid=(B // TB,),
        in_specs=[
            row_blk(DK), row_blk(DK), row_blk(DV), row_blk(DV),
            row_blk(LANES), row_blk(LANES),
            _resident((RB, LANES)), _resident((RB, LANES)), _resident((RB, DV)),
            state_blk, row_blk(DK), row_blk(LANES),
        ],
        out_specs=(row_blk(DV), state_blk, row_blk(DK), row_blk(LANES)),
        compiler_params=_params(1),
        name="mlstm_sample",
    )(*operands)
    return (hg.reshape(B, v_w), c_new.reshape(B, H, DK, DV), n_new.reshape(B, H, DK),
            m_new[:, 0].reshape(B, H))


def _ffn_kernel(h_ref, a_ref, wa_ref, gn_ref, wgu_ref, wd_ref, gf_ref, o_ref, *, d_ff, tf, final_norm, a_tile):
    if a_tile:
        proj = jnp.concatenate(
            [lax.dot_general(a_ref[t], wa_ref[...], (((0,), (0,)), ((), ())), preferred_element_type=F32)
             for t in range(a_ref.shape[0])], axis=0)
    else:
        proj = jnp.dot(a_ref[...], wa_ref[...], preferred_element_type=F32)
    h1 = h_ref[...] + proj
    xn = _rms(h1, gn_ref[...]).astype(BF16)
    acc = h1
    for c in range(d_ff // tf):
        g = jnp.dot(xn, wgu_ref[:, c * tf:(c + 1) * tf], preferred_element_type=F32)
        u = jnp.dot(xn, wgu_ref[:, d_ff + c * tf:d_ff + (c + 1) * tf], preferred_element_type=F32)
        act = (g * jax.nn.sigmoid(g) * u).astype(BF16)
        acc = acc + jnp.dot(act, wd_ref[c * tf:(c + 1) * tf, :], preferred_element_type=F32)
    if final_norm:
        acc = _rms(acc, gf_ref[...])
    o_ref[...] = acc


def _mixer_out_ffn(h, a, w_a, g_ffn, w_gu, w_d, layer, g_final, tm, final_norm):
    M, D = h.shape
    KA = w_a.shape[0]
    d_ff = w_d.shape[1]
    a_tile = a.shape[2] if a.ndim == 3 else 0
    kern = functools.partial(_ffn_kernel, d_ff=d_ff, tf=256, final_norm=final_norm, a_tile=a_tile)
    single = pl.Buffered(1)
    a_spec = (pl.BlockSpec((tm // a_tile, KA, a_tile), lambda i: (i, 0, 0)) if a_tile
              else pl.BlockSpec((tm, KA), lambda i: (i, 0)))
    return pl.pallas_call(
        kern,
        out_shape=jax.ShapeDtypeStruct((M, D), F32),
        grid=(M // tm,),
        in_specs=[
            pl.BlockSpec((tm, D), lambda i: (i, 0)),
            a_spec,
            pl.BlockSpec((KA, D), lambda i: (0, 0), pipeline_mode=single),
            _resident((1, D)),
            pl.BlockSpec((None, D, 2 * d_ff), lambda i: (layer, 0, 0), pipeline_mode=single),
            pl.BlockSpec((None, d_ff, D), lambda i: (layer, 0, 0), pipeline_mode=single),
            _resident((1, D)),
        ],
        out_specs=pl.BlockSpec((tm, D), lambda i: (i, 0)),
        compiler_params=_params(1),
        name="mixer_out_ffn",
    )(h, a, w_a, g_ffn, w_gu, w_d, g_final)


def _rope_slot(x, cos, sin):
    return x * cos + pltpu.roll(x, 32, 1) * sin


def _mla_proj_kernel(h_ref, gn_ref, win_ref, gq_ref, gkv_ref, wuq_ref, wukt_ref,
                     cos_ref, sin_ref,
                     q_ref, kcat_ref, lat_ref, kr_ref, *maybe_vt_ref, H, vt_tile):
    xn = _rms(h_ref[...], gn_ref[...]).astype(BF16)
    t = jnp.dot(xn, win_ref[...], preferred_element_type=F32)
    c_q = t[:, :Q_LORA]
    c_kv = t[:, Q_LORA:Q_LORA + KV_LORA]
    k_slot = t[:, Q_LORA + KV_LORA:]
    cos, sin = cos_ref[...], sin_ref[...]

    lat = _rms(c_kv, gkv_ref[...])
    k_rot = _rope_slot(k_slot, cos, sin)
    lat_ref[...] = lat
    kr_ref[...] = k_rot[:, :QK_ROPE]
    kcat_ref[:, :KV_LORA] = lat.astype(BF16)
    kcat_ref[:, KV_LORA:] = k_rot.astype(BF16)
    if vt_tile:
        (vt_ref,) = maybe_vt_ref
        for t in range(lat.shape[0] // vt_tile):
            vt_ref[t] = lat[t * vt_tile:(t + 1) * vt_tile, :].T.astype(BF16)

    cqn = _rms(c_q, gq_ref[...]).astype(BF16)
    tm = cqn.shape[0]

    def store_q(h, lanes, val):
        if vt_tile:
            sub = vt_tile // 2
            for t in range(tm // vt_tile):
                for half in range(2):
                    src = slice(t * vt_tile + half * sub, t * vt_tile + (half + 1) * sub)
                    dst = slice((half * H + h) * sub, (half * H + h + 1) * sub)
                    q_ref[t, dst, lanes] = val[src].astype(BF16)
        else:
            q_ref[:, slice(h * Q_SLOT + lanes.start, h * Q_SLOT + lanes.stop)] = val.astype(BF16)

    q_rope = jnp.dot(cqn, wuq_ref[:, H * QK_NOPE:], preferred_element_type=F32)
    for h in range(H):
        store_q(h, slice(KV_LORA, Q_SLOT), _rope_slot(q_rope[:, h * LANES:(h + 1) * LANES], cos, sin))
    q_nope = jnp.dot(cqn, wuq_ref[:, :H * QK_NOPE], preferred_element_type=F32)
    for h in range(H):
        q_lat = jnp.dot(q_nope[:, h * QK_NOPE:(h + 1) * QK_NOPE].astype(BF16), wukt_ref[h],
                        preferred_element_type=F32)
        store_q(h, slice(0, KV_LORA), q_lat)


def _mla_project(h, g_norm, w_in, g_q, g_kv, w_uq, w_ukt, cos, sin, tm, table_blocks, vt_tile):
    M, D = h.shape
    H = MLA_HEADS
    kern = functools.partial(_mla_proj_kernel, H=H, vt_tile=vt_tile)
    table = pl.BlockSpec((tm, LANES), lambda i: (i % table_blocks, 0))
    out_shape = [
        jax.ShapeDtypeStruct((M, H * Q_SLOT), BF16),
        jax.ShapeDtypeStruct((M, Q_SLOT), BF16),
        jax.ShapeDtypeStruct((M, KV_LORA), F32),
        jax.ShapeDtypeStruct((M, QK_ROPE), F32),
    ]
    out_specs = [
        pl.BlockSpec((tm, H * Q_SLOT), lambda i: (i, 0)),
        pl.BlockSpec((tm, Q_SLOT), lambda i: (i, 0)),
        pl.BlockSpec((tm, KV_LORA), lambda i: (i, 0)),
        pl.BlockSpec((tm, QK_ROPE), lambda i: (i, 0)),
    ]
    if vt_tile:
        out_shape[0] = jax.ShapeDtypeStruct((M // vt_tile, H * vt_tile, Q_SLOT), BF16)
        out_specs[0] = pl.BlockSpec((tm // vt_tile, H * vt_tile, Q_SLOT), lambda i: (i, 0, 0))
        out_shape.append(jax.ShapeDtypeStruct((M // vt_tile, KV_LORA, vt_tile), BF16))
        out_specs.append(pl.BlockSpec((tm // vt_tile, KV_LORA, vt_tile), lambda i: (i, 0, 0)))
    return pl.pallas_call(
        kern,
        out_shape=tuple(out_shape),
        grid=(M // tm,),
        in_specs=[
            pl.BlockSpec((tm, D), lambda i: (i, 0)),
            _resident((1, D)),
            _resident(w_in.shape),
            _resident((1, Q_LORA)),
            _resident((1, KV_LORA)),
            _resident(w_uq.shape),
            _resident(w_ukt.shape),
            table, table,
        ],
        out_specs=tuple(out_specs),
        compiler_params=_params(1),
        name="mla_project",
    )(h, g_norm, w_in, g_q, g_kv, w_uq, w_ukt, cos, sin)


def _attn_kernel(q_ref, k_ref, vt_ref, wuvt_ref, o_ref, m_ref, l_ref, acc_ref, sa_ref, sb_ref, *, H, T):
    i = pl.program_id(1)

    def scores(j):
        kj = k_ref[0, pl.ds(pl.multiple_of(j * T, T), T), :]
        return lax.dot_general(kj, q_ref[0], NT_DIMS, preferred_element_type=F32) * MLA_SCALE

    def update(j, st):
        vtj = vt_ref[j]
        m_prev = m_ref[...]
        m_new = jnp.maximum(m_prev, jnp.max(st, axis=0, keepdims=True))
        alpha = jnp.exp(m_prev - m_new)
        p = jnp.exp(st - m_new)
        l_ref[...] = alpha * l_ref[...] + jnp.sum(p, axis=0, keepdims=True)
        acc_ref[...] = alpha * acc_ref[...] + jnp.dot(vtj, p.astype(BF16), preferred_element_type=F32)
        m_ref[...] = m_new

    SUB, C2 = T // 2, H * T // 2
    base = pl.multiple_of(i * T, T)
    k_a = k_ref[0, pl.ds(base, SUB), :]
    k_b = k_ref[0, pl.ds(base + SUB, SUB), :]
    vt_i = vt_ref[i]
    vt_a, vt_b = vt_i[:, :SUB], vt_i[:, SUB:]
    s_a = lax.dot_general(k_a, q_ref[0], NT_DIMS, preferred_element_type=F32) * MLA_SCALE
    s_b = lax.dot_general(k_b, q_ref[0, C2:, :], NT_DIMS, preferred_element_type=F32) * MLA_SCALE
    key = lax.broadcasted_iota(jnp.int32, (SUB, C2), 0)
    qry = lax.broadcasted_iota(jnp.int32, (SUB, C2), 1) & (SUB - 1)
    causal = key <= qry
    s_a0 = jnp.where(causal, s_a[:, :C2], -jnp.inf)
    s_a1 = s_a[:, C2:]
    s_b1 = jnp.where(causal, s_b, -jnp.inf)
    m_0 = jnp.max(s_a0, axis=0, keepdims=True)
    m_1 = jnp.maximum(jnp.max(s_a1, axis=0, keepdims=True), jnp.max(s_b1, axis=0, keepdims=True))
    p_a0, p_a1, p_b1 = jnp.exp(s_a0 - m_0), jnp.exp(s_a1 - m_1), jnp.exp(s_b1 - m_1)
    m_ref[:, :C2] = m_0
    m_ref[:, C2:] = m_1
    l_ref[:, :C2] = jnp.sum(p_a0, axis=0, keepdims=True)
    l_ref[:, C2:] = jnp.sum(p_a1, axis=0, keepdims=True) + jnp.sum(p_b1, axis=0, keepdims=True)
    acc_ref[:, :C2] = jnp.dot(vt_a, p_a0.astype(BF16), preferred_element_type=F32)
    acc_ref[:, C2:] = (jnp.dot(vt_a, p_a1.astype(BF16), preferred_element_type=F32)
                       + jnp.dot(vt_b, p_b1.astype(BF16), preferred_element_type=F32))

    @pl.when(i > 0)
    def _():
        sa_ref[...] = scores(0)

    def pair(jj, carry):
        j = 2 * jj
        sb_ref[...] = scores(j + 1)
        update(j, sa_ref[...])
        sa_ref[...] = scores(jnp.minimum(j + 2, i - 1))
        update(j + 1, sb_ref[...])
        return carry

    lax.fori_loop(0, i // 2, pair, 0)

    @pl.when(i % 2 == 1)
    def _():
        update(i - 1, sa_ref[...])

    o_t = (acc_ref[...] / l_ref[...]).astype(BF16)
    for half in range(2):
        for h in range(H):
            cols = slice((half * H + h) * SUB, (half * H + h + 1) * SUB)
            v_t = jnp.dot(wuvt_ref[h], o_t[:, cols], preferred_element_type=F32)
            o_ref[0, h * V_DIM:(h + 1) * V_DIM, half * SUB:(half + 1) * SUB] = v_t.astype(BF16)


def _attention_prompt(q, kcat, vt, w_uvt, B, S, T):
    H = MLA_HEADS
    NQ = S // T
    kern = functools.partial(_attn_kernel, H=H, T=T)
    return pl.pallas_call(
        kern,
        out_shape=jax.ShapeDtypeStruct((B * NQ, H * V_DIM, T), BF16),
        grid=(B, NQ),
        in_specs=[
            pl.BlockSpec((1, H * T, Q_SLOT), lambda b, i: (b * NQ + i, 0, 0)),
            pl.BlockSpec((1, S, Q_SLOT), lambda b, i: (b, 0, 0)),
            pl.BlockSpec((NQ, KV_LORA, T), lambda b, i: (b, 0, 0)),
            _resident(w_uvt.shape),
        ],
        out_specs=pl.BlockSpec((1, H * V_DIM, T), lambda b, i: (b * NQ + i, 0, 0)),
        scratch_shapes=[
            pltpu.VMEM((1, H * T), F32),
            pltpu.VMEM((1, H * T), F32),
            pltpu.VMEM((KV_LORA, H * T), F32),
            pltpu.VMEM((T, H * T), F32),
            pltpu.VMEM((T, H * T), F32),
        ],
        compiler_params=_params(2),
        name="attention_prompt",
    )(q, kcat.reshape(B, S, Q_SLOT), vt, w_uvt)


def _decode_mlstm_kernel(pt_ref,
                         q_ref, cn_ref, krn_ref, lat_hbm, krt_hbm,
                         mq_ref, mk_ref, mv_ref, mo_ref, gt_ref, bias_ref, gh_ref,
                         o_ref, hg_ref, c_out_ref, n_out_ref, m_out_ref,
                         lat_buf, kr_buf, sem, caug_ref, mst_ref,
                         *, layer, G, P, NCH, RING, STREAMS, L, H_ML, DK, DV, NC):
    s = pl.program_id(0)
    ns = pl.num_programs(0)
    total = ns * NCH
    ml = dict(L=L, H=H_ML, DK=DK, DV=DV)

    def page_copies(c):
        slot = lax.rem(c, RING)
        cw = jnp.where(c >= total, c - total, c)
        bb, jj = lax.div(cw, NCH), lax.rem(cw, NCH)
        copies = []
        for g in range(G):
            page = pt_ref[bb, jj * G + g]
            copies.append(pltpu.make_async_copy(
                lat_hbm.at[layer, page], lat_buf.at[slot, pl.ds(g * P, P), :], sem.at[0, slot]))
            copies.append(pltpu.make_async_copy(
                krt_hbm.at[layer, page], kr_buf.at[slot, :, pl.ds(g * P, P)], sem.at[1, slot]))
        return copies

    def start(c):
        for cp in page_copies(c):
            cp.start()

    def wait(c):
        for cp in page_copies(c):
            cp.wait()

    @pl.when(s == 0)
    def _():
        for c in range(RING - 1):
            start(jnp.int32(c))

    ml_chunk = lax.rem(s, NC)

    @pl.when(ml_chunk == 0)
    def _():
        _mlstm_reset(caug_ref, mst_ref)

    q = q_ref[0].astype(F32)
    q_lat = q[:, :KV_LORA]
    q_rope = q[:, KV_LORA:KV_LORA + QK_ROPE]
    c_new = cn_ref[0]
    kr_new = krn_ref[0]

    s_new = (jnp.sum(q_lat * c_new, axis=1, keepdims=True)
             + jnp.sum(q_rope * kr_new, axis=1, keepdims=True)) * MLA_SCALE
    n_heads = q.shape[0]
    m_run = [s_new] + [jnp.full_like(s_new, -jnp.inf)] * (STREAMS - 1)
    l_run = [jnp.ones_like(s_new)] + [jnp.zeros_like(s_new)] * (STREAMS - 1)
    acc = ([jnp.broadcast_to(c_new, (n_heads, KV_LORA)).astype(F32)]
           + [jnp.zeros((n_heads, KV_LORA), F32)] * (STREAMS - 1))
    W = (G * P) // STREAMS

    ml_pieces = _mlstm_chunk_phases(mq_ref, mk_ref, mv_ref, mo_ref, gt_ref, bias_ref, gh_ref, hg_ref,
                                    caug_ref, mst_ref, **ml)
    assert NCH >= 2
    for j in range(NCH):
        c = s * NCH + j
        slot = lax.rem(c, RING)
        wait(c)
        kl = [lat_buf[slot, i * W:(i + 1) * W, :] for i in range(STREAMS)]
        sc = [(lax.dot_general(q_lat, kl[i], NT_DIMS, preferred_element_type=F32)
               + jnp.dot(q_rope, kr_buf[slot, :, i * W:(i + 1) * W], preferred_element_type=F32)) * MLA_SCALE
              for i in range(STREAMS)]
        if j in (0, NCH - 1):
            next(ml_pieces)
        for i in range(STREAMS):
            m_new = jnp.maximum(m_run[i], jnp.max(sc[i], axis=1, keepdims=True))
            alpha = jnp.exp(m_run[i] - m_new)
            p = jnp.exp(sc[i] - m_new)
            m_run[i] = m_new
            l_run[i] = alpha * l_run[i] + jnp.sum(p, axis=1, keepdims=True)
            acc[i] = alpha * acc[i] + jnp.dot(p, kl[i], preferred_element_type=F32)
        if j in (0, NCH - 1):
            next(ml_pieces, None)
        start(c + (RING - 1))

    m_all = functools.reduce(jnp.maximum, m_run)
    scale = [jnp.exp(m_i - m_all) for m_i in m_run]
    l_all = sum(l_i * w_i for l_i, w_i in zip(l_run, scale))
    acc_all = sum(a_i * w_i for a_i, w_i in zip(acc, scale))
    o_ref[0] = acc_all / l_all

    @pl.when(ml_chunk == NC - 1)
    def _():
        _mlstm_write_state(caug_ref, mst_ref, c_out_ref, n_out_ref, m_out_ref, H=H_ML, DK=DK, DV=DV)

    @pl.when(s == ns - 1)
    def _():
        for c in range(RING - 1):
            wait(total + c)


def _attention_sample_with_mlstm_prompt(q3, c_new, kr_new, cache_latent, cache_k_rope_t, page_table, layer, G,
                                        qkv, og, bias, g_head, B, S, L):
    BS, H, _ = q3.shape
    n_pages = page_table.shape[1]
    P = cache_latent.shape[2]
    NCH = n_pages // G
    assert n_pages % G == 0
    RING = 3
    H_ML, DK, DV = ML_HEADS, 64, 128
    NC = S // L
    assert B * NC == BS, "one mLSTM chunk per decode sequence"
    qk_w, v_w = H_ML * DK, H_ML * DV
    gate_blk = v_w // LANES
    kern = functools.partial(_decode_mlstm_kernel, layer=layer, G=G, P=P, NCH=NCH, RING=RING, STREAMS=2,
                             L=L, H_ML=H_ML, DK=DK, DV=DV, NC=NC)
    grid_spec = pltpu.PrefetchScalarGridSpec(
        num_scalar_prefetch=1,
        grid=(BS,),
        in_specs=[
            pl.BlockSpec((1, H, Q_SLOT), lambda s, pt: (s, 0, 0)),
            pl.BlockSpec((1, 1, KV_LORA), lambda s, pt: (s, 0, 0)),
            pl.BlockSpec((1, 1, QK_ROPE), lambda s, pt: (s, 0, 0)),
            pl.BlockSpec(memory_space=pl.ANY),
            pl.BlockSpec(memory_space=pl.ANY),
            pl.BlockSpec((L, qk_w), lambda s, pt: (s, 0)),
            pl.BlockSpec((L, qk_w), lambda s, pt: (s, 1)),
            pl.BlockSpec((L, v_w), lambda s, pt: (s, 1)),
            pl.BlockSpec((L, v_w), lambda s, pt: (s, 0)),
            pl.BlockSpec((L, LANES), lambda s, pt: (s, gate_blk)),
            pl.BlockSpec((1, LANES), lambda s, pt: (0, 0)),
            pl.BlockSpec((1, v_w), lambda s, pt: (0, 0)),
        ],
        out_specs=(
            pl.BlockSpec((1, H, KV_LORA), lambda s, pt: (s, 0, 0)),
            pl.BlockSpec((L, v_w), lambda s, pt: (s, 0)),
            pl.BlockSpec((1, H_ML, DK, DV), lambda s, pt: (s // NC, 0, 0, 0)),
            pl.BlockSpec((1, H_ML, DK), lambda s, pt: (s // NC, 0, 0)),
            pl.BlockSpec((1, 1, H_ML), lambda s, pt: (s // NC, 0, 0)),
        ),
        scratch_shapes=[
            pltpu.VMEM((RING, G * P, KV_LORA), F32),
            pltpu.VMEM((RING, QK_ROPE, G * P), F32),
            pltpu.SemaphoreType.DMA((2, RING)),
            pltpu.VMEM((H_ML // 2, 2 * DK, 2 * DV), F32),
            pltpu.VMEM((H_ML, LANES), F32),
        ],
    )
    return pl.pallas_call(
        kern,
        out_shape=(
            jax.ShapeDtypeStruct((BS, H, KV_LORA), F32),
            jax.ShapeDtypeStruct((B * S, v_w), BF16),
            jax.ShapeDtypeStruct((B, H_ML, DK, DV), F32),
            jax.ShapeDtypeStruct((B, H_ML, DK), F32),
            jax.ShapeDtypeStruct((B, 1, H_ML), F32),
        ),
        grid_spec=grid_spec,
        compiler_params=_params(1),
        name="attention_sample_mlstm_prompt",
    )(page_table, q3, c_new.reshape(BS, 1, KV_LORA), kr_new.reshape(BS, 1, QK_ROPE),
      cache_latent, cache_k_rope_t, qkv, qkv, qkv, og, og, bias, g_head)


def _value_up_kernel(o_ref, wuv_ref, v_ref, *, H):
    for h in range(H):
        o_h = o_ref[:, h * KV_LORA:(h + 1) * KV_LORA].astype(BF16)
        v_ref[:, h * V_DIM:(h + 1) * V_DIM] = jnp.dot(
            o_h, wuv_ref[h], preferred_element_type=F32).astype(BF16)


def _value_up(o_lat, w_uv):
    M = o_lat.shape[0]
    H = MLA_HEADS
    return pl.pallas_call(
        functools.partial(_value_up_kernel, H=H),
        out_shape=jax.ShapeDtypeStruct((M, H * V_DIM), BF16),
        grid=(1,),
        in_specs=[_resident(o_lat.shape), _resident(w_uv.shape)],
        out_specs=_resident((M, H * V_DIM)),
        compiler_params=_params(1),
        name="value_up",
    )(o_lat, w_uv)


def _rope_tables(pos):
    inv = ROPE_THETA ** (-jnp.arange(0, QK_ROPE, 2, dtype=F32) / QK_ROPE)
    ang = pos[:, None] * inv[None, :]
    cos, sin = jnp.cos(ang), jnp.sin(ang)
    z = jnp.zeros_like(cos)
    return jnp.concatenate([cos, cos, z, z], axis=1), jnp.concatenate([-sin, sin, z, z], axis=1)


def _pad_cols(w, n):
    return jnp.pad(w, ((0, 0), (0, n - w.shape[1])))


def kernel(x_prompt, x_sample, state_mlstm_C, state_mlstm_n, state_mlstm_m, cache_latent, cache_k_rope,
           page_table, norm_mix, norm_ffn, norm_final, mlstm_w_in, mlstm_b_gates, mlstm_g_head, mlstm_w_out,
           mla_w_in, mla_g_q, mla_g_kv, mla_w_uq, mla_w_uk, mla_w_uv, mla_w_o, ffn_w_gate_up, ffn_w_down):
    B, S, D = x_prompt.shape
    BS, T, _ = x_sample.shape
    assert T == 1, "sample group is one new token per sequence"
    depth = norm_mix.shape[0]
    H = MLA_HEADS
    past_len = page_table.shape[1] * cache_latent.shape[2]

    hp = x_prompt.reshape(B * S, D)
    hs = x_sample.reshape(BS, D)
    row = lambda v: v.reshape(1, -1).astype(F32)

    ROW_TILE = 1024
    ML_CHUNK = 256
    T_ATT = 512
    PAGES_PER_CHUNK = 64
    SAMPLE_SEQS_PER_STEP = 16

    pos_p = jnp.arange(S, dtype=F32)
    pos_s = jnp.broadcast_to(jnp.arange(T, dtype=F32) + past_len, (BS,))
    rope_p = _rope_tables(pos_p)
    rope_s = _rope_tables(pos_s)

    assert depth == 2, "schedule below is written for one mLSTM layer followed by one MLA layer"
    g_fin = row(norm_final)

    n_gates = 2 * ML_HEADS
    ml_w_in = _pad_cols(mlstm_w_in[0], mlstm_w_in.shape[2] - n_gates + LANES).astype(BF16)
    ml_bias = _pad_cols(mlstm_b_gates[0].reshape(1, -1), LANES).astype(F32)
    ml_g_head = row(mlstm_g_head[0])
    ml_w_out = mlstm_w_out[0].astype(BF16)
    mla_in = jnp.concatenate([mla_w_in[0], mla_w_in[0][:, Q_LORA + KV_LORA:]], axis=1).astype(BF16)
    wq = mla_w_uq[0].reshape(Q_LORA, H, QK_NOPE + QK_ROPE)
    wq_nope = wq[:, :, :QK_NOPE].reshape(Q_LORA, H * QK_NOPE)
    wq_rope = jnp.concatenate([wq[:, :, QK_NOPE:]] * 2, axis=2).reshape(Q_LORA, H * LANES)
    w_uq = jnp.concatenate([wq_nope, wq_rope], axis=1).astype(BF16)
    w_ukt = jnp.transpose(mla_w_uk[0].reshape(KV_LORA, H, QK_NOPE), (1, 2, 0)).astype(BF16)
    w_uv = jnp.transpose(mla_w_uv[0].reshape(KV_LORA, H, V_DIM), (1, 0, 2)).astype(BF16)
    w_uvt = jnp.transpose(mla_w_uv[0].reshape(KV_LORA, H, V_DIM), (1, 2, 0)).astype(BF16)
    w_o = mla_w_o[0].astype(BF16)
    g_q, g_kv = row(mla_g_q[0]), row(mla_g_kv[0])
    mla_weights = (mla_in, g_q, g_kv, w_uq, w_ukt)
    w_gu_all, w_d_all = ffn_w_gate_up.astype(BF16), ffn_w_down.astype(BF16)
    ffn = [(row(norm_ffn[l]), w_gu_all, w_d_all, l) for l in range(depth)]
    g_mix = [row(norm_mix[l]) for l in range(depth)]
    cache_k_rope_t = jnp.swapaxes(cache_k_rope, 2, 3)

    n_qkv = 2 * ML_HEADS * 64 + ML_HEADS * 128
    qkv_s, og_s = _norm_matmul(hs, g_mix[0], ml_w_in, BS, n_qkv)
    a_s, C_s, n_s, m_s = _mlstm_sample(qkv_s, og_s, mlstm_b_gates[0].astype(F32), ml_g_head, state_mlstm_C[0],
                                       state_mlstm_n[0], state_mlstm_m[0], TB=SAMPLE_SEQS_PER_STEP)
    hs = _mixer_out_ffn(hs, a_s, ml_w_out, *ffn[0], g_fin, BS, final_norm=False)
    q_s, _, lat_s, kr_s = _mla_project(hs, g_mix[1], *mla_weights, *rope_s, tm=BS, table_blocks=1, vt_tile=0)

    qkv_p, og_p = _norm_matmul(hp, g_mix[0], ml_w_in, ROW_TILE, n_qkv)
    o_s, a_p, C_p, n_p, m_p = _attention_sample_with_mlstm_prompt(
        q_s.reshape(BS, H, Q_SLOT), lat_s, kr_s, cache_latent, cache_k_rope_t, page_table, layer=0,
        G=PAGES_PER_CHUNK, qkv=qkv_p, og=og_p, bias=ml_bias, g_head=ml_g_head, B=B, S=S, L=ML_CHUNK)
    hp = _mixer_out_ffn(hp, a_p, ml_w_out, *ffn[0], g_fin, ROW_TILE, final_norm=False)

    q_p, kcat_p, lat_p, kr_p, vt_p = _mla_project(hp, g_mix[1], *mla_weights, *rope_p,
                                                  tm=ROW_TILE, table_blocks=S // ROW_TILE, vt_tile=T_ATT)
    a_p = _attention_prompt(q_p, kcat_p, vt_p, w_uvt, B, S, T=T_ATT)
    hp = _mixer_out_ffn(hp, a_p, w_o, *ffn[1], g_fin, ROW_TILE, final_norm=True)

    a_s = _value_up(o_s.reshape(BS, H * KV_LORA), w_uv)
    hs = _mixer_out_ffn(hs, a_s, w_o, *ffn[1], g_fin, BS, final_norm=True)

    return (hp.reshape(B, S, D), hs.reshape(BS, T, D),
            C_p[None], n_p[None], m_p.reshape(1, B, -1),
            C_s[None], n_s[None], m_s.reshape(1, BS, -1),
            lat_p.reshape(1, B, S, KV_LORA), kr_p.reshape(1, B, S, QK_ROPE),
            lat_s.reshape(1, BS, T, KV_LORA), kr_s.reshape(1, BS, T, QK_ROPE))
```

```python
import functools

import jax
import jax.numpy as jnp
from jax import lax
from jax.experimental import pallas as pl
from jax.experimental.pallas import tpu as pltpu

F32 = jnp.float32
BF16 = jnp.bfloat16
EPS = 1e-6
ROPE_THETA = 10000.0

V7X_VMEM_BYTES = 64 * 1024 * 1024
LANES = 128
VMEM_LIMIT_BYTES = V7X_VMEM_BYTES - 8 * 1024 * 1024

ML_HEADS = 8
MLA_HEADS = 8
Q_LORA = 384
KV_LORA = 256
QK_NOPE = 128
QK_ROPE = 64
V_DIM = 128
Q_SLOT = KV_LORA + LANES
MLA_SCALE = (QK_NOPE + QK_ROPE) ** -0.5

NT_DIMS = (((1,), (1,)), ((), ()))


def _params(n_grid_axes):
    return pltpu.CompilerParams(
        dimension_semantics=("arbitrary",) * n_grid_axes,
        vmem_limit_bytes=VMEM_LIMIT_BYTES,
    )


def _rms(x, g):
    return x * lax.rsqrt(jnp.mean(x * x, axis=-1, keepdims=True) + EPS) * g


def _log_sigmoid(x):
    return jnp.minimum(x, 0.0) - jnp.log1p(jnp.exp(-jnp.abs(x)))


def _resident(shape):
    nd = len(shape)
    return pl.BlockSpec(shape, lambda *_: (0,) * nd)


def _norm_matmul_kernel(x_ref, g_ref, w_ref, lo_ref, hi_ref):
    xn = _rms(x_ref[...], g_ref[...]).astype(BF16)
    y = jnp.dot(xn, w_ref[...], preferred_element_type=F32)
    n_lo = lo_ref.shape[1]
    lo_ref[...] = y[:, :n_lo].astype(BF16)
    hi_ref[...] = y[:, n_lo:]


def _norm_matmul(x, g, w, tm, n_lo):
    M, D = x.shape
    N = w.shape[1]
    return pl.pallas_call(
        _norm_matmul_kernel,
        out_shape=(jax.ShapeDtypeStruct((M, n_lo), BF16), jax.ShapeDtypeStruct((M, N - n_lo), F32)),
        grid=(M // tm,),
        in_specs=[pl.BlockSpec((tm, D), lambda i: (i, 0)), _resident((1, D)), _resident((D, N))],
        out_specs=(pl.BlockSpec((tm, n_lo), lambda i: (i, 0)), pl.BlockSpec((tm, N - n_lo), lambda i: (i, 0))),
        compiler_params=_params(1),
        name="norm_matmul",
    )(x, g, w)


def _mlstm_reset(caug_ref, mst_ref):
    caug_ref[...] = jnp.zeros_like(caug_ref)
    mst_ref[...] = jnp.zeros_like(mst_ref)


def _mlstm_chunk_phases(q_ref, k_ref, v_ref, o_ref, gt_ref, bias_ref, gh_ref, hg_ref, caug_ref, mst_ref,
                        *, L, H, DK, DV):
    gates = gt_ref[...] + bias_ref[...]
    lane = lax.broadcasted_iota(jnp.int32, gates.shape, 1)
    G = jnp.where(lane < H, gates, _log_sigmoid(gates))
    row = lax.broadcasted_iota(jnp.int32, (L, L), 0)
    col = lax.broadcasted_iota(jnp.int32, (L, L), 1)
    causal = col <= row
    cs = jnp.dot(causal.astype(F32), G, precision=lax.Precision.HIGHEST,
                 preferred_element_type=F32)
    GT = G.T
    csT = cs.T

    kT = (k_ref[...].astype(F32) * (DK ** -0.5)).T
    ones = jnp.ones((L, DV), BF16)
    heads = range(H)


    b_col = [cs[:, H + h:H + h + 1] for h in heads]
    b_row = [csT[H + h:H + h + 1, :] for h in heads]
    i_row = [GT[h:h + 1, :] for h in heads]
    r_mat = [jnp.where(causal, i_row[h] - b_row[h], -jnp.inf) for h in heads]
    r_max = [jnp.max(r_mat[h], axis=1, keepdims=True) for h in heads]
    yield

    m_prev = [mst_ref[h:h + 1, 0:1] for h in heads]
    log_inter = [b_col[h] + m_prev[h] for h in heads]
    m_t = [jnp.maximum(log_inter[h], b_col[h] + r_max[h]) for h in heads]
    w_inter = [jnp.exp(log_inter[h] - m_t[h]) for h in heads]
    W = [jnp.exp(r_mat[h] + (b_col[h] - m_t[h])) for h in heads]

    lane_pair = lax.broadcasted_iota(jnp.int32, (L, 2 * DK), 1)
    S, inter, caug_prev = [None] * H, [None] * H, [None] * H
    for p in range(H // 2):
        q_pair = q_ref[:, p * 2 * DK:(p + 1) * 2 * DK]
        q_lo = jnp.where(lane_pair < DK, q_pair, 0.0)
        q_hi = jnp.where(lane_pair < DK, 0.0, q_pair)
        qm2 = jnp.concatenate([q_lo, q_hi], axis=0).astype(BF16)
        k_pair = (k_ref[:, p * 2 * DK:(p + 1) * 2 * DK] * (DK ** -0.5)).astype(BF16)
        s2 = lax.dot_general(qm2, k_pair, NT_DIMS, preferred_element_type=F32)
        caug_pair = caug_ref[p]
        i2 = jnp.dot(qm2, caug_pair.astype(BF16), preferred_element_type=F32)
        for half in range(2):
            h = 2 * p + half
            S[h] = s2[half * L:(half + 1) * L] * W[h]
            inter[h] = i2[half * L:(half + 1) * L]
            caug_prev[h] = caug_pair[half * DK:(half + 1) * DK, :]
    yield

    m_new = [m_t[h][L - 1:L, :] for h in heads]
    b_last = [cs[L - 1:L, H + h:H + h + 1] for h in heads]
    R = []
    for h in heads:
        w_s_row = jnp.exp(b_last[h] - b_row[h] + i_row[h] - m_new[h])
        kwT = (kT[h * DK:(h + 1) * DK, :] * w_s_row).astype(BF16)
        vaug = jnp.concatenate([v_ref[:, h * DV:(h + 1) * DV], ones], axis=1)
        lhs = jnp.concatenate([S[h].astype(BF16), kwT], axis=0)
        R.append(jnp.dot(lhs, vaug, preferred_element_type=F32))
    yield

    hh = []
    for h in heads:
        numden = w_inter[h] * inter[h] + R[h][:L]
        num, den = numden[:, :DV], numden[:, DV:]
        hh.append(num / jnp.maximum(jnp.abs(den), jnp.exp(-m_t[h])))
    ms = [jnp.mean(hh[h] * hh[h], axis=-1, keepdims=True) for h in heads]
    for h in heads:
        hn = hh[h] * lax.rsqrt(ms[h] + EPS)
        og = jax.nn.sigmoid(o_ref[:, h * DV:(h + 1) * DV])
        hg_ref[:, h * DV:(h + 1) * DV] = (og * (hn * gh_ref[:, h * DV:(h + 1) * DV])).astype(BF16)

    for h in heads:
        p, r0 = h // 2, (h % 2) * DK
        w_c = jnp.exp(b_last[h] + m_prev[h] - m_new[h])
        caug_ref[p, r0:r0 + DK, :] = w_c * caug_prev[h] + R[h][L:]
        mst_ref[h:h + 1, :] = jnp.broadcast_to(m_new[h], (1, LANES))


def _mlstm_write_state(caug_ref, mst_ref, c_out_ref, n_out_ref, m_out_ref, *, H, DK, DV):
    pick0 = (lax.broadcasted_iota(jnp.int32, (8, DV), 1) == 0).astype(F32)
    for h in range(H):
        p, r0 = h // 2, (h % 2) * DK
        ca = caug_ref[p, r0:r0 + DK, :]
        c_out_ref[0, h] = ca[:, :DV]
        n_rows = lax.dot_general(pick0, ca[:, DV:], NT_DIMS, precision=lax.Precision.HIGHEST,
                                 preferred_element_type=F32)
        n_out_ref[0, h:h + 1, :] = n_rows[0:1, :]
        m_out_ref[0, :, h:h + 1] = mst_ref[h:h + 1, 0:1]


def _mlstm_step_kernel(q_ref, k_ref, v_ref, o_ref, gi_ref, gf_ref, bi_ref, bf_ref, gh_ref,
                       c0_ref, n0_ref, m0_ref,
                       hg_ref, c_out_ref, n_out_ref, m_out_ref, *, H, DK, DV, TB):
    RB = TB * H
    i_pre = gi_ref[...] + bi_ref[...]
    log_f = _log_sigmoid(gf_ref[...] + bf_ref[...])
    log_inter = log_f + m0_ref[...]
    m_t = jnp.maximum(log_inter, i_pre)
    w_inter = jnp.exp(log_inter - m_t)
    w_intra = jnp.exp(i_pre - m_t)

    q = q_ref[...].astype(F32)
    k = k_ref[...].astype(F32) * (DK ** -0.5)
    v = v_ref[...].astype(F32)
    n_prev = n0_ref[...]
    s = jnp.sum(q * k, axis=1, keepdims=True) * w_intra
    den = w_inter * jnp.sum(q * n_prev, axis=1, keepdims=True) + s

    def block_diag(x):
        wide = jnp.concatenate([x] * H, axis=1)
        head_of_lane = lax.broadcasted_iota(jnp.int32, wide.shape, 1) // DK
        head_of_row = lax.broadcasted_iota(jnp.int32, wide.shape, 0) % H
        return jnp.where(head_of_lane == head_of_row, wide, 0.0)

    q_bd = block_diag(q)
    kw_bd = block_diag(k * w_intra[:, :DK])
    qc = jnp.concatenate(
        [jnp.dot(q_bd[t * H:(t + 1) * H, :], c0_ref[t], preferred_element_type=F32) for t in range(TB)],
        axis=0)

    num = w_inter * qc + s * v
    hh = num / jnp.maximum(jnp.abs(den), jnp.exp(-m_t))
    hn = hh * lax.rsqrt(jnp.mean(hh * hh, axis=-1, keepdims=True) + EPS)
    hg_ref[...] = (jax.nn.sigmoid(o_ref[...]) * (hn * gh_ref[...])).astype(BF16)
    n_out_ref[...] = w_inter[:, :DK] * n_prev + w_intra[:, :DK] * k
    m_out_ref[...] = m_t

    for t in range(TB):
        rows = slice(t * H, (t + 1) * H)
        d_c = lax.dot_general(kw_bd[rows, :], v[rows, :], (((0,), (0,)), ((), ())),
                              preferred_element_type=F32)
        for h in range(H):
            blk = slice(h * DK, (h + 1) * DK)
            c_out_ref[t, blk, :] = w_inter[t * H + h:t * H + h + 1, :] * c0_ref[t, blk, :] + d_c[blk, :]


def _mlstm_sample(qkv, og, b_gates, g_head, c0, n0, m0, TB):
    H, DK, DV = ML_HEADS, 64, 128
    B = qkv.shape[0]
    R, RB = B * H, TB * H
    qk_w, v_w = H * DK, H * DV
    lanes = lambda x: jnp.broadcast_to(x.reshape(-1, 1), (x.size, LANES))
    per_block = lambda x: jnp.tile(x, (TB, 1))
    q = qkv[:, :qk_w].reshape(R, DK)
    k = qkv[:, qk_w:2 * qk_w].reshape(R, DK)
    v = qkv[:, 2 * qk_w:].reshape(R, DV)
    o = og[:, :v_w].reshape(R, DV)
    operands = (
        q, k, v, o,
        lanes(og[:, v_w:v_w + H]), lanes(og[:, v_w + H:v_w + 2 * H]),
        per_block(lanes(b_gates[:H])), per_block(lanes(b_gates[H:])),
        per_block(g_head.reshape(H, DV)),
        c0.reshape(B, H * DK, DV), n0.reshape(R, DK), lanes(m0),
    )
    row_blk = lambda n: pl.BlockSpec((RB, n), lambda b: (b, 0))
    state_blk = pl.BlockSpec((TB, H * DK, DV), lambda b: (b, 0, 0))
    kern = functools.partial(_mlstm_step_kernel, H=H, DK=DK, DV=DV, TB=TB)
    hg, c_new, n_new, m_new = pl.pallas_call(
        kern,
        out_shape=(
            jax.ShapeDtypeStruct((R, DV), BF16),
            jax.ShapeDtypeStruct((B, H * DK, DV), F32),
            jax.ShapeDtypeStruct((R, DK), F32),
            jax.ShapeDtypeStruct((R, LANES), F32),
        ),
        grid=(B // TB,),
        in_specs=[
            row_blk(DK), row_blk(DK), row_blk(DV), row_blk(DV),
            row_blk(LANES), row_blk(LANES),
            _resident((RB, LANES)), _resident((RB, LANES)), _resident((RB, DV)),
            state_blk, row_blk(DK), row_blk(LANES),
        ],
        out_specs=(row_blk(DV), state_blk, row_blk(DK), row_blk(LANES)),
        compiler_params=_params(1),
        name="mlstm_sample",
    )(*operands)
    return (hg.reshape(B, v_w), c_new.reshape(B, H, DK, DV), n_new.reshape(B, H, DK),
            m_new[:, 0].reshape(B, H))


def _ffn_kernel(h_ref, a_ref, wa_ref, gn_ref, wgu_ref, wd_ref, gf_ref, o_ref, *, d_ff, tf, final_norm, a_tile):
    if a_tile:
        proj = jnp.concatenate(
            [lax.dot_general(a_ref[t], wa_ref[...], (((0,), (0,)), ((), ())), preferred_element_type=F32)
             for t in range(a_ref.shape[0])], axis=0)
    else:
        proj = jnp.dot(a_ref[...], wa_ref[...], preferred_element_type=F32)
    h1 = h_ref[...] + proj
    xn = _rms(h1, gn_ref[...]).astype(BF16)
    acc = h1
    for c in range(d_ff // tf):
        g = jnp.dot(xn, wgu_ref[:, c * tf:(c + 1) * tf], preferred_element_type=F32)
        u = jnp.dot(xn, wgu_ref[:, d_ff + c * tf:d_ff + (c + 1) * tf], preferred_element_type=F32)
        act = (g * jax.nn.sigmoid(g) * u).astype(BF16)
        acc = acc + jnp.dot(act, wd_ref[c * tf:(c + 1) * tf, :], preferred_element_type=F32)
    if final_norm:
        acc = _rms(acc, gf_ref[...])
    o_ref[...] = acc


def _mixer_out_ffn(h, a, w_a, g_ffn, w_gu, w_d, layer, g_final, tm, final_norm):
    M, D = h.shape
    KA = w_a.shape[0]
    d_ff = w_d.shape[1]
    a_tile = a.shape[2] if a.ndim == 3 else 0
    kern = functools.partial(_ffn_kernel, d_ff=d_ff, tf=256, final_norm=final_norm, a_tile=a_tile)
    single = pl.Buffered(1)
    a_spec = (pl.BlockSpec((tm // a_tile, KA, a_tile), lambda i: (i, 0, 0)) if a_tile
              else pl.BlockSpec((tm, KA), lambda i: (i, 0)))
    return pl.pallas_call(
        kern,
        out_shape=jax.ShapeDtypeStruct((M, D), F32),
        grid=(M // tm,),
        in_specs=[
            pl.BlockSpec((tm, D), lambda i: (i, 0)),
            a_spec,
            pl.BlockSpec((KA, D), lambda i: (0, 0), pipeline_mode=single),
            _resident((1, D)),
            pl.BlockSpec((None, D, 2 * d_ff), lambda i: (layer, 0, 0), pipeline_mode=single),
            pl.BlockSpec((None, d_ff, D), lambda i: (layer, 0, 0), pipeline_mode=single),
            _resident((1, D)),
        ],
        out_specs=pl.BlockSpec((tm, D), lambda i: (i, 0)),
        compiler_params=_params(1),
        name="mixer_out_ffn",
    )(h, a, w_a, g_ffn, w_gu, w_d, g_final)


def _rope_slot(x, cos, sin):
    return x * cos + pltpu.roll(x, 32, 1) * sin


def _mla_proj_kernel(h_ref, gn_ref, win_ref, gq_ref, gkv_ref, wuq_ref, wukt_ref,
                     cos_ref, sin_ref,
                     q_ref, kcat_ref, lat_ref, kr_ref, *maybe_vt_ref, H, vt_tile):
    xn = _rms(h_ref[...], gn_ref[...]).astype(BF16)
    t = jnp.dot(xn, win_ref[...], preferred_element_type=F32)
    c_q = t[:, :Q_LORA]
    c_kv = t[:, Q_LORA:Q_LORA + KV_LORA]
    k_slot = t[:, Q_LORA + KV_LORA:]
    cos, sin = cos_ref[...], sin_ref[...]

    lat = _rms(c_kv, gkv_ref[...])
    k_rot = _rope_slot(k_slot, cos, sin)
    lat_ref[...] = lat
    kr_ref[...] = k_rot[:, :QK_ROPE]
    kcat_ref[:, :KV_LORA] = lat.astype(BF16)
    kcat_ref[:, KV_LORA:] = k_rot.astype(BF16)
    if vt_tile:
        (vt_ref,) = maybe_vt_ref
        for t in range(lat.shape[0] // vt_tile):
            vt_ref[t] = lat[t * vt_tile:(t + 1) * vt_tile, :].T.astype(BF16)

    cqn = _rms(c_q, gq_ref[...]).astype(BF16)
    tm = cqn.shape[0]

    def store_q(h, lanes, val):
        if vt_tile:
            sub = vt_tile // 2
            for t in range(tm // vt_tile):
                for half in range(2):
                    src = slice(t * vt_tile + half * sub, t * vt_tile + (half + 1) * sub)
                    dst = slice((half * H + h) * sub, (half * H + h + 1) * sub)
                    q_ref[t, dst, lanes] = val[src].astype(BF16)
        else:
            q_ref[:, slice(h * Q_SLOT + lanes.start, h * Q_SLOT + lanes.stop)] = val.astype(BF16)

    q_rope = jnp.dot(cqn, wuq_ref[:, H * QK_NOPE:], preferred_element_type=F32)
    for h in range(H):
        store_q(h, slice(KV_LORA, Q_SLOT), _rope_slot(q_rope[:, h * LANES:(h + 1) * LANES], cos, sin))
    q_nope = jnp.dot(cqn, wuq_ref[:, :H * QK_NOPE], preferred_element_type=F32)
    for h in range(H):
        q_lat = jnp.dot(q_nope[:, h * QK_NOPE:(h + 1) * QK_NOPE].astype(BF16), wukt_ref[h],
                        preferred_element_type=F32)
        store_q(h, slice(0, KV_LORA), q_lat)


def _mla_project(h, g_norm, w_in, g_q, g_kv, w_uq, w_ukt, cos, sin, tm, table_blocks, vt_tile):
    M, D = h.shape
    H = MLA_HEADS
    kern = functools.partial(_mla_proj_kernel, H=H, vt_tile=vt_tile)
    table = pl.BlockSpec((tm, LANES), lambda i: (i % table_blocks, 0))
    out_shape = [
        jax.ShapeDtypeStruct((M, H * Q_SLOT), BF16),
        jax.ShapeDtypeStruct((M, Q_SLOT), BF16),
        jax.ShapeDtypeStruct((M, KV_LORA), F32),
        jax.ShapeDtypeStruct((M, QK_ROPE), F32),
    ]
    out_specs = [
        pl.BlockSpec((tm, H * Q_SLOT), lambda i: (i, 0)),
        pl.BlockSpec((tm, Q_SLOT), lambda i: (i, 0)),
        pl.BlockSpec((tm, KV_LORA), lambda i: (i, 0)),
        pl.BlockSpec((tm, QK_ROPE), lambda i: (i, 0)),
    ]
    if vt_tile:
        out_shape[0] = jax.ShapeDtypeStruct((M // vt_tile, H * vt_tile, Q_SLOT), BF16)
        out_specs[0] = pl.BlockSpec((tm // vt_tile, H * vt_tile, Q_SLOT), lambda i: (i, 0, 0))
        out_shape.append(jax.ShapeDtypeStruct((M // vt_tile, KV_LORA, vt_tile), BF16))
        out_specs.append(pl.BlockSpec((tm // vt_tile, KV_LORA, vt_tile), lambda i: (i, 0, 0)))
    return pl.pallas_call(
        kern,
        out_shape=tuple(out_shape),
        grid=(M // tm,),
        in_specs=[
            pl.BlockSpec((tm, D), lambda i: (i, 0)),
            _resident((1, D)),
            _resident(w_in.shape),
            _resident((1, Q_LORA)),
            _resident((1, KV_LORA)),
            _resident(w_uq.shape),
            _resident(w_ukt.shape),
            table, table,
        ],
        out_specs=tuple(out_specs),
        compiler_params=_params(1),
        name="mla_project",
    )(h, g_norm, w_in, g_q, g_kv, w_uq, w_ukt, cos, sin)


def _attn_kernel(q_ref, k_ref, vt_ref, wuvt_ref, o_ref, m_ref, l_ref, acc_ref, sa_ref, sb_ref, *, H, T):
    i = pl.program_id(1)

    def scores(j):
        kj = k_ref[0, pl.ds(pl.multiple_of(j * T, T), T), :]
        return lax.dot_general(kj, q_ref[0], NT_DIMS, preferred_element_type=F32) * MLA_SCALE

    def update(j, st):
        vtj = vt_ref[j]
        m_prev = m_ref[...]
        m_new = jnp.maximum(m_prev, jnp.max(st, axis=0, keepdims=True))
        alpha = jnp.exp(m_prev - m_new)
        p = jnp.exp(st - m_new)
        l_ref[...] = alpha * l_ref[...] + jnp.sum(p, axis=0, keepdims=True)
        acc_ref[...] = alpha * acc_ref[...] + jnp.dot(vtj, p.astype(BF16), preferred_element_type=F32)
        m_ref[...] = m_new

    SUB, C2 = T // 2, H * T // 2
    base = pl.multiple_of(i * T, T)
    k_a = k_ref[0, pl.ds(base, SUB), :]
    k_b = k_ref[0, pl.ds(base + SUB, SUB), :]
    vt_i = vt_ref[i]
    vt_a, vt_b = vt_i[:, :SUB], vt_i[:, SUB:]
    s_a = lax.dot_general(k_a, q_ref[0], NT_DIMS, preferred_element_type=F32) * MLA_SCALE
    s_b = lax.dot_general(k_b, q_ref[0, C2:, :], NT_DIMS, preferred_element_type=F32) * MLA_SCALE
    key = lax.broadcasted_iota(jnp.int32, (SUB, C2), 0)
    qry = lax.broadcasted_iota(jnp.int32, (SUB, C2), 1) & (SUB - 1)
    causal = key <= qry
    s_a0 = jnp.where(causal, s_a[:, :C2], -jnp.inf)
    s_a1 = s_a[:, C2:]
    s_b1 = jnp.where(causal, s_b, -jnp.inf)
    m_0 = jnp.max(s_a0, axis=0, keepdims=True)
    m_1 = jnp.maximum(jnp.max(s_a1, axis=0, keepdims=True), jnp.max(s_b1, axis=0, keepdims=True))
    p_a0, p_a1, p_b1 = jnp.exp(s_a0 - m_0), jnp.exp(s_a1 - m_1), jnp.exp(s_b1 - m_1)
    m_ref[:, :C2] = m_0
    m_ref[:, C2:] = m_1
    l_ref[:, :C2] = jnp.sum(p_a0, axis=0, keepdims=True)
    l_ref[:, C2:] = jnp.sum(p_a1, axis=0, keepdims=True) + jnp.sum(p_b1, axis=0, keepdims=True)
    acc_ref[:, :C2] = jnp.dot(vt_a, p_a0.astype(BF16), preferred_element_type=F32)
    acc_ref[:, C2:] = (jnp.dot(vt_a, p_a1.astype(BF16), preferred_element_type=F32)
                       + jnp.dot(vt_b, p_b1.astype(BF16), preferred_element_type=F32))

    @pl.when(i > 0)
    def _():
        sa_ref[...] = scores(0)

    def pair(jj, carry):
        j = 2 * jj
        sb_ref[...] = scores(j + 1)
        update(j, sa_ref[...])
        sa_ref[...] = scores(jnp.minimum(j + 2, i - 1))
        update(j + 1, sb_ref[...])
        return carry

    lax.fori_loop(0, i // 2, pair, 0)

    @pl.when(i % 2 == 1)
    def _():
        update(i - 1, sa_ref[...])

    o_t = (acc_ref[...] / l_ref[...]).astype(BF16)
    for half in range(2):
        for h in range(H):
            cols = slice((half * H + h) * SUB, (half * H + h + 1) * SUB)
            v_t = jnp.dot(wuvt_ref[h], o_t[:, cols], preferred_element_type=F32)
            o_ref[0, h * V_DIM:(h + 1) * V_DIM, half * SUB:(half + 1) * SUB] = v_t.astype(BF16)


def _attention_prompt(q, kcat, vt, w_uvt, B, S, T):
    H = MLA_HEADS
    NQ = S // T
    kern = functools.partial(_attn_kernel, H=H, T=T)
    return pl.pallas_call(
        kern,
        out_shape=jax.ShapeDtypeStruct((B * NQ, H * V_DIM, T), BF16),
        grid=(B, NQ),
        in_specs=[
            pl.BlockSpec((1, H * T, Q_SLOT), lambda b, i: (b * NQ + i, 0, 0)),
            pl.BlockSpec((1, S, Q_SLOT), lambda b, i: (b, 0, 0)),
            pl.BlockSpec((NQ, KV_LORA, T), lambda b, i: (b, 0, 0)),
            _resident(w_uvt.shape),
        ],
        out_specs=pl.BlockSpec((1, H * V_DIM, T), lambda b, i: (b * NQ + i, 0, 0)),
        scratch_shapes=[
            pltpu.VMEM((1, H * T), F32),
            pltpu.VMEM((1, H * T), F32),
            pltpu.VMEM((KV_LORA, H * T), F32),
            pltpu.VMEM((T, H * T), F32),
            pltpu.VMEM((T, H * T), F32),
        ],
        compiler_params=_params(2),
        name="attention_prompt",
    )(q, kcat.reshape(B, S, Q_SLOT), vt, w_uvt)


def _decode_mlstm_kernel(pt_ref,
                         q_ref, cn_ref, krn_ref, lat_hbm, krt_hbm,
                         mq_ref, mk_ref, mv_ref, mo_ref, gt_ref, bias_ref, gh_ref,
                         o_ref, hg_ref, c_out_ref, n_out_ref, m_out_ref,
                         lat_buf, kr_buf, sem, caug_ref, mst_ref,
                         *, layer, G, P, NCH, RING, STREAMS, L, H_ML, DK, DV, NC):
    s = pl.program_id(0)
    ns = pl.num_programs(0)
    total = ns * NCH
    ml = dict(L=L, H=H_ML, DK=DK, DV=DV)

    def page_copies(c):
        slot = lax.rem(c, RING)
        cw = jnp.where(c >= total, c - total, c)
        bb, jj = lax.div(cw, NCH), lax.rem(cw, NCH)
        copies = []
        for g in range(G):
            page = pt_ref[bb, jj * G + g]
            copies.append(pltpu.make_async_copy(
                lat_hbm.at[layer, page], lat_buf.at[slot, pl.ds(g * P, P), :], sem.at[0, slot]))
            copies.append(pltpu.make_async_copy(
                krt_hbm.at[layer, page], kr_buf.at[slot, :, pl.ds(g * P, P)], sem.at[1, slot]))
        return copies

    def start(c):
        for n, cp in enumerate(page_copies(c)):
            cp.start(priority=n % 2)

    def wait(c):
        for cp in page_copies(c):
            cp.wait()

    @pl.when(s == 0)
    def _():
        for c in range(RING - 1):
            start(jnp.int32(c))

    ml_chunk = lax.rem(s, NC)

    @pl.when(ml_chunk == 0)
    def _():
        _mlstm_reset(caug_ref, mst_ref)

    q = q_ref[0].astype(F32)
    q_lat = q[:, :KV_LORA]
    q_rope = q[:, KV_LORA:KV_LORA + QK_ROPE]
    c_new = cn_ref[0]
    kr_new = krn_ref[0]

    s_new = (jnp.sum(q_lat * c_new, axis=1, keepdims=True)
             + jnp.sum(q_rope * kr_new, axis=1, keepdims=True)) * MLA_SCALE
    n_heads = q.shape[0]
    m_run = [s_new] + [jnp.full_like(s_new, -jnp.inf)] * (STREAMS - 1)
    l_run = [jnp.ones_like(s_new)] + [jnp.zeros_like(s_new)] * (STREAMS - 1)
    acc = ([jnp.broadcast_to(c_new, (n_heads, KV_LORA)).astype(F32)]
           + [jnp.zeros((n_heads, KV_LORA), F32)] * (STREAMS - 1))
    W = (G * P) // STREAMS

    ml_pieces = _mlstm_chunk_phases(mq_ref, mk_ref, mv_ref, mo_ref, gt_ref, bias_ref, gh_ref, hg_ref,
                                    caug_ref, mst_ref, **ml)
    assert NCH >= 2
    for j in range(NCH):
        c = s * NCH + j
        slot = lax.rem(c, RING)
        wait(c)
        kl = [lat_buf[slot, i * W:(i + 1) * W, :] for i in range(STREAMS)]
        sc = [(lax.dot_general(q_lat, kl[i], NT_DIMS, preferred_element_type=F32)
               + jnp.dot(q_rope, kr_buf[slot, :, i * W:(i + 1) * W], preferred_element_type=F32)) * MLA_SCALE
              for i in range(STREAMS)]
        if j in (0, NCH - 1):
            next(ml_pieces)
        for i in range(STREAMS):
            m_new = jnp.maximum(m_run[i], jnp.max(sc[i], axis=1, keepdims=True))
            alpha = jnp.exp(m_run[i] - m_new)
            p = jnp.exp(sc[i] - m_new)
            m_run[i] = m_new
            l_run[i] = alpha * l_run[i] + jnp.sum(p, axis=1, keepdims=True)
            acc[i] = alpha * acc[i] + jnp.dot(p, kl[i], preferred_element_type=F32)
        if j in (0, NCH - 1):
            next(ml_pieces, None)
        start(c + (RING - 1))

    m_all = functools.reduce(jnp.maximum, m_run)
    scale = [jnp.exp(m_i - m_all) for m_i in m_run]
    l_all = sum(l_i * w_i for l_i, w_i in zip(l_run, scale))
    acc_all = sum(a_i * w_i for a_i, w_i in zip(acc, scale))
    o_ref[0] = acc_all / l_all

    @pl.when(ml_chunk == NC - 1)
    def _():
        _mlstm_write_state(caug_ref, mst_ref, c_out_ref, n_out_ref, m_out_ref, H=H_ML, DK=DK, DV=DV)

    @pl.when(s == ns - 1)
    def _():
        for c in range(RING - 1):
            wait(total + c)


def _attention_sample_with_mlstm_prompt(q3, c_new, kr_new, cache_latent, cache_k_rope_t, page_table, layer, G,
                                        qkv, og, bias, g_head, B, S, L):
    BS, H, _ = q3.shape
    n_pages = page_table.shape[1]
    P = cache_latent.shape[2]
    NCH = n_pages // G
    assert n_pages % G == 0
    RING = 3
    H_ML, DK, DV = ML_HEADS, 64, 128
    NC = S // L
    assert B * NC == BS, "one mLSTM chunk per decode sequence"
    qk_w, v_w = H_ML * DK, H_ML * DV
    gate_blk = v_w // LANES
    kern = functools.partial(_decode_mlstm_kernel, layer=layer, G=G, P=P, NCH=NCH, RING=RING, STREAMS=2,
                             L=L, H_ML=H_ML, DK=DK, DV=DV, NC=NC)
    grid_spec = pltpu.PrefetchScalarGridSpec(
        num_scalar_prefetch=1,
        grid=(BS,),
        in_specs=[
            pl.BlockSpec((1, H, Q_SLOT), lambda s, pt: (s, 0, 0)),
            pl.BlockSpec((1, 1, KV_LORA), lambda s, pt: (s, 0, 0)),
            pl.BlockSpec((1, 1, QK_ROPE), lambda s, pt: (s, 0, 0)),
            pl.BlockSpec(memory_space=pl.ANY),
            pl.BlockSpec(memory_space=pl.ANY),
            pl.BlockSpec((L, qk_w), lambda s, pt: (s, 0)),
            pl.BlockSpec((L, qk_w), lambda s, pt: (s, 1)),
            pl.BlockSpec((L, v_w), lambda s, pt: (s, 1)),
            pl.BlockSpec((L, v_w), lambda s, pt: (s, 0)),
            pl.BlockSpec((L, LANES), lambda s, pt: (s, gate_blk)),
            pl.BlockSpec((1, LANES), lambda s, pt: (0, 0)),
            pl.BlockSpec((1, v_w), lambda s, pt: (0, 0)),
        ],
        out_specs=(
            pl.BlockSpec((1, H, KV_LORA), lambda s, pt: (s, 0, 0)),
            pl.BlockSpec((L, v_w), lambda s, pt: (s, 0)),
            pl.BlockSpec((1, H_ML, DK, DV), lambda s, pt: (s // NC, 0, 0, 0)),
            pl.BlockSpec((1, H_ML, DK), lambda s, pt: (s // NC, 0, 0)),
            pl.BlockSpec((1, 1, H_ML), lambda s, pt: (s // NC, 0, 0)),
        ),
        scratch_shapes=[
            pltpu.VMEM((RING, G * P, KV_LORA), F32),
            pltpu.VMEM((RING, QK_ROPE, G * P), F32),
            pltpu.SemaphoreType.DMA((2, RING)),
            pltpu.VMEM((H_ML // 2, 2 * DK, 2 * DV), F32),
            pltpu.VMEM((H_ML, LANES), F32),
        ],
    )
    return pl.pallas_call(
        kern,
        out_shape=(
            jax.ShapeDtypeStruct((BS, H, KV_LORA), F32),
            jax.ShapeDtypeStruct((B * S, v_w), BF16),
            jax.ShapeDtypeStruct((B, H_ML, DK, DV), F32),
            jax.ShapeDtypeStruct((B, H_ML, DK), F32),
            jax.ShapeDtypeStruct((B, 1, H_ML), F32),
        ),
        grid_spec=grid_spec,
        compiler_params=_params(1),
        name="attention_sample_mlstm_prompt",
    )(page_table, q3, c_new.reshape(BS, 1, KV_LORA), kr_new.reshape(BS, 1, QK_ROPE),
      cache_latent, cache_k_rope_t, qkv, qkv, qkv, og, og, bias, g_head)


def _value_up_kernel(o_ref, wuv_ref, v_ref, *, H):
    for h in range(H):
        o_h = o_ref[:, h * KV_LORA:(h + 1) * KV_LORA].astype(BF16)
        v_ref[:, h * V_DIM:(h + 1) * V_DIM] = jnp.dot(
            o_h, wuv_ref[h], preferred_element_type=F32).astype(BF16)


def _value_up(o_lat, w_uv):
    M = o_lat.shape[0]
    H = MLA_HEADS
    return pl.pallas_call(
        functools.partial(_value_up_kernel, H=H),
        out_shape=jax.ShapeDtypeStruct((M, H * V_DIM), BF16),
        grid=(1,),
        in_specs=[_resident(o_lat.shape), _resident(w_uv.shape)],
        out_specs=_resident((M, H * V_DIM)),
        compiler_params=_params(1),
        name="value_up",
    )(o_lat, w_uv)


def _rope_tables(pos):
    inv = ROPE_THETA ** (-jnp.arange(0, QK_ROPE, 2, dtype=F32) / QK_ROPE)
    ang = pos[:, None] * inv[None, :]
    cos, sin = jnp.cos(ang), jnp.sin(ang)
    z = jnp.zeros_like(cos)
    return jnp.concatenate([cos, cos, z, z], axis=1), jnp.concatenate([-sin, sin, z, z], axis=1)


def _pad_cols(w, n):
    return jnp.pad(w, ((0, 0), (0, n - w.shape[1])))


def kernel(x_prompt, x_sample, state_mlstm_C, state_mlstm_n, state_mlstm_m, cache_latent, cache_k_rope,
           page_table, norm_mix, norm_ffn, norm_final, mlstm_w_in, mlstm_b_gates, mlstm_g_head, mlstm_w_out,
           mla_w_in, mla_g_q, mla_g_kv, mla_w_uq, mla_w_uk, mla_w_uv, mla_w_o, ffn_w_gate_up, ffn_w_down):
    B, S, D = x_prompt.shape
    BS, T, _ = x_sample.shape
    assert T == 1, "sample group is one new token per sequence"
    depth = norm_mix.shape[0]
    H = MLA_HEADS
    past_len = page_table.shape[1] * cache_latent.shape[2]

    hp = x_prompt.reshape(B * S, D)
    hs = x_sample.reshape(BS, D)
    row = lambda v: v.reshape(1, -1).astype(F32)

    ROW_TILE = 1024
    ML_CHUNK = 256
    T_ATT = 512
    PAGES_PER_CHUNK = 64
    SAMPLE_SEQS_PER_STEP = 16

    pos_p = jnp.arange(S, dtype=F32)
    pos_s = jnp.broadcast_to(jnp.arange(T, dtype=F32) + past_len, (BS,))
    rope_p = _rope_tables(pos_p)
    rope_s = _rope_tables(pos_s)

    assert depth == 2, "schedule below is written for one mLSTM layer followed by one MLA layer"
    g_fin = row(norm_final)

    n_gates = 2 * ML_HEADS
    ml_w_in = _pad_cols(mlstm_w_in[0], mlstm_w_in.shape[2] - n_gates + LANES).astype(BF16)
    ml_bias = _pad_cols(mlstm_b_gates[0].reshape(1, -1), LANES).astype(F32)
    ml_g_head = row(mlstm_g_head[0])
    ml_w_out = mlstm_w_out[0].astype(BF16)
    mla_in = jnp.concatenate([mla_w_in[0], mla_w_in[0][:, Q_LORA + KV_LORA:]], axis=1).astype(BF16)
    wq = mla_w_uq[0].reshape(Q_LORA, H, QK_NOPE + QK_ROPE)
    wq_nope = wq[:, :, :QK_NOPE].reshape(Q_LORA, H * QK_NOPE)
    wq_rope = jnp.concatenate([wq[:, :, QK_NOPE:]] * 2, axis=2).reshape(Q_LORA, H * LANES)
    w_uq = jnp.concatenate([wq_nope, wq_rope], axis=1).astype(BF16)
    w_ukt = jnp.transpose(mla_w_uk[0].reshape(KV_LORA, H, QK_NOPE), (1, 2, 0)).astype(BF16)
    w_uv = jnp.transpose(mla_w_uv[0].reshape(KV_LORA, H, V_DIM), (1, 0, 2)).astype(BF16)
    w_uvt = jnp.transpose(mla_w_uv[0].reshape(KV_LORA, H, V_DIM), (1, 2, 0)).astype(BF16)
    w_o = mla_w_o[0].astype(BF16)
    g_q, g_kv = row(mla_g_q[0]), row(mla_g_kv[0])
    mla_weights = (mla_in, g_q, g_kv, w_uq, w_ukt)
    w_gu_all, w_d_all = ffn_w_gate_up.astype(BF16), ffn_w_down.astype(BF16)
    ffn = [(row(norm_ffn[l]), w_gu_all, w_d_all, l) for l in range(depth)]
    g_mix = [row(norm_mix[l]) for l in range(depth)]
    cache_k_rope_t = jnp.swapaxes(cache_k_rope, 2, 3)

    n_qkv = 2 * ML_HEADS * 64 + ML_HEADS * 128
    qkv_s, og_s = _norm_matmul(hs, g_mix[0], ml_w_in, BS, n_qkv)
    a_s, C_s, n_s, m_s = _mlstm_sample(qkv_s, og_s, mlstm_b_gates[0].astype(F32), ml_g_head, state_mlstm_C[0],
                                       state_mlstm_n[0], state_mlstm_m[0], TB=SAMPLE_SEQS_PER_STEP)
    hs = _mixer_out_ffn(hs, a_s, ml_w_out, *ffn[0], g_fin, BS, final_norm=False)
    q_s, _, lat_s, kr_s = _mla_project(hs, g_mix[1], *mla_weights, *rope_s, tm=BS, table_blocks=1, vt_tile=0)

    qkv_p, og_p = _norm_matmul(hp, g_mix[0], ml_w_in, ROW_TILE, n_qkv)
    o_s, a_p, C_p, n_p, m_p = _attention_sample_with_mlstm_prompt(
        q_s.reshape(BS, H, Q_SLOT), lat_s, kr_s, cache_latent, cache_k_rope_t, page_table, layer=0,
        G=PAGES_PER_CHUNK, qkv=qkv_p, og=og_p, bias=ml_bias, g_head=ml_g_head, B=B, S=S, L=ML_CHUNK)
    hp = _mixer_out_ffn(hp, a_p, ml_w_out, *ffn[0], g_fin, ROW_TILE, final_norm=False)

    q_p, kcat_p, lat_p, kr_p, vt_p = _mla_project(hp, g_mix[1], *mla_weights, *rope_p,
                                                  tm=ROW_TILE, table_blocks=S // ROW_TILE, vt_tile=T_ATT)
    a_p = _attention_prompt(q_p, kcat_p, vt_p, w_uvt, B, S, T=T_ATT)
    hp = _mixer_out_ffn(hp, a_p, w_o, *ffn[1], g_fin, ROW_TILE, final_norm=True)

    a_s = _value_up(o_s.reshape(BS, H * KV_LORA), w_uv)
    hs = _mixer_out_ffn(hs, a_s, w_o, *ffn[1], g_fin, BS, final_norm=True)

    return (hp.reshape(B, S, D), hs.reshape(BS, T, D),
            C_p[None], n_p[None], m_p.reshape(1, B, -1),
            C_s[None], n_s[None], m_s.reshape(1, BS, -1),
            lat_p.reshape(1, B, S, KV_LORA), kr_p.reshape(1, B, S, QK_ROPE),
            lat_s.reshape(1, BS, T, KV_LORA), kr_s.reshape(1, BS, T, QK_ROPE))
```
